```python
import math
import jax, jax.numpy as jnp
from jax import lax
import numpy as np

D_MODEL = 2048
BATCH = 2
SEQ = 16384
DEPTH = 4
DEC_BATCH = 8
DEC_SEQ = 32
PAST_LEN = 4096

CHUNK = 64
QBLOCK = 128
ROPE_THETA = 10000.0
NORM_EPS = 1e-6
N_MIXERS = 3
N_A = (DEPTH + 2) // 3
N_B = (DEPTH + 1) // 3
N_C = DEPTH // 3

MLA_HEADS = 16
MLA_Q_RANK = 512
MLA_KV_RANK = 512
MLA_NOPE = 128
MLA_ROPE = 64
MLA_V = 128
SWA_HEADS = 32
SWA_KV_HEADS = 4
SWA_GROUP = SWA_HEADS // SWA_KV_HEADS
SWA_HEAD_DIM = 64
WINDOW = 128
WIN_CHUNKS = WINDOW // CHUNK
DIFF_HEADS = 8
DIFF_KV_HEADS = 4
DIFF_GROUP = DIFF_HEADS // DIFF_KV_HEADS
DIFF_HEAD_DIM = 128
SUBLN_EPS = 1e-5
D_FF = 5632
CONV_W = 3

kernel_name = "chunk_streaming_hybrid_mla_swa_diff_convffn_step"


def rmsnorm(x, g, eps=NORM_EPS):
    xf = x.astype(jnp.float32)
    y = xf * lax.rsqrt(jnp.mean(xf * xf, axis=-1, keepdims=True) + eps)
    return (y * g.astype(jnp.float32)).astype(x.dtype)


def rope_tables(length, offset, dim):
    pos = jnp.arange(length, dtype=jnp.float32) + offset
    inv = ROPE_THETA ** (-jnp.arange(0, dim, 2, dtype=jnp.float32) / dim)
    ang = pos[:, None] * inv[None, :]
    return jnp.cos(ang), jnp.sin(ang)


def apply_rope(x, cos, sin):
    half = x.shape[-1] // 2
    shape = (cos.shape[0],) + (1,) * (x.ndim - 3) + (half,)
    c, s = cos.reshape(shape), sin.reshape(shape)
    xf = x.astype(jnp.float32)
    x1, x2 = xf[..., :half], xf[..., half:]
    return jnp.concatenate([x1 * c - x2 * s, x2 * c + x1 * s], axis=-1).astype(x.dtype)


def attend(q, k, v, mix, scale, mask):
    s = jnp.einsum('bqhgmd,bkhmd->bhgmqk', q, k).astype(jnp.float32) * scale
    if mask is not None:
        s = jnp.where(mask, s, -jnp.inf)
    p = jax.nn.softmax(s, axis=-1)
    p = jnp.einsum('bhgmqk,m->bhgqk', p, mix)
    return jnp.einsum('bhgqk,bkhe->bqhge', p.astype(v.dtype), v)


def chunk_causal_attention(q, k, v, mix, scale):
    b, s = q.shape[:2]
    nb = s // QBLOCK
    qb = jnp.moveaxis(q.reshape(b, nb, QBLOCK, *q.shape[2:]), 1, 0)
    kchunk = jnp.arange(s) // CHUNK

    def block(args):
        qi, bi = args
        qchunk = (bi * QBLOCK + jnp.arange(QBLOCK)) // CHUNK
        mask = kchunk[None, :] <= qchunk[:, None]
        return attend(qi, k, v, mix, scale, mask)

    out = lax.map(block, (qb, jnp.arange(nb)))
    return jnp.moveaxis(out, 0, 1).reshape(b, s, *out.shape[3:])


def sink_attend(q, k, v, sinks, scale, mask):
    hk, g = q.shape[3], q.shape[4]
    s = jnp.einsum('bcqhgd,bckhd->bchgqk', q, k).astype(jnp.float32) * scale
    if mask is not None:
        s = jnp.where(mask[None, :, None, None, None, :], s, -jnp.inf)
    sink = jnp.broadcast_to(sinks.astype(jnp.float32).reshape(1, 1, hk, g, 1, 1), s.shape[:-1] + (1,))
    p = jax.nn.softmax(jnp.concatenate([s, sink], axis=-1), axis=-1)[..., :-1]
    return jnp.einsum('bchgqk,bckhd->bcqhgd', p.astype(v.dtype), v)


def banded_window_attention(q, k, v, sinks, scale):
    b, s = q.shape[:2]
    n_c = s // CHUNK
    pad = WIN_CHUNKS * CHUNK
    band_len = (WIN_CHUNKS + 1) * CHUNK

    def band(t):
        tp = jnp.pad(t, ((0, 0), (pad, 0), (0, 0), (0, 0)))
        tp = tp.reshape(b, n_c + WIN_CHUNKS, CHUNK, *t.shape[2:])
        return jnp.concatenate([tp[:, j:j + n_c] for j in range(WIN_CHUNKS + 1)], axis=2)

    kb, vb = band(k), band(v)
    kpos = jnp.arange(n_c)[:, None] * CHUNK - pad + jnp.arange(band_len)[None, :]
    valid = kpos >= 0
    cpb = QBLOCK // CHUNK
    n_g = n_c // cpb

    def grp(t):
        return jnp.moveaxis(t.reshape(b, n_g, cpb, *t.shape[2:]), 1, 0)

    qg = grp(q.reshape(b, n_c, CHUNK, *q.shape[2:]))
    out = lax.map(lambda a: sink_attend(a[0], a[1], a[2], sinks, scale, a[3]),
                  (qg, grp(kb), grp(vb), valid.reshape(n_g, cpb, band_len)))
    return jnp.moveaxis(out, 0, 1).reshape(b, s, *q.shape[2:])


def mla_project(h, offset, w_dq, q_norm, w_uq, w_dkv, kv_norm):
    b, l, _ = h.shape
    cos, sin = rope_tables(l, offset, MLA_ROPE)
    q = (rmsnorm(h @ w_dq, q_norm) @ w_uq).reshape(b, l, MLA_HEADS, MLA_NOPE + MLA_ROPE)
    q = jnp.concatenate([q[..., :MLA_NOPE], apply_rope(q[..., MLA_NOPE:], cos, sin)], axis=-1)
    kv_a = h @ w_dkv
    c_kv = rmsnorm(kv_a[..., :MLA_KV_RANK], kv_norm)
    k_pe = apply_rope(kv_a[..., MLA_KV_RANK:], cos, sin)
    return q[:, :, :, None, None, :], c_kv, k_pe


def mla_expand(c_kv, k_pe, w_ukv):
    b, l, _ = c_kv.shape
    kv = (c_kv @ w_ukv).reshape(b, l, MLA_HEADS, MLA_NOPE + MLA_V)
    k = jnp.concatenate([kv[..., :MLA_NOPE],
                         jnp.broadcast_to(k_pe[:, :, None, :], (b, l, MLA_HEADS, MLA_ROPE))], axis=-1)
    return k[:, :, :, None, :], kv[..., MLA_NOPE:]


def swa_project(h, offset, w_qkv):
    b, l, _ = h.shape
    qw, kw = SWA_HEADS * SWA_HEAD_DIM, SWA_KV_HEADS * SWA_HEAD_DIM
    qkv = h @ w_qkv
    cos, sin = rope_tables(l, offset, SWA_HEAD_DIM)
    q = apply_rope(qkv[..., :qw].reshape(b, l, SWA_KV_HEADS, SWA_GROUP, SWA_HEAD_DIM), cos, sin)
    k = apply_rope(qkv[..., qw:qw + kw].reshape(b, l, SWA_KV_HEADS, SWA_HEAD_DIM), cos, sin)
    v = qkv[..., qw + kw:].reshape(b, l, SWA_KV_HEADS, SWA_HEAD_DIM)
    return q, k, v


def diff_project(h, offset, w_qkv):
    b, l, _ = h.shape
    qw = DIFF_HEADS * 2 * DIFF_HEAD_DIM
    kw = DIFF_KV_HEADS * 2 * DIFF_HEAD_DIM
    qkv = h @ w_qkv
    cos, sin = rope_tables(l, offset, DIFF_HEAD_DIM)
    q = apply_rope(qkv[..., :qw].reshape(b, l, DIFF_KV_HEADS, DIFF_GROUP, 2, DIFF_HEAD_DIM), cos, sin)
    k = apply_rope(qkv[..., qw:qw + kw].reshape(b, l, DIFF_KV_HEADS, 2, DIFF_HEAD_DIM), cos, sin)
    v = qkv[..., qw + kw:].reshape(b, l, DIFF_KV_HEADS, 2 * DIFF_HEAD_DIM)
    return q, k, v


def diff_mix(lq1, lk1, lq2, lk2, lam_init):
    f32 = jnp.float32
    lam = (jnp.exp(jnp.sum(lq1.astype(f32) * lk1.astype(f32)))
           - jnp.exp(jnp.sum(lq2.astype(f32) * lk2.astype(f32))) + lam_init)
    return jnp.stack([jnp.ones((), f32), -lam])


def diff_out(o, subln, lam_init, w_o):
    b, l = o.shape[:2]
    o = rmsnorm(o, subln, SUBLN_EPS) * (1.0 - lam_init)
    return o.reshape(b, l, -1) @ w_o


def conv_ffn(h, prev, w_gate, w_up, conv_w, conv_b, w_down):
    l = h.shape[1]
    gate = h @ w_gate
    ext = jnp.concatenate([prev.astype(gate.dtype), gate], axis=1)
    conv = conv_b
    for j in range(CONV_W):
        conv = conv + ext[:, j:j + l] * conv_w[j]
    out = (jax.nn.silu(conv) * (h @ w_up)) @ w_down
    return out, ext[:, l:]


def setup_inputs(seed: int = 0) -> dict:
    key = jax.random.key(seed)
    ks = iter(jax.random.split(key, 64))
    f32 = jnp.float32

    def nrm(shape, scale=1.0):
        return jax.random.normal(next(ks), shape, f32) * scale

    def w(*shape):
        return nrm(shape, shape[-2] ** -0.5)

    def gain(*shape):
        return 1.0 + nrm(shape, 0.01)

    return {
        'x_prompt': nrm((BATCH, SEQ, D_MODEL)),
        'x_sample': nrm((DEC_BATCH, DEC_SEQ, D_MODEL)),
        'cache_mla_ckv': nrm((N_A, DEC_BATCH, PAST_LEN, MLA_KV_RANK)),
        'cache_mla_kpe': nrm((N_A, DEC_BATCH, PAST_LEN, MLA_ROPE)),
        'cache_swa_k': nrm((N_B, DEC_BATCH, WINDOW, SWA_KV_HEADS, SWA_HEAD_DIM)),
        'cache_swa_v': nrm((N_B, DEC_BATCH, WINDOW, SWA_KV_HEADS, SWA_HEAD_DIM)),
        'cache_diff_k': nrm((N_C, DEC_BATCH, PAST_LEN, DIFF_KV_HEADS, 2, DIFF_HEAD_DIM)),
        'cache_diff_v': nrm((N_C, DEC_BATCH, PAST_LEN, DIFF_KV_HEADS, 2 * DIFF_HEAD_DIM)),
        'state_ffn_conv': nrm((DEPTH, DEC_BATCH, CONV_W - 1, D_FF)),
        'norm_mix': gain(DEPTH, D_MODEL),
        'norm_ffn': gain(DEPTH, D_MODEL),
        'final_norm': gain(D_MODEL),
        'mla_w_dq': w(N_A, D_MODEL, MLA_Q_RANK),
        'mla_q_norm': gain(N_A, MLA_Q_RANK),
        'mla_w_uq': w(N_A, MLA_Q_RANK, MLA_HEADS * (MLA_NOPE + MLA_ROPE)),
        'mla_w_dkv': w(N_A, D_MODEL, MLA_KV_RANK + MLA_ROPE),
        'mla_kv_norm': gain(N_A, MLA_KV_RANK),
        'mla_w_ukv': w(N_A, MLA_KV_RANK, MLA_HEADS * (MLA_NOPE + MLA_V)),
        'mla_w_o': w(N_A, MLA_HEADS * MLA_V, D_MODEL),
        'swa_w_qkv': w(N_B, D_MODEL, (SWA_HEADS + 2 * SWA_KV_HEADS) * SWA_HEAD_DIM),
        'swa_sinks': nrm((N_B, SWA_HEADS), 0.5),
        'swa_w_o': w(N_B, SWA_HEADS * SWA_HEAD_DIM, D_MODEL),
        'diff_w_qkv': w(N_C, D_MODEL, (2 * DIFF_HEADS + 4 * DIFF_KV_HEADS) * DIFF_HEAD_DIM),
        'diff_lambda_q1': nrm((N_C, DIFF_HEAD_DIM), 0.1),
        'diff_lambda_k1': nrm((N_C, DIFF_HEAD_DIM), 0.1),
        'diff_lambda_q2': nrm((N_C, DIFF_HEAD_DIM), 0.1),
        'diff_lambda_k2': nrm((N_C, DIFF_HEAD_DIM), 0.1),
        'diff_subln': gain(N_C, 2 * DIFF_HEAD_DIM),
        'diff_w_o': w(N_C, 2 * DIFF_HEADS * DIFF_HEAD_DIM, D_MODEL),
        'ffn_w_gate': w(DEPTH, D_MODEL, D_FF),
        'ffn_w_up': w(DEPTH, D_MODEL, D_FF),
        'ffn_conv_w': nrm((DEPTH, CONV_W, D_FF), CONV_W ** -0.5),
        'ffn_conv_b': nrm((DEPTH, D_FF), 0.01),
        'ffn_w_down': w(DEPTH, D_FF, D_MODEL),
    }


def reference(x_prompt, x_sample, cache_mla_ckv, cache_mla_kpe, cache_swa_k, cache_swa_v,
              cache_diff_k, cache_diff_v, state_ffn_conv,
              norm_mix, norm_ffn, final_norm,
              mla_w_dq, mla_q_norm, mla_w_uq, mla_w_dkv, mla_kv_norm, mla_w_ukv, mla_w_o,
              swa_w_qkv, swa_sinks, swa_w_o,
              diff_w_qkv, diff_lambda_q1, diff_lambda_k1, diff_lambda_q2, diff_lambda_k2, diff_subln, diff_w_o,
              ffn_w_gate, ffn_w_up, ffn_conv_w, ffn_conv_b, ffn_w_down):
    b = x_prompt.shape[0]
    yp, ys = x_prompt, x_sample
    p_ckv, p_kpe, p_swk, p_swv, p_dk, p_dv, p_conv = [], [], [], [], [], [], []
    s_ckv, s_kpe, s_swk, s_swv, s_dk, s_dv, s_conv = [], [], [], [], [], [], []
    for i in range(DEPTH):
        kind, j = i % N_MIXERS, i // N_MIXERS
        hp = rmsnorm(yp, norm_mix[i])
        hs = rmsnorm(ys, norm_mix[i])
        if kind == 0:
            wts = (mla_w_dq[j], mla_q_norm[j], mla_w_uq[j], mla_w_dkv[j], mla_kv_norm[j])
            one = jnp.ones((1,), jnp.float32)
            scale = (MLA_NOPE + MLA_ROPE) ** -0.5
            qp, cp, kpp = mla_project(hp, 0, *wts)
            kp, vp = mla_expand(cp, kpp, mla_w_ukv[j])
            op = chunk_causal_attention(qp, kp, vp, one, scale)
            qs, cs, kps = mla_project(hs, PAST_LEN, *wts)
            ks, vs = mla_expand(jnp.concatenate([cache_mla_ckv[j], cs], axis=1),
                                jnp.concatenate([cache_mla_kpe[j], kps], axis=1), mla_w_ukv[j])
            os_ = attend(qs, ks, vs, one, scale, None)
            mp = op.reshape(*op.shape[:2], -1) @ mla_w_o[j]
            ms = os_.reshape(*os_.shape[:2], -1) @ mla_w_o[j]
            p_ckv.append(cp); p_kpe.append(kpp); s_ckv.append(cs); s_kpe.append(kps)
        elif kind == 1:
            scale = SWA_HEAD_DIM ** -0.5
            qp, kp, vp = swa_project(hp, 0, swa_w_qkv[j])
            op = banded_window_attention(qp, kp, vp, swa_sinks[j], scale)
            qs, ks, vs = swa_project(hs, PAST_LEN, swa_w_qkv[j])
            k_all = jnp.concatenate([cache_swa_k[j], ks], axis=1)
            v_all = jnp.concatenate([cache_swa_v[j], vs], axis=1)
            os_ = sink_attend(qs[:, None], k_all[:, None], v_all[:, None], swa_sinks[j], scale, None)[:, 0]
            mp = op.reshape(*op.shape[:2], -1) @ swa_w_o[j]
            ms = os_.reshape(*os_.shape[:2], -1) @ swa_w_o[j]
            p_swk.append(kp[:, -WINDOW:]); p_swv.append(vp[:, -WINDOW:])
            s_swk.append(k_all[:, -WINDOW:]); s_swv.append(v_all[:, -WINDOW:])
        else:
            lam_init = 0.8 - 0.6 * math.exp(-0.3 * i)
            mix = diff_mix(diff_lambda_q1[j], diff_lambda_k1[j], diff_lambda_q2[j], diff_lambda_k2[j], lam_init)
            scale = DIFF_HEAD_DIM ** -0.5
            qp, kp, vp = diff_project(hp, 0, diff_w_qkv[j])
            op = chunk_causal_attention(qp, kp, vp, mix, scale)
            qs, ks, vs = diff_project(hs, PAST_LEN, diff_w_qkv[j])
            os_ = attend(qs, jnp.concatenate([cache_diff_k[j], ks], axis=1),
                         jnp.concatenate([cache_diff_v[j], vs], axis=1), mix, scale, None)
            mp = diff_out(op, diff_subln[j], lam_init, diff_w_o[j])
            ms = diff_out(os_, diff_subln[j], lam_init, diff_w_o[j])
            p_dk.append(kp); p_dv.append(vp); s_dk.append(ks); s_dv.append(vs)
        yp = yp + mp
        ys = ys + ms
        ffw = (ffn_w_gate[i], ffn_w_up[i], ffn_conv_w[i], ffn_conv_b[i], ffn_w_down[i])
        fp, cvp = conv_ffn(rmsnorm(yp, norm_ffn[i]), jnp.zeros((b, CONV_W - 1, D_FF), yp.dtype), *ffw)
        fs, cvs = conv_ffn(rmsnorm(ys, norm_ffn[i]), state_ffn_conv[i], *ffw)
        yp = yp + fp
        ys = ys + fs
        p_conv.append(cvp); s_conv.append(cvs)
    y_prompt = rmsnorm(yp, final_norm)
    y_sample = rmsnorm(ys, final_norm)
    return (y_prompt, y_sample,
            jnp.stack(p_ckv), jnp.stack(p_kpe), jnp.stack(p_swk), jnp.stack(p_swv),
            jnp.stack(p_dk), jnp.stack(p_dv), jnp.stack(p_conv),
            jnp.stack(s_ckv), jnp.stack(s_kpe), jnp.stack(s_swk), jnp.stack(s_swv),
            jnp.stack(s_dk), jnp.stack(s_dv), jnp.stack(s_conv))
```

```python
import functools
import math

import jax
import jax.numpy as jnp
from jax import lax
from jax.experimental import pallas as pl
from jax.experimental.pallas import tpu as pltpu

F32 = jnp.float32
BF16 = jnp.bfloat16

CHUNK = 64
ROPE_THETA = 10000.0
NORM_EPS = 1e-6
SUBLN_EPS = 1e-5
N_MIXERS = 3
MLA_HEADS = 16
MLA_NOPE = 128
MLA_ROPE = 64
MLA_V = 128
SWA_HEADS = 32
SWA_KV_HEADS = 4
SWA_HEAD_DIM = 64
WINDOW = 128
DIFF_HEADS = 8
DIFF_KV_HEADS = 4
DIFF_HEAD_DIM = 128
CONV_W = 3

LANES = 128
SUBLANES = 8
MXU_COLS = 256
MLA_QK_PAD = MXU_COLS
VMEM_LIMIT_BYTES = 56 * 1024 * 1024
LOG2E = 1.4426950408889634

ROW_TILE = 512
FFN_ROW_TILE = 1024
FFN_COL_TILE = 512
ATTN_TILE = 512
MM_TILE = 1024
SWA_Q_TILE = 2 * CHUNK


def _cparams(n_axes):
    return pltpu.CompilerParams(dimension_semantics=("arbitrary",) * n_axes,
                                vmem_limit_bytes=VMEM_LIMIT_BYTES)


def _resident(shape):
    return pl.BlockSpec(shape, lambda *_: (0,) * len(shape), pipeline_mode=pl.Buffered(1))


def _tile(n, pref):
    t = min(n, pref)
    assert n % t == 0, (n, pref)
    return t


def _rms(x, g, eps):
    ms = jnp.mean(x * x, axis=-1, keepdims=True)
    return (x * lax.rsqrt(ms + eps)) * g


def _rope_slab(x, cos, sin, head_dim):
    if head_dim == LANES:
        swapped = pltpu.roll(x, LANES // 2, axis=1)
    else:
        half = head_dim // 2
        lane = lax.broadcasted_iota(jnp.int32, x.shape, 1)
        first = (lane & (head_dim - 1)) < half
        swapped = jnp.where(first, pltpu.roll(x, LANES - half, axis=1), pltpu.roll(x, half, axis=1))
    return x * cos + swapped * sin


def _rope_tables(seq_len, offset, head_dim, rows):
    half = head_dim // 2
    pos = jnp.arange(seq_len, dtype=F32) + offset
    inv = ROPE_THETA ** (-jnp.arange(0, head_dim, 2, dtype=F32) / head_dim)
    ang = pos[:, None] * inv[None, :]
    cos, sin = jnp.cos(ang), jnp.sin(ang)
    reps = LANES // head_dim
    cos_l = jnp.tile(jnp.concatenate([cos, cos], axis=1), (1, reps))
    sin_l = jnp.tile(jnp.concatenate([-sin, sin], axis=1), (1, reps))
    n = max(rows // seq_len, 1)
    return jnp.tile(cos_l, (n, 1)), jnp.tile(sin_l, (n, 1))


def _mla_down_kernel(x_ref, g_ref, w_ref, qn_ref, kvn_ref, cos_ref, sin_ref,
                     qa_ref, ckv_ref, kpe_ref, *, q_rank, kv_rank):
    h = _rms(x_ref[...], g_ref[...], NORM_EPS).astype(BF16)
    d = jnp.dot(h, w_ref[...], preferred_element_type=F32)
    qa_ref[...] = _rms(d[:, :q_rank], qn_ref[...], NORM_EPS).astype(BF16)
    ckv_ref[...] = _rms(d[:, q_rank:q_rank + kv_rank], kvn_ref[...], NORM_EPS)
    slab = d[:, q_rank + kv_rank:q_rank + kv_rank + LANES]
    kpe_ref[...] = _rope_slab(slab, cos_ref[...], sin_ref[...], MLA_ROPE)[:, :MLA_ROPE]


def _mla_down(x, gain, w_cat, q_norm, kv_norm, cos, sin, q_rank, kv_rank):
    m, d_model = x.shape
    tm = _tile(m, ROW_TILE)
    n_tab = cos.shape[0] // tm
    wn = w_cat.shape[1]
    row = lambda i: (i, 0)
    fixed = lambda i: (0, 0)
    tab = lambda i: (i % n_tab, 0)
    return pl.pallas_call(
        functools.partial(_mla_down_kernel, q_rank=q_rank, kv_rank=kv_rank),
        grid=(m // tm,),
        in_specs=[pl.BlockSpec((tm, d_model), row), pl.BlockSpec((1, d_model), fixed),
                  _resident((d_model, wn)), pl.BlockSpec((1, q_rank), fixed),
                  pl.BlockSpec((1, kv_rank), fixed), pl.BlockSpec((tm, LANES), tab),
                  pl.BlockSpec((tm, LANES), tab)],
        out_specs=[pl.BlockSpec((tm, q_rank), row), pl.BlockSpec((tm, kv_rank), row),
                   pl.BlockSpec((tm, MLA_ROPE), row)],
        out_shape=[jax.ShapeDtypeStruct((m, q_rank), BF16), jax.ShapeDtypeStruct((m, kv_rank), F32),
                   jax.ShapeDtypeStruct((m, MLA_ROPE), F32)],
        compiler_params=_cparams(1), name="mla_down",
    )(x, gain, w_cat, q_norm, kv_norm, cos, sin)


def _mla_q_up_kernel(qa_ref, w_ref, cos_ref, sin_ref, q_ref):
    qa = qa_ref[...]
    cos, sin = cos_ref[...], sin_ref[...]
    for h in range(MLA_HEADS):
        c0 = h * MLA_QK_PAD
        d = jnp.dot(qa, w_ref[:, c0:c0 + MLA_QK_PAD], preferred_element_type=F32)
        q_ref[:, c0:c0 + LANES] = d[:, :LANES].astype(BF16)
        q_ref[:, c0 + LANES:c0 + MLA_QK_PAD] = _rope_slab(d[:, LANES:], cos, sin, MLA_ROPE).astype(BF16)


def _mla_q_up(qa, w_pad, cos, sin):
    m, q_rank = qa.shape
    tm = _tile(m, ROW_TILE)
    n_tab = cos.shape[0] // tm
    n = w_pad.shape[1]
    row = lambda i: (i, 0)
    fixed = lambda i: (0, 0)
    tab = lambda i: (i % n_tab, 0)
    return pl.pallas_call(
        _mla_q_up_kernel, grid=(m // tm,),
        in_specs=[pl.BlockSpec((tm, q_rank), row), _resident((q_rank, n)),
                  pl.BlockSpec((tm, LANES), tab), pl.BlockSpec((tm, LANES), tab)],
        out_specs=pl.BlockSpec((tm, n), row),
        out_shape=jax.ShapeDtypeStruct((m, n), BF16),
        compiler_params=_cparams(1), name="mla_q_up",
    )(qa, w_pad, cos, sin)


def _mla_expand_kernel(ckv_ref, kpe_ref, wk_ref, wv_ref, k_ref, v_ref):
    c = ckv_ref[...].astype(BF16)
    kpe = kpe_ref[...].astype(BF16)
    v_ref[...] = jnp.dot(c, wv_ref[...], preferred_element_type=F32).astype(BF16)
    kn = jnp.dot(c, wk_ref[...], preferred_element_type=F32).astype(BF16)
    zeros = jnp.zeros((kpe.shape[0], MLA_QK_PAD - MLA_NOPE - MLA_ROPE), BF16)
    for h in range(MLA_HEADS):
        c0 = h * MLA_QK_PAD
        k_ref[:, c0:c0 + MLA_NOPE] = kn[:, h * MLA_NOPE:(h + 1) * MLA_NOPE]
        k_ref[:, c0 + MLA_NOPE:c0 + MLA_NOPE + MLA_ROPE] = kpe
        k_ref[:, c0 + MLA_NOPE + MLA_ROPE:c0 + MLA_QK_PAD] = zeros


def _mla_expand(ckv, kpe, wk, wv):
    m, kv_rank = ckv.shape
    tm = _tile(m, ROW_TILE)
    row = lambda i: (i, 0)
    fixed = lambda i: (0, 0)
    nk, nv = MLA_HEADS * MLA_QK_PAD, MLA_HEADS * MLA_V
    return pl.pallas_call(
        _mla_expand_kernel, grid=(m // tm,),
        in_specs=[pl.BlockSpec((tm, kv_rank), row), pl.BlockSpec((tm, MLA_ROPE), row),
                  _resident(wk.shape), _resident(wv.shape)],
        out_specs=[pl.BlockSpec((tm, nk), row), pl.BlockSpec((tm, nv), row)],
        out_shape=[jax.ShapeDtypeStruct((m, nk), BF16), jax.ShapeDtypeStruct((m, nv), BF16)],
        compiler_params=_cparams(1), name="mla_expand",
    )(ckv, kpe, wk, wv)


def _qkv_proj_kernel(x_ref, g_ref, w_ref, cos_ref, sin_ref, q_ref, k_ref, v_ref, kb_ref, vb_ref,
                     *, qw, kw, vw, head_dim):
    h = _rms(x_ref[...], g_ref[...], NORM_EPS).astype(BF16)
    cos, sin = cos_ref[...], sin_ref[...]
    for n0 in range(0, qw + kw + vw, MXU_COLS):
        dd = jnp.dot(h, w_ref[:, n0:n0 + MXU_COLS], preferred_element_type=F32)
        for c0 in range(n0, n0 + MXU_COLS, LANES):
            d = dd[:, c0 - n0:c0 - n0 + LANES]
            if c0 < qw:
                q_ref[:, c0:c0 + LANES] = _rope_slab(d, cos, sin, head_dim).astype(BF16)
            elif c0 < qw + kw:
                r = _rope_slab(d, cos, sin, head_dim)
                k_ref[:, c0 - qw:c0 - qw + LANES] = r
                kb_ref[:, c0 - qw:c0 - qw + LANES] = r.astype(BF16)
            else:
                c1 = c0 - qw - kw
                v_ref[:, c1:c1 + LANES] = d
                vb_ref[:, c1:c1 + LANES] = d.astype(BF16)


def _qkv_proj(x, gain, w, cos, sin, qw, kw, vw, head_dim):
    m, d_model = x.shape
    tm = _tile(m, ROW_TILE)
    n_tab = cos.shape[0] // tm
    row = lambda i: (i, 0)
    fixed = lambda i: (0, 0)
    tab = lambda i: (i % n_tab, 0)
    return pl.pallas_call(
        functools.partial(_qkv_proj_kernel, qw=qw, kw=kw, vw=vw, head_dim=head_dim),
        grid=(m // tm,),
        in_specs=[pl.BlockSpec((tm, d_model), row), pl.BlockSpec((1, d_model), fixed),
                  _resident(w.shape), pl.BlockSpec((tm, LANES), tab),
                  pl.BlockSpec((tm, LANES), tab)],
        out_specs=[pl.BlockSpec((tm, qw), row), pl.BlockSpec((tm, kw), row), pl.BlockSpec((tm, vw), row),
                   pl.BlockSpec((tm, kw), row), pl.BlockSpec((tm, vw), row)],
        out_shape=[jax.ShapeDtypeStruct((m, qw), BF16), jax.ShapeDtypeStruct((m, kw), F32),
                   jax.ShapeDtypeStruct((m, vw), F32), jax.ShapeDtypeStruct((m, kw), BF16),
                   jax.ShapeDtypeStruct((m, vw), BF16)],
        compiler_params=_cparams(1), name="qkv_proj",
    )(x, gain, w, cos, sin)


def _mm_res_kernel(a_ref, w_ref, r_ref, o_ref):
    o_ref[...] = r_ref[...] + jnp.dot(a_ref[...], w_ref[...], preferred_element_type=F32)


def _mm_res(a, w, res, tm_pref=MM_TILE, tn_pref=MM_TILE):
    m, k = a.shape
    n = w.shape[1]
    tm, tn = _tile(m, tm_pref), _tile(n, tn_pref)
    return pl.pallas_call(
        _mm_res_kernel, grid=(m // tm, n // tn),
        in_specs=[pl.BlockSpec((tm, k), lambda i, j: (i, 0)), pl.BlockSpec((k, tn), lambda i, j: (0, j)),
                  pl.BlockSpec((tm, tn), lambda i, j: (i, j))],
        out_specs=pl.BlockSpec((tm, tn), lambda i, j: (i, j)),
        out_shape=jax.ShapeDtypeStruct((m, n), F32),
        compiler_params=_cparams(2), name="mm_res",
    )(a, w, res)


def _lane_bcast(x, width):
    if width == LANES:
        return x
    if width < LANES:
        return x[:, :width]
    return jnp.tile(x, (1, width // LANES))


def _diff_lambda(lam_ref, lam_init):
    lam = lam_ref[...]
    a = jnp.sum(lam[0:1] * lam[1:2], axis=-1, keepdims=True)
    b = jnp.sum(lam[2:3] * lam[3:4], axis=-1, keepdims=True)
    return jnp.exp(a) - jnp.exp(b) + lam_init


def _combine_heads(o_list, o_ref, lam_ref, subln_ref, *, n_maps, dv, lam_init):
    groups = len(o_list) // n_maps
    for g in range(groups):
        if n_maps == 1:
            o = o_list[g]
        else:
            lam = _diff_lambda(lam_ref, lam_init)
            o = o_list[2 * g] - lam * o_list[2 * g + 1]
            o = _rms(o, subln_ref[...], SUBLN_EPS) * (1.0 - lam_init)
        o_ref[:, g * dv:(g + 1) * dv] = o.astype(o_ref.dtype)


def _flash_kernel(*refs, tq, dk, dv, groups, n_maps, scale, lam_init):
    if n_maps == 2:
        lam_ref, subln_ref, q_ref, k_ref, v_ref, o_ref, m_ref, l_ref, acc_ref = refs
    else:
        q_ref, k_ref, v_ref, o_ref, m_ref, l_ref, acc_ref = refs
        lam_ref = subln_ref = None
    n_sub = groups * n_maps
    qi = pl.program_id(2)
    c = scale * LOG2E

    m_ref[...] = jnp.full(m_ref.shape, -jnp.inf, F32)
    l_ref[...] = jnp.zeros(l_ref.shape, F32)
    acc_ref[...] = jnp.zeros(acc_ref.shape, F32)

    def block(start, masked):
        k = k_ref[pl.ds(start, tq), :]
        v = v_ref[pl.ds(start, tq), :]
        for u in range(n_sub):
            mi = u % n_maps
            q = q_ref[:, u * dk:(u + 1) * dk]
            s = lax.dot_general(q, k[:, mi * dk:(mi + 1) * dk], (((1,), (1,)), ((), ())),
                                preferred_element_type=F32) * c
            if masked:
                rc = lax.broadcasted_iota(jnp.int32, s.shape, 0) // CHUNK
                kc = lax.broadcasted_iota(jnp.int32, s.shape, 1) // CHUNK
                s = jnp.where(kc <= rc, s, -jnp.inf)
            m_prev = m_ref[u]
            m_new = jnp.maximum(m_prev, jnp.max(s, axis=1, keepdims=True))
            alpha = jnp.exp2(m_prev - m_new)
            p = jnp.exp2(s - _lane_bcast(m_new, tq))
            l_ref[u] = alpha * l_ref[u] + jnp.sum(p, axis=1, keepdims=True)
            acc_ref[u] = acc_ref[u] * _lane_bcast(alpha, dv) + jnp.dot(
                p.astype(BF16), v, preferred_element_type=F32)
            m_ref[u] = m_new

    def body(j, carry):
        block(pl.multiple_of(j * tq, tq), False)
        return carry

    lax.fori_loop(0, qi, body, 0)
    block(pl.multiple_of(qi * tq, tq), True)

    outs = [acc_ref[u] / _lane_bcast(l_ref[u], dv) for u in range(n_sub)]
    _combine_heads(outs, o_ref, lam_ref, subln_ref, n_maps=n_maps, dv=dv, lam_init=lam_init)


def _flash_attention(q, k, v, *, batch, seq, n_kv_heads, dk, dv, groups, n_maps, scale,
                     lam=None, subln=None, lam_init=0.0):
    tq = _tile(seq, ATTN_TILE)
    nq = seq // tq
    n_sub = groups * n_maps
    kern = functools.partial(_flash_kernel, tq=tq, dk=dk, dv=dv, groups=groups, n_maps=n_maps,
                             scale=scale, lam_init=lam_init)
    in_specs = [pl.BlockSpec((tq, n_sub * dk), lambda b, h, i: (b * nq + i, h)),
                pl.BlockSpec((seq, n_maps * dk), lambda b, h, i: (b, h)),
                pl.BlockSpec((seq, dv), lambda b, h, i: (b, h))]
    args = [q, k, v]
    if n_maps == 2:
        in_specs = [pl.BlockSpec(lam.shape, lambda b, h, i: (0, 0)),
                    pl.BlockSpec(subln.shape, lambda b, h, i: (0, 0))] + in_specs
        args = [lam, subln] + args
    return pl.pallas_call(
        kern, grid=(batch, n_kv_heads, nq), in_specs=in_specs,
        out_specs=pl.BlockSpec((tq, groups * dv), lambda b, h, i: (b * nq + i, h)),
        out_shape=jax.ShapeDtypeStruct((batch * seq, n_kv_heads * groups * dv), BF16),
        scratch_shapes=[pltpu.VMEM((n_sub, tq, LANES), F32), pltpu.VMEM((n_sub, tq, LANES), F32),
                        pltpu.VMEM((n_sub, tq, dv), F32)],
        compiler_params=_cparams(3), name="flash_attention",
    )(*args)


def _decode_kernel(*refs, dk, dv, groups, n_maps, scale, lam_init):
    if n_maps == 2:
        lam_ref, subln_ref, q_ref, kc_ref, vc_ref, kn_ref, vn_ref, o_ref = refs
    else:
        q_ref, kc_ref, vc_ref, kn_ref, vn_ref, o_ref = refs
        lam_ref = subln_ref = None
    c = scale * LOG2E
    nt = (((1,), (1,)), ((), ()))
    vc = vc_ref[...].astype(BF16)
    vn = vn_ref[...].astype(BF16)
    outs = []
    for u in range(groups * n_maps):
        mi = u % n_maps
        q = q_ref[:, u * dk:(u + 1) * dk]
        kc = kc_ref[:, mi * dk:(mi + 1) * dk].astype(BF16)
        kn = kn_ref[:, mi * dk:(mi + 1) * dk].astype(BF16)
        s1 = lax.dot_general(q, kc, nt, preferred_element_type=F32) * c
        s2 = lax.dot_general(q, kn, nt, preferred_element_type=F32) * c
        m = jnp.maximum(jnp.max(s1, axis=1, keepdims=True), jnp.max(s2, axis=1, keepdims=True))
        p1 = jnp.exp2(s1 - m)
        p2 = jnp.exp2(s2 - m)
        l = jnp.sum(p1, axis=1, keepdims=True) + jnp.sum(p2, axis=1, keepdims=True)
        o = (jnp.dot(p1.astype(BF16), vc, preferred_element_type=F32)
             + jnp.dot(p2.astype(BF16), vn, preferred_element_type=F32))
        outs.append(o / l)
    _combine_heads(outs, o_ref, lam_ref, subln_ref, n_maps=n_maps, dv=dv, lam_init=lam_init)


def _decode_attention(q, kc, vc, kn, vn, *, batch, past, new, n_kv_heads, dk, dv, groups, n_maps, scale,
                      lam=None, subln=None, lam_init=0.0):
    n_sub = groups * n_maps
    kern = functools.partial(_decode_kernel, dk=dk, dv=dv, groups=groups, n_maps=n_maps,
                             scale=scale, lam_init=lam_init)
    bh = lambda b, h: (b, h)
    in_specs = [pl.BlockSpec((new, n_sub * dk), bh), pl.BlockSpec((past, n_maps * dk), bh),
                pl.BlockSpec((past, dv), bh), pl.BlockSpec((new, n_maps * dk), bh),
                pl.BlockSpec((new, dv), bh)]
    args = [q, kc, vc, kn, vn]
    if n_maps == 2:
        in_specs = [pl.BlockSpec(lam.shape, lambda b, h: (0, 0)),
                    pl.BlockSpec(subln.shape, lambda b, h: (0, 0))] + in_specs
        args = [lam, subln] + args
    return pl.pallas_call(
        kern, grid=(batch, n_kv_heads), in_specs=in_specs,
        out_specs=pl.BlockSpec((new, groups * dv), bh),
        out_shape=jax.ShapeDtypeStruct((batch * new, n_kv_heads * groups * dv), BF16),
        compiler_params=_cparams(2), name="decode_attention",
    )(*args)


def _swa_kernel(sinks_ref, q_ref, ka_ref, kb_ref, va_ref, vb_ref, o_ref, *, banded, blocks_per_seq):
    nq = q_ref.shape[0]
    na, nb = ka_ref.shape[0], kb_ref.shape[0]
    group = SWA_HEADS // SWA_KV_HEADS
    d = SWA_HEAD_DIM
    scale = d ** -0.5
    nt = (((1,), (1,)), ((), ()))
    if banded:
        rows = lax.broadcasted_iota(jnp.int32, (group * nq, na + nb), 0)
        cols = lax.broadcasted_iota(jnp.int32, (group * nq, na + nb), 1)
        qc = (rows % nq) // CHUNK
        kc = cols // CHUNK
        first = (pl.program_id(0) % blocks_per_seq) == 0
        lo = jnp.where(first, na // CHUNK, 0)
        valid = (kc >= qc) & (kc <= qc + na // CHUNK) & (kc >= lo)
    for h in range(SWA_KV_HEADS):
        hs = slice(h * d, (h + 1) * d)
        k = jnp.concatenate([ka_ref[:, hs], kb_ref[:, hs]], axis=0).astype(BF16)
        v = jnp.concatenate([va_ref[:, hs], vb_ref[:, hs]], axis=0).astype(BF16)
        q = jnp.concatenate([q_ref[:, (h * group + g) * d:(h * group + g + 1) * d] for g in range(group)],
                            axis=0)
        sink = jnp.concatenate([jnp.full((nq, 1), sinks_ref[h * group + g], F32) for g in range(group)],
                               axis=0)
        s = lax.dot_general(q, k, nt, preferred_element_type=F32) * scale
        if banded:
            s = jnp.where(valid, s, -jnp.inf)
        m = jnp.maximum(jnp.max(s, axis=1, keepdims=True), sink)
        p = jnp.exp(s - m)
        l = jnp.sum(p, axis=1, keepdims=True) + jnp.exp(sink - m)
        o = jnp.dot(p.astype(BF16), v, preferred_element_type=F32) / l
        for g in range(0, group, 2):
            pair = jnp.concatenate([o[g * nq:(g + 1) * nq], o[(g + 1) * nq:(g + 2) * nq]], axis=1)
            c0 = (h * group + g) * d
            o_ref[:, c0:c0 + 2 * d] = pair.astype(o_ref.dtype)


def _swa_attention(q, ka, kb, va, vb, sinks, *, nq, na, nb, banded, blocks_per_seq):
    m, qw = q.shape
    kvw = ka.shape[1]
    n_steps = m // nq
    cur = lambda t: (t, 0)
    a_map = (lambda t: (jnp.maximum(t - 1, 0), 0)) if banded else cur
    return pl.pallas_call(
        functools.partial(_swa_kernel, banded=banded, blocks_per_seq=blocks_per_seq),
        grid=(n_steps,),
        in_specs=[pl.BlockSpec(memory_space=pltpu.SMEM), pl.BlockSpec((nq, qw), cur),
                  pl.BlockSpec((na, kvw), a_map), pl.BlockSpec((nb, kvw), cur),
                  pl.BlockSpec((na, kvw), a_map), pl.BlockSpec((nb, kvw), cur)],
        out_specs=pl.BlockSpec((nq, qw), cur),
        out_shape=jax.ShapeDtypeStruct((m, qw), BF16),
        compiler_params=_cparams(1), name="swa_attention",
    )(sinks, q, ka, kb, va, vb)


def _ffn_up_kernel(x_ref, g_ref, wg_ref, wu_ref, cw_ref, cb_ref, prev_ref, act_ref, tail_ref,
                   h_ref, gbuf_ref, carry_ref, *, rows, n_slab, tiles_per_seq):
    i, j = pl.program_id(0), pl.program_id(1)
    halo = SUBLANES

    @pl.when(j == 0)
    def _():
        h_ref[...] = _rms(x_ref[...], g_ref[...], NORM_EPS).astype(BF16)

    h = h_ref[...]
    gate = jnp.dot(h, wg_ref[...], preferred_element_type=F32)
    up = jnp.dot(h, wu_ref[...], preferred_element_type=F32)
    w0, w1, w2 = cw_ref[0:1, :], cw_ref[1:2, :], cw_ref[2:3, :]
    bias = cb_ref[...]
    for s in range(n_slab):
        gs = gate[s * rows:(s + 1) * rows]
        if n_slab == 1:
            is_start = (i % tiles_per_seq) == 0

            @pl.when(is_start)
            def _():
                gbuf_ref[0:halo, :] = prev_ref[0]

            @pl.when(jnp.logical_not(is_start))
            def _():
                gbuf_ref[0:halo, :] = carry_ref[j]
        else:
            gbuf_ref[0:halo, :] = prev_ref[s]
        gbuf_ref[halo:halo + rows, :] = gs
        conv = bias + gbuf_ref[halo - 2:halo - 2 + rows, :] * w0
        conv = conv + gbuf_ref[halo - 1:halo - 1 + rows, :] * w1
        conv = conv + gs * w2
        act = (conv * jax.nn.sigmoid(conv)) * up[s * rows:(s + 1) * rows]
        act_ref[s * rows:(s + 1) * rows, :] = act.astype(BF16)
        tail_ref[s] = gs[rows - halo:rows]
    if n_slab == 1:
        carry_ref[j] = gate[rows - halo:rows]


def _ffn_up(x, gain, wg, wu, conv_w, conv_b, prev, *, seq):
    m, d_model = x.shape
    d_ff = wg.shape[1]
    tm = _tile(m, FFN_ROW_TILE)
    tf = _tile(d_ff, FFN_COL_TILE)
    if tm >= seq:
        rows, n_slab, tiles_per_seq = seq, tm // seq, 1
        prev_spec = pl.BlockSpec((n_slab, SUBLANES, tf), lambda i, j: (i, 0, j))
    else:
        rows, n_slab, tiles_per_seq = tm, 1, seq // tm
        prev_spec = pl.BlockSpec((1, SUBLANES, tf), lambda i, j: (i // tiles_per_seq, 0, j))
    nj = d_ff // tf
    return pl.pallas_call(
        functools.partial(_ffn_up_kernel, rows=rows, n_slab=n_slab, tiles_per_seq=tiles_per_seq),
        grid=(m // tm, nj),
        in_specs=[pl.BlockSpec((tm, d_model), lambda i, j: (i, 0)),
                  pl.BlockSpec((1, d_model), lambda i, j: (0, 0)),
                  pl.BlockSpec((d_model, tf), lambda i, j: (0, j)),
                  pl.BlockSpec((d_model, tf), lambda i, j: (0, j)),
                  pl.BlockSpec((CONV_W, tf), lambda i, j: (0, j)),
                  pl.BlockSpec((1, tf), lambda i, j: (0, j)),
                  prev_spec],
        out_specs=[pl.BlockSpec((tm, tf), lambda i, j: (i, j)),
                   pl.BlockSpec((n_slab, SUBLANES, tf), lambda i, j: (i, 0, j))],
        out_shape=[jax.ShapeDtypeStruct((m, d_ff), BF16),
                   jax.ShapeDtypeStruct((m // rows, SUBLANES, d_ff), F32)],
        scratch_shapes=[pltpu.VMEM((tm, d_model), BF16), pltpu.VMEM((rows + SUBLANES, tf), F32),
                        pltpu.VMEM((nj, SUBLANES, tf), F32)],
        compiler_params=_cparams(2), name="ffn_up",
    )(x, gain, wg, wu, conv_w, conv_b, prev)


def _final_norm_kernel(x_ref, g_ref, o_ref):
    o_ref[...] = _rms(x_ref[...], g_ref[...], NORM_EPS)


def _final_norm(x, gain):
    m, d = x.shape
    tm = _tile(m, ROW_TILE)
    return pl.pallas_call(
        _final_norm_kernel, grid=(m // tm,),
        in_specs=[pl.BlockSpec((tm, d), lambda i: (i, 0)), pl.BlockSpec((1, d), lambda i: (0, 0))],
        out_specs=pl.BlockSpec((tm, d), lambda i: (i, 0)),
        out_shape=jax.ShapeDtypeStruct((m, d), F32),
        compiler_params=_cparams(1), name="final_norm",
    )(x, gain)


def _conv_ffn(y, seq, prev_state, gain, wg, wu, conv_w, conv_b, wd):
    n_seq = y.shape[0] // seq
    d_ff = wg.shape[1]
    prev = jnp.concatenate([jnp.zeros((n_seq, SUBLANES - (CONV_W - 1), d_ff), F32), prev_state], axis=1)
    act, tails = _ffn_up(y, gain, wg, wu, conv_w, conv_b, prev, seq=seq)
    y = _mm_res(act, wd, y, tm_pref=MM_TILE // 2)
    state = tails.reshape(n_seq, -1, SUBLANES, d_ff)[:, -1, SUBLANES - (CONV_W - 1):, :]
    return y, state


def kernel(x_prompt, x_sample, cache_mla_ckv, cache_mla_kpe, cache_swa_k, cache_swa_v, cache_diff_k,
           cache_diff_v, state_ffn_conv, norm_mix, norm_ffn, final_norm, mla_w_dq, mla_q_norm, mla_w_uq,
           mla_w_dkv, mla_kv_norm, mla_w_ukv, mla_w_o, swa_w_qkv, swa_sinks, swa_w_o, diff_w_qkv,
           diff_lambda_q1, diff_lambda_k1, diff_lambda_q2, diff_lambda_k2, diff_subln, diff_w_o,
           ffn_w_gate, ffn_w_up, ffn_conv_w, ffn_conv_b, ffn_w_down):
    bp, sp, d_model = x_prompt.shape
    bs, ss, _ = x_sample.shape
    depth = norm_mix.shape[0]
    past = cache_mla_ckv.shape[2]
    d_ff = ffn_w_gate.shape[2]
    q_rank = mla_w_dq.shape[2]
    kv_rank = mla_w_ukv.shape[1]

    yp = x_prompt.reshape(bp * sp, d_model)
    ys = x_sample.reshape(bs * ss, d_model)
    row_p = _tile(bp * sp, ROW_TILE)
    row_s = _tile(bs * ss, ROW_TILE)

    def tables(head_dim):
        return (_rope_tables(sp, 0, head_dim, row_p), _rope_tables(ss, past, head_dim, row_s))

    tab64, tab128 = tables(64), tables(128)
    outs_p = {k: [] for k in ("ckv", "kpe", "swk", "swv", "dk", "dv", "conv")}
    outs_s = {k: [] for k in ("ckv", "kpe", "swk", "swv", "dk", "dv", "conv")}

    for i in range(depth):
        kind, j = i % N_MIXERS, i // N_MIXERS
        gain = norm_mix[i][None, :]
        if kind == 0:
            scale = (MLA_NOPE + MLA_ROPE) ** -0.5
            pad = (-(q_rank + kv_rank + MLA_ROPE)) % LANES
            w_cat = jnp.concatenate([mla_w_dq[j], mla_w_dkv[j], jnp.zeros((d_model, pad), F32)],
                                    axis=1).astype(BF16)
            w_uq = mla_w_uq[j].reshape(q_rank, MLA_HEADS, MLA_NOPE + MLA_ROPE)
            w_uq = jnp.pad(w_uq, ((0, 0), (0, 0), (0, MLA_QK_PAD - MLA_NOPE - MLA_ROPE)))
            w_uq = w_uq.reshape(q_rank, MLA_HEADS * MLA_QK_PAD).astype(BF16)
            w_ukv = mla_w_ukv[j].reshape(kv_rank, MLA_HEADS, MLA_NOPE + MLA_V)
            wk = w_ukv[:, :, :MLA_NOPE].reshape(kv_rank, MLA_HEADS * MLA_NOPE).astype(BF16)
            wv = w_ukv[:, :, MLA_NOPE:].reshape(kv_rank, MLA_HEADS * MLA_V).astype(BF16)
            w_o = mla_w_o[j].astype(BF16)
            qn, kvn = mla_q_norm[j][None, :], mla_kv_norm[j][None, :]

            (cos_p, sin_p), (cos_s, sin_s) = tab64
            qa, ckv_p, kpe_p = _mla_down(yp, gain, w_cat, qn, kvn, cos_p, sin_p, q_rank, kv_rank)
            q = _mla_q_up(qa, w_uq, cos_p, sin_p)
            k, v = _mla_expand(ckv_p, kpe_p, wk, wv)
            o = _flash_attention(q, k, v, batch=bp, seq=sp, n_kv_heads=MLA_HEADS, dk=MLA_QK_PAD,
                                 dv=MLA_V, groups=1, n_maps=1, scale=scale)
            yp = _mm_res(o, w_o, yp)

            qa, ckv_s, kpe_s = _mla_down(ys, gain, w_cat, qn, kvn, cos_s, sin_s, q_rank, kv_rank)
            q = _mla_q_up(qa, w_uq, cos_s, sin_s)
            kn, vn = _mla_expand(ckv_s, kpe_s, wk, wv)
            kc, vc = _mla_expand(cache_mla_ckv[j].reshape(bs * past, kv_rank),
                                 cache_mla_kpe[j].reshape(bs * past, MLA_ROPE), wk, wv)
            o = _decode_attention(q, kc, vc, kn, vn, batch=bs, past=past, new=ss, n_kv_heads=MLA_HEADS,
                                  dk=MLA_QK_PAD, dv=MLA_V, groups=1, n_maps=1, scale=scale)
            ys = _mm_res(o, w_o, ys)
            outs_p["ckv"].append(ckv_p.reshape(bp, sp, kv_rank))
            outs_p["kpe"].append(kpe_p.reshape(bp, sp, MLA_ROPE))
            outs_s["ckv"].append(ckv_s.reshape(bs, ss, kv_rank))
            outs_s["kpe"].append(kpe_s.reshape(bs, ss, MLA_ROPE))
        elif kind == 1:
            qw, kw = SWA_HEADS * SWA_HEAD_DIM, SWA_KV_HEADS * SWA_HEAD_DIM
            w_qkv = swa_w_qkv[j].astype(BF16)
            w_o = swa_w_o[j].astype(BF16)
            sinks = swa_sinks[j]
            (cos_p, sin_p), (cos_s, sin_s) = tab64
            q, k, v, _, _ = _qkv_proj(yp, gain, w_qkv, cos_p, sin_p, qw, kw, kw, SWA_HEAD_DIM)
            o = _swa_attention(q, k, k, v, v, sinks, nq=SWA_Q_TILE, na=WINDOW, nb=SWA_Q_TILE, banded=True,
                               blocks_per_seq=sp // SWA_Q_TILE)
            yp = _mm_res(o, w_o, yp)
            k3 = k.reshape(bp, sp, SWA_KV_HEADS, SWA_HEAD_DIM)
            v3 = v.reshape(bp, sp, SWA_KV_HEADS, SWA_HEAD_DIM)
            outs_p["swk"].append(k3[:, sp - WINDOW:])
            outs_p["swv"].append(v3[:, sp - WINDOW:])

            q, k, v, _, _ = _qkv_proj(ys, gain, w_qkv, cos_s, sin_s, qw, kw, kw, SWA_HEAD_DIM)
            kc = cache_swa_k[j].reshape(bs * WINDOW, kw)
            vc = cache_swa_v[j].reshape(bs * WINDOW, kw)
            o = _swa_attention(q, kc, k, vc, v, sinks, nq=ss, na=WINDOW, nb=ss, banded=False,
                               blocks_per_seq=1)
            ys = _mm_res(o, w_o, ys)
            k_all = jnp.concatenate([cache_swa_k[j], k.reshape(bs, ss, SWA_KV_HEADS, SWA_HEAD_DIM)], axis=1)
            v_all = jnp.concatenate([cache_swa_v[j], v.reshape(bs, ss, SWA_KV_HEADS, SWA_HEAD_DIM)], axis=1)
            outs_s["swk"].append(k_all[:, ss:])
            outs_s["swv"].append(v_all[:, ss:])
        else:
            lam_init = 0.8 - 0.6 * math.exp(-0.3 * i)
            scale = DIFF_HEAD_DIM ** -0.5
            groups = DIFF_HEADS // DIFF_KV_HEADS
            qw = DIFF_HEADS * 2 * DIFF_HEAD_DIM
            kw = DIFF_KV_HEADS * 2 * DIFF_HEAD_DIM
            w_qkv = diff_w_qkv[j].astype(BF16)
            w_o = diff_w_o[j].astype(BF16)
            lam = jnp.stack([diff_lambda_q1[j], diff_lambda_k1[j], diff_lambda_q2[j], diff_lambda_k2[j]])
            subln = diff_subln[j][None, :]
            common = dict(n_kv_heads=DIFF_KV_HEADS, dk=DIFF_HEAD_DIM, dv=2 * DIFF_HEAD_DIM, groups=groups,
                          n_maps=2, scale=scale, lam=lam, subln=subln, lam_init=lam_init)
            (cos_p, sin_p), (cos_s, sin_s) = tab128
            q, k, v, kb, vb = _qkv_proj(yp, gain, w_qkv, cos_p, sin_p, qw, kw, kw, DIFF_HEAD_DIM)
            o = _flash_attention(q, kb, vb, batch=bp, seq=sp, **common)
            yp = _mm_res(o, w_o, yp)
            outs_p["dk"].append(k.reshape(bp, sp, DIFF_KV_HEADS, 2, DIFF_HEAD_DIM))
            outs_p["dv"].append(v.reshape(bp, sp, DIFF_KV_HEADS, 2 * DIFF_HEAD_DIM))

            q, k, v, _, _ = _qkv_proj(ys, gain, w_qkv, cos_s, sin_s, qw, kw, kw, DIFF_HEAD_DIM)
            o = _decode_attention(q, cache_diff_k[j].reshape(bs * past, kw),
                                  cache_diff_v[j].reshape(bs * past, kw), k, v,
                                  batch=bs, past=past, new=ss, **common)
            ys = _mm_res(o, w_o, ys)
            outs_s["dk"].append(k.reshape(bs, ss, DIFF_KV_HEADS, 2, DIFF_HEAD_DIM))
            outs_s["dv"].append(v.reshape(bs, ss, DIFF_KV_HEADS, 2 * DIFF_HEAD_DIM))

        ffn_gain = norm_ffn[i][None, :]
        wg, wu, wd = ffn_w_gate[i].astype(BF16), ffn_w_up[i].astype(BF16), ffn_w_down[i].astype(BF16)
        cb = ffn_conv_b[i][None, :]
        yp, conv_p = _conv_ffn(yp, sp, jnp.zeros((bp, CONV_W - 1, d_ff), F32), ffn_gain, wg, wu,
                               ffn_conv_w[i], cb, wd)
        ys, conv_s = _conv_ffn(ys, ss, state_ffn_conv[i], ffn_gain, wg, wu, ffn_conv_w[i], cb, wd)
        outs_p["conv"].append(conv_p)
        outs_s["conv"].append(conv_s)

    fg = final_norm[None, :]
    y_prompt = _final_norm(yp, fg).reshape(bp, sp, d_model)
    y_sample = _final_norm(ys, fg).reshape(bs, ss, d_model)
    order = ("ckv", "kpe", "swk", "swv", "dk", "dv", "conv")
    return (y_prompt, y_sample) + tuple(jnp.stack(outs_p[k]) for k in order) + tuple(
        jnp.stack(outs_s[k]) for k in order)
```

```python
import functools
import math

import jax
import jax.numpy as jnp
from jax import lax
from jax.experimental import pallas as pl
from jax.experimental.pallas import tpu as pltpu

F32 = jnp.float32
BF16 = jnp.bfloat16

CHUNK = 64
ROPE_THETA = 10000.0
NORM_EPS = 1e-6
SUBLN_EPS = 1e-5
N_MIXERS = 3
MLA_HEADS = 16
MLA_NOPE = 128
MLA_ROPE = 64
MLA_V = 128
SWA_HEADS = 32
SWA_KV_HEADS = 4
SWA_HEAD_DIM = 64
WINDOW = 128
DIFF_HEADS = 8
DIFF_KV_HEADS = 4
DIFF_HEAD_DIM = 128
CONV_W = 3

LANES = 128
SUBLANES = 8
MXU_COLS = 256
MLA_QK_PAD = MXU_COLS
VMEM_LIMIT_BYTES = 56 * 1024 * 1024
LOG2E = 1.4426950408889634

ROW_TILE = 512
FFN_ROW_TILE = 1024
FFN_COL_TILE = 512
ATTN_Q_TILE = 2048
ATTN_KV_TILE = 512
ATTN_CHAIN = 256
ATTN_LOOKAHEAD = 3
ONES_ROWS = 16
MM_TILE = 1024
SWA_Q_TILE = 2 * CHUNK


def _cparams(n_axes):
    return pltpu.CompilerParams(dimension_semantics=("arbitrary",) * n_axes,
                                vmem_limit_bytes=VMEM_LIMIT_BYTES)


def _resident(shape):
    return pl.BlockSpec(shape, lambda *_: (0,) * len(shape), pipeline_mode=pl.Buffered(1))


def _tile(n, pref):
    t = min(n, pref)
    assert n % t == 0, (n, pref)
    return t


def _rms(x, g, eps):
    ms = jnp.mean(x * x, axis=-1, keepdims=True)
    return (x * lax.rsqrt(ms + eps)) * g


def _rope_slab(x, cos, sin, head_dim):
    if head_dim == LANES:
        swapped = pltpu.roll(x, LANES // 2, axis=1)
    else:
        half = head_dim // 2
        lane = lax.broadcasted_iota(jnp.int32, x.shape, 1)
        first = (lane & (head_dim - 1)) < half
        swapped = jnp.where(first, pltpu.roll(x, LANES - half, axis=1), pltpu.roll(x, half, axis=1))
    return x * cos + swapped * sin


def _rope_tables(seq_len, offset, head_dim, rows):
    half = head_dim // 2
    pos = jnp.arange(seq_len, dtype=F32) + offset
    inv = ROPE_THETA ** (-jnp.arange(0, head_dim, 2, dtype=F32) / head_dim)
    ang = pos[:, None] * inv[None, :]
    cos, sin = jnp.cos(ang), jnp.sin(ang)
    reps = LANES // head_dim
    cos_l = jnp.tile(jnp.concatenate([cos, cos], axis=1), (1, reps))
    sin_l = jnp.tile(jnp.concatenate([-sin, sin], axis=1), (1, reps))
    n = max(rows // seq_len, 1)
    return jnp.tile(cos_l, (n, 1)), jnp.tile(sin_l, (n, 1))


def _mla_down_kernel(x_ref, g_ref, w_ref, qn_ref, kvn_ref, cos_ref, sin_ref,
                     qa_ref, ckv_ref, kpe_ref, *, q_rank, kv_rank):
    h = _rms(x_ref[...], g_ref[...], NORM_EPS).astype(BF16)
    d = jnp.dot(h, w_ref[...], preferred_element_type=F32)
    qa_ref[...] = _rms(d[:, :q_rank], qn_ref[...], NORM_EPS).astype(BF16)
    ckv_ref[...] = _rms(d[:, q_rank:q_rank + kv_rank], kvn_ref[...], NORM_EPS)
    slab = d[:, q_rank + kv_rank:q_rank + kv_rank + LANES]
    kpe_ref[...] = _rope_slab(slab, cos_ref[...], sin_ref[...], MLA_ROPE)[:, :MLA_ROPE]


def _mla_down(x, gain, w_cat, q_norm, kv_norm, cos, sin, q_rank, kv_rank):
    m, d_model = x.shape
    tm = _tile(m, ROW_TILE)
    n_tab = cos.shape[0] // tm
    wn = w_cat.shape[1]
    row = lambda i: (i, 0)
    fixed = lambda i: (0, 0)
    tab = lambda i: (i % n_tab, 0)
    return pl.pallas_call(
        functools.partial(_mla_down_kernel, q_rank=q_rank, kv_rank=kv_rank),
        grid=(m // tm,),
        in_specs=[pl.BlockSpec((tm, d_model), row), pl.BlockSpec((1, d_model), fixed),
                  _resident((d_model, wn)), pl.BlockSpec((1, q_rank), fixed),
                  pl.BlockSpec((1, kv_rank), fixed), pl.BlockSpec((tm, LANES), tab),
                  pl.BlockSpec((tm, LANES), tab)],
        out_specs=[pl.BlockSpec((tm, q_rank), row), pl.BlockSpec((tm, kv_rank), row),
                   pl.BlockSpec((tm, MLA_ROPE), row)],
        out_shape=[jax.ShapeDtypeStruct((m, q_rank), BF16), jax.ShapeDtypeStruct((m, kv_rank), F32),
                   jax.ShapeDtypeStruct((m, MLA_ROPE), F32)],
        compiler_params=_cparams(1), name="mla_down",
    )(x, gain, w_cat, q_norm, kv_norm, cos, sin)


def _mla_q_up_kernel(qa_ref, w_ref, cos_ref, sin_ref, q_ref, *, q_scale):
    qa = qa_ref[...]
    cos, sin = cos_ref[...], sin_ref[...]
    for h in range(MLA_HEADS):
        c0 = h * MLA_QK_PAD
        d = jnp.dot(qa, w_ref[:, c0:c0 + MLA_QK_PAD], preferred_element_type=F32)
        q_ref[:, c0:c0 + LANES] = (d[:, :LANES] * q_scale).astype(BF16)
        q_ref[:, c0 + LANES:c0 + MLA_QK_PAD] = (_rope_slab(d[:, LANES:], cos, sin, MLA_ROPE) * q_scale).astype(BF16)


def _mla_q_up(qa, w_pad, cos, sin, q_scale):
    m, q_rank = qa.shape
    tm = _tile(m, ROW_TILE)
    n_tab = cos.shape[0] // tm
    n = w_pad.shape[1]
    row = lambda i: (i, 0)
    fixed = lambda i: (0, 0)
    tab = lambda i: (i % n_tab, 0)
    return pl.pallas_call(
        functools.partial(_mla_q_up_kernel, q_scale=q_scale), grid=(m // tm,),
        in_specs=[pl.BlockSpec((tm, q_rank), row), _resident((q_rank, n)),
                  pl.BlockSpec((tm, LANES), tab), pl.BlockSpec((tm, LANES), tab)],
        out_specs=pl.BlockSpec((tm, n), row),
        out_shape=jax.ShapeDtypeStruct((m, n), BF16),
        compiler_params=_cparams(1), name="mla_q_up",
    )(qa, w_pad, cos, sin)


def _mla_expand_kernel(ckv_ref, kpe_ref, wk_ref, wv_ref, k_ref, v_ref, *, v_transposed):
    c = ckv_ref[...].astype(BF16)
    kpe = kpe_ref[...].astype(BF16)
    if v_transposed:
        vt = lax.dot_general(wv_ref[...], c, (((1,), (1,)), ((), ())), preferred_element_type=F32)
        dva = MLA_V + ONES_ROWS
        for h in range(MLA_HEADS):
            v_ref[0, h * dva:h * dva + MLA_V, :] = vt[h * MLA_V:(h + 1) * MLA_V].astype(BF16)
            v_ref[0, h * dva + MLA_V:(h + 1) * dva, :] = jnp.ones((ONES_ROWS, vt.shape[1]), BF16)
    else:
        v_ref[...] = jnp.dot(c, wv_ref[...], preferred_element_type=F32).astype(BF16)
    kn = jnp.dot(c, wk_ref[...], preferred_element_type=F32).astype(BF16)
    zeros = jnp.zeros((kpe.shape[0], MLA_QK_PAD - MLA_NOPE - MLA_ROPE), BF16)
    for h in range(MLA_HEADS):
        c0 = h * MLA_QK_PAD
        k_ref[:, c0:c0 + MLA_NOPE] = kn[:, h * MLA_NOPE:(h + 1) * MLA_NOPE]
        k_ref[:, c0 + MLA_NOPE:c0 + MLA_NOPE + MLA_ROPE] = kpe
        k_ref[:, c0 + MLA_NOPE + MLA_ROPE:c0 + MLA_QK_PAD] = zeros


def _mla_expand(ckv, kpe, wk, wv, *, v_transposed):
    m, kv_rank = ckv.shape
    tm = _tile(m, ATTN_KV_TILE if v_transposed else ROW_TILE)
    row = lambda i: (i, 0)
    nk, nv = MLA_HEADS * MLA_QK_PAD, MLA_HEADS * MLA_V
    if v_transposed:
        nva = MLA_HEADS * (MLA_V + ONES_ROWS)
        v_spec = pl.BlockSpec((1, nva, tm), lambda i: (i, 0, 0))
        v_shape = jax.ShapeDtypeStruct((m // tm, nva, tm), BF16)
    else:
        v_spec = pl.BlockSpec((tm, nv), row)
        v_shape = jax.ShapeDtypeStruct((m, nv), BF16)
    return pl.pallas_call(
        functools.partial(_mla_expand_kernel, v_transposed=v_transposed), grid=(m // tm,),
        in_specs=[pl.BlockSpec((tm, kv_rank), row), pl.BlockSpec((tm, MLA_ROPE), row),
                  _resident(wk.shape), _resident(wv.shape)],
        out_specs=[pl.BlockSpec((tm, nk), row), v_spec],
        out_shape=[jax.ShapeDtypeStruct((m, nk), BF16), v_shape],
        compiler_params=_cparams(1), name="mla_expand",
    )(ckv, kpe, wk, wv)


def _qkv_proj_kernel(x_ref, g_ref, w_ref, cos_ref, sin_ref, q_ref, k_ref, v_ref, kb_ref, vb_ref,
                     *, qw, kw, vw, head_dim, q_scale, v_head_dim):
    h = _rms(x_ref[...], g_ref[...], NORM_EPS).astype(BF16)
    cos, sin = cos_ref[...], sin_ref[...]
    for n0 in range(0, qw + kw + vw, MXU_COLS):
        dd = jnp.dot(h, w_ref[:, n0:n0 + MXU_COLS], preferred_element_type=F32)
        for c0 in range(n0, n0 + MXU_COLS, LANES):
            d = dd[:, c0 - n0:c0 - n0 + LANES]
            if c0 < qw:
                q_ref[:, c0:c0 + LANES] = (_rope_slab(d, cos, sin, head_dim) * q_scale).astype(BF16)
            elif c0 < qw + kw:
                r = _rope_slab(d, cos, sin, head_dim)
                k_ref[:, c0 - qw:c0 - qw + LANES] = r
                kb_ref[:, c0 - qw:c0 - qw + LANES] = r.astype(BF16)
            else:
                c1 = c0 - qw - kw
                v_ref[:, c1:c1 + LANES] = d
                if v_head_dim:
                    r0 = (c1 // v_head_dim) * (v_head_dim + ONES_ROWS) + c1 % v_head_dim
                    vb_ref[0, r0:r0 + LANES, :] = d.T.astype(BF16)
                    if (c1 + LANES) % v_head_dim == 0:
                        vb_ref[0, r0 + LANES:r0 + LANES + ONES_ROWS, :] = jnp.ones((ONES_ROWS, d.shape[0]), BF16)
                else:
                    vb_ref[:, c1:c1 + LANES] = d.astype(BF16)


def _qkv_proj(x, gain, w, cos, sin, qw, kw, vw, head_dim, *, q_scale=1.0, v_head_dim=None):
    m, d_model = x.shape
    tm = _tile(m, ATTN_KV_TILE if v_head_dim else ROW_TILE)
    n_tab = cos.shape[0] // tm
    row = lambda i: (i, 0)
    fixed = lambda i: (0, 0)
    tab = lambda i: (i % n_tab, 0)
    if v_head_dim:
        vwa = vw // v_head_dim * (v_head_dim + ONES_ROWS)
        vb_spec = pl.BlockSpec((1, vwa, tm), lambda i: (i, 0, 0))
        vb_shape = jax.ShapeDtypeStruct((m // tm, vwa, tm), BF16)
    else:
        vb_spec = pl.BlockSpec((tm, vw), row)
        vb_shape = jax.ShapeDtypeStruct((m, vw), BF16)
    return pl.pallas_call(
        functools.partial(_qkv_proj_kernel, qw=qw, kw=kw, vw=vw, head_dim=head_dim, q_scale=q_scale,
                          v_head_dim=v_head_dim),
        grid=(m // tm,),
        in_specs=[pl.BlockSpec((tm, d_model), row), pl.BlockSpec((1, d_model), fixed),
                  _resident(w.shape), pl.BlockSpec((tm, LANES), tab),
                  pl.BlockSpec((tm, LANES), tab)],
        out_specs=[pl.BlockSpec((tm, qw), row), pl.BlockSpec((tm, kw), row), pl.BlockSpec((tm, vw), row),
                   pl.BlockSpec((tm, kw), row), vb_spec],
        out_shape=[jax.ShapeDtypeStruct((m, qw), BF16), jax.ShapeDtypeStruct((m, kw), F32),
                   jax.ShapeDtypeStruct((m, vw), F32), jax.ShapeDtypeStruct((m, kw), BF16), vb_shape],
        compiler_params=_cparams(1), name="qkv_proj",
    )(x, gain, w, cos, sin)


def _mm_res_kernel(a_ref, w_ref, r_ref, o_ref):
    o_ref[...] = r_ref[...] + jnp.dot(a_ref[...], w_ref[...], preferred_element_type=F32)


def _mm_res(a, w, res, tm_pref=MM_TILE, tn_pref=MM_TILE):
    m, k = a.shape
    n = w.shape[1]
    tm, tn = _tile(m, tm_pref), _tile(n, tn_pref)
    return pl.pallas_call(
        _mm_res_kernel, grid=(m // tm, n // tn),
        in_specs=[pl.BlockSpec((tm, k), lambda i, j: (i, 0)), pl.BlockSpec((k, tn), lambda i, j: (0, j)),
                  pl.BlockSpec((tm, tn), lambda i, j: (i, j))],
        out_specs=pl.BlockSpec((tm, tn), lambda i, j: (i, j)),
        out_shape=jax.ShapeDtypeStruct((m, n), F32),
        compiler_params=_cparams(2), name="mm_res",
    )(a, w, res)


def _diff_lambda(lam_ref, lam_init):
    lam = lam_ref[...]
    a = jnp.sum(lam[0:1] * lam[1:2], axis=-1, keepdims=True)
    b = jnp.sum(lam[2:3] * lam[3:4], axis=-1, keepdims=True)
    return jnp.exp(a) - jnp.exp(b) + lam_init


def _combine_heads(o_list, o_ref, rows, lam_ref, subln_ref, *, n_maps, dv, lam_init):
    groups = len(o_list) // n_maps
    for g in range(groups):
        if n_maps == 1:
            o = o_list[g]
        else:
            lam = _diff_lambda(lam_ref, lam_init)
            o = o_list[2 * g] - lam * o_list[2 * g + 1]
            o = _rms(o, subln_ref[...], SUBLN_EPS) * (1.0 - lam_init)
        o_ref[rows, g * dv:(g + 1) * dv] = o.astype(o_ref.dtype)


def _flash_kernel(*refs, tq, tk, cw, dk, dv, groups, n_maps, lam_init):
    if n_maps == 2:
        lam_ref, subln_ref, q_ref, k_ref, vt_ref, o_ref, m_ref, acc_ref = refs
    else:
        q_ref, k_ref, vt_ref, o_ref, m_ref, acc_ref = refs
        lam_ref = subln_ref = None
    n_sub = groups * n_maps
    n_chain = tq // cw
    kv_per_q = tq // tk
    qi = pl.program_id(2)
    nt = (((1,), (1,)), ((), ()))

    m_ref[...] = jnp.full(m_ref.shape, -jnp.inf, F32)
    acc_ref[...] = jnp.zeros(acc_ref.shape, F32)

    def scores(item, k_blks):
        u, n, k_lo, bi = item
        mi = u % n_maps
        q_n = q_ref[n * cw:(n + 1) * cw, u * dk:(u + 1) * dk]
        st = lax.dot_general(k_blks[bi][:, mi * dk:(mi + 1) * dk], q_n, nt, preferred_element_type=F32)
        if k_lo is not None:
            kc = (lax.broadcasted_iota(jnp.int32, st.shape, 0) + k_lo) // CHUNK
            qc = (lax.broadcasted_iota(jnp.int32, st.shape, 1) + n * cw) // CHUNK
            st = jnp.where(kc <= qc, st, -jnp.inf)
        return st

    def update(item, st, vt_blks):
        u, n, _, bi = item
        idx = u * n_chain + n
        m_prev = m_ref[idx]
        m_new = jnp.maximum(m_prev, jnp.max(st, axis=0, keepdims=True))
        alpha = jnp.exp2(m_prev - m_new)
        pt = jnp.exp2(st - m_new).astype(BF16)
        acc_ref[idx] = acc_ref[idx] * alpha + jnp.dot(vt_blks[bi], pt, preferred_element_type=F32)
        m_ref[idx] = m_new

    def run_chains(items, first_block):
        blocks = sorted({it[3] for it in items})
        k_blks = {bi: k_ref[pl.ds(pl.multiple_of((first_block + bi) * tk, tk), tk), :] for bi in blocks}
        vt_blks = {bi: vt_ref[first_block + bi] for bi in blocks}
        pending = [scores(it, k_blks) for it in items[:ATTN_LOOKAHEAD]]
        for t, item in enumerate(items):
            if t + ATTN_LOOKAHEAD < len(items):
                pending.append(scores(items[t + ATTN_LOOKAHEAD], k_blks))
            update(item, pending.pop(0), vt_blks)

    def body(j, carry):
        items = [(u, n, None, bi) for bi in range(kv_per_q) for n in range(n_chain) for u in range(n_sub)]
        run_chains(items, j * kv_per_q)
        return carry

    lax.fori_loop(0, qi, body, 0)

    items = [(u, n, bi * tk if (bi + 1) * tk > n * cw else None, bi)
             for bi in range(kv_per_q) for n in range(n_chain) if bi * tk < (n + 1) * cw
             for u in range(n_sub)]
    run_chains(items, qi * kv_per_q)

    for n in range(n_chain):
        outs = []
        for u in range(n_sub):
            a = acc_ref[u * n_chain + n]
            outs.append((a[:dv] / a[dv:dv + 1]).T)
        _combine_heads(outs, o_ref, slice(n * cw, (n + 1) * cw), lam_ref, subln_ref,
                       n_maps=n_maps, dv=dv, lam_init=lam_init)


def _flash_attention(q, k, vt, *, batch, seq, n_kv_heads, dk, dv, groups, n_maps,
                     lam=None, subln=None, lam_init=0.0):
    tk = vt.shape[2]
    dva = dv + ONES_ROWS
    n_sub = groups * n_maps
    tq = _tile(seq, max(ATTN_Q_TILE if n_sub == 1 else ATTN_Q_TILE // 2, tk))
    cw = _tile(tq, ATTN_CHAIN)
    assert tq % tk == 0 and tk % CHUNK == 0 and cw % CHUNK == 0
    nq, nk = seq // tq, seq // tk
    n_chain = tq // cw
    kern = functools.partial(_flash_kernel, tq=tq, tk=tk, cw=cw, dk=dk, dv=dv, groups=groups,
                             n_maps=n_maps, lam_init=lam_init)
    in_specs = [pl.BlockSpec((tq, n_sub * dk), lambda b, h, i: (b * nq + i, h)),
                pl.BlockSpec((seq, n_maps * dk), lambda b, h, i: (b, h)),
                pl.BlockSpec((nk, dva, tk), lambda b, h, i: (b, h, 0))]
    args = [q, k, vt]
    if n_maps == 2:
        in_specs = [pl.BlockSpec(lam.shape, lambda b, h, i: (0, 0)),
                    pl.BlockSpec(subln.shape, lambda b, h, i: (0, 0))] + in_specs
        args = [lam, subln] + args
    return pl.pallas_call(
        kern, grid=(batch, n_kv_heads, nq), in_specs=in_specs,
        out_specs=pl.BlockSpec((tq, groups * dv), lambda b, h, i: (b * nq + i, h)),
        out_shape=jax.ShapeDtypeStruct((batch * seq, n_kv_heads * groups * dv), BF16),
        scratch_shapes=[pltpu.VMEM((n_sub * n_chain, 1, cw), F32),
                        pltpu.VMEM((n_sub * n_chain, dva, cw), F32)],
        compiler_params=_cparams(3), name="flash_attention",
    )(*args)


def _decode_kernel(*refs, dk, dv, groups, n_maps, lam_init):
    if n_maps == 2:
        lam_ref, subln_ref, q_ref, kc_ref, vc_ref, kn_ref, vn_ref, o_ref = refs
    else:
        q_ref, kc_ref, vc_ref, kn_ref, vn_ref, o_ref = refs
        lam_ref = subln_ref = None
    nt = (((1,), (1,)), ((), ()))
    vc = vc_ref[...].astype(BF16)
    vn = vn_ref[...].astype(BF16)
    outs = []
    for u in range(groups * n_maps):
        mi = u % n_maps
        q = q_ref[:, u * dk:(u + 1) * dk]
        kc = kc_ref[:, mi * dk:(mi + 1) * dk].astype(BF16)
        kn = kn_ref[:, mi * dk:(mi + 1) * dk].astype(BF16)
        s1 = lax.dot_general(q, kc, nt, preferred_element_type=F32)
        s2 = lax.dot_general(q, kn, nt, preferred_element_type=F32)
        m = jnp.maximum(jnp.max(s1, axis=1, keepdims=True), jnp.max(s2, axis=1, keepdims=True))
        p1 = jnp.exp2(s1 - m)
        p2 = jnp.exp2(s2 - m)
        l = jnp.sum(p1, axis=1, keepdims=True) + jnp.sum(p2, axis=1, keepdims=True)
        o = (jnp.dot(p1.astype(BF16), vc, preferred_element_type=F32)
             + jnp.dot(p2.astype(BF16), vn, preferred_element_type=F32))
        outs.append(o / l)
    _combine_heads(outs, o_ref, slice(None), lam_ref, subln_ref, n_maps=n_maps, dv=dv, lam_init=lam_init)


def _decode_attention(q, kc, vc, kn, vn, *, batch, past, new, n_kv_heads, dk, dv, groups, n_maps,
                      lam=None, subln=None, lam_init=0.0):
    n_sub = groups * n_maps
    kern = functools.partial(_decode_kernel, dk=dk, dv=dv, groups=groups, n_maps=n_maps, lam_init=lam_init)
    bh = lambda b, h: (b, h)
    in_specs = [pl.BlockSpec((new, n_sub * dk), bh), pl.BlockSpec((past, n_maps * dk), bh),
                pl.BlockSpec((past, dv), bh), pl.BlockSpec((new, n_maps * dk), bh),
                pl.BlockSpec((new, dv), bh)]
    args = [q, kc, vc, kn, vn]
    if n_maps == 2:
        in_specs = [pl.BlockSpec(lam.shape, lambda b, h: (0, 0)),
                    pl.BlockSpec(subln.shape, lambda b, h: (0, 0))] + in_specs
        args = [lam, subln] + args
    return pl.pallas_call(
        kern, grid=(batch, n_kv_heads), in_specs=in_specs,
        out_specs=pl.BlockSpec((new, groups * dv), bh),
        out_shape=jax.ShapeDtypeStruct((batch * new, n_kv_heads * groups * dv), BF16),
        compiler_params=_cparams(2), name="decode_attention",
    )(*args)


def _swa_kernel(sinks_ref, q_ref, ka_ref, kb_ref, va_ref, vb_ref, o_ref, *, banded, blocks_per_seq):
    nq = q_ref.shape[0]
    na, nb = ka_ref.shape[0], kb_ref.shape[0]
    group = SWA_HEADS // SWA_KV_HEADS
    d = SWA_HEAD_DIM
    scale = d ** -0.5
    nt = (((1,), (1,)), ((), ()))
    if banded:
        rows = lax.broadcasted_iota(jnp.int32, (group * nq, na + nb), 0)
        cols = lax.broadcasted_iota(jnp.int32, (group * nq, na + nb), 1)
        qc = (rows % nq) // CHUNK
        kc = cols // CHUNK
        first = (pl.program_id(0) % blocks_per_seq) == 0
        lo = jnp.where(first, na // CHUNK, 0)
        valid = (kc >= qc) & (kc <= qc + na // CHUNK) & (kc >= lo)
    for h in range(SWA_KV_HEADS):
        hs = slice(h * d, (h + 1) * d)
        k = jnp.concatenate([ka_ref[:, hs], kb_ref[:, hs]], axis=0).astype(BF16)
        v = jnp.concatenate([va_ref[:, hs], vb_ref[:, hs]], axis=0).astype(BF16)
        q = jnp.concatenate([q_ref[:, (h * group + g) * d:(h * group + g + 1) * d] for g in range(group)],
                            axis=0)
        sink = jnp.concatenate([jnp.full((nq, 1), sinks_ref[h * group + g], F32) for g in range(group)],
                               axis=0)
        s = lax.dot_general(q, k, nt, preferred_element_type=F32) * scale
        if banded:
            s = jnp.where(valid, s, -jnp.inf)
        m = jnp.maximum(jnp.max(s, axis=1, keepdims=True), sink)
        p = jnp.exp(s - m)
        l = jnp.sum(p, axis=1, keepdims=True) + jnp.exp(sink - m)
        o = jnp.dot(p.astype(BF16), v, preferred_element_type=F32) / l
        for g in range(0, group, 2):
            pair = jnp.concatenate([o[g * nq:(g + 1) * nq], o[(g + 1) * nq:(g + 2) * nq]], axis=1)
            c0 = (h * group + g) * d
            o_ref[:, c0:c0 + 2 * d] = pair.astype(o_ref.dtype)


def _swa_attention(q, ka, kb, va, vb, sinks, *, nq, na, nb, banded, blocks_per_seq):
    m, qw = q.shape
    kvw = ka.shape[1]
    n_steps = m // nq
    cur = lambda t: (t, 0)
    a_map = (lambda t: (jnp.maximum(t - 1, 0), 0)) if banded else cur
    return pl.pallas_call(
        functools.partial(_swa_kernel, banded=banded, blocks_per_seq=blocks_per_seq),
        grid=(n_steps,),
        in_specs=[pl.BlockSpec(memory_space=pltpu.SMEM), pl.BlockSpec((nq, qw), cur),
                  pl.BlockSpec((na, kvw), a_map), pl.BlockSpec((nb, kvw), cur),
                  pl.BlockSpec((na, kvw), a_map), pl.BlockSpec((nb, kvw), cur)],
        out_specs=pl.BlockSpec((nq, qw), cur),
        out_shape=jax.ShapeDtypeStruct((m, qw), BF16),
        compiler_params=_cparams(1), name="swa_attention",
    )(sinks, q, ka, kb, va, vb)


def _ffn_up_kernel(x_ref, g_ref, wg_ref, wu_ref, cw_ref, cb_ref, prev_ref, act_ref, tail_ref,
                   h_ref, gbuf_ref, carry_ref, *, rows, n_slab, tiles_per_seq):
    i, j = pl.program_id(0), pl.program_id(1)
    halo = SUBLANES

    @pl.when(j == 0)
    def _():
        h_ref[...] = _rms(x_ref[...], g_ref[...], NORM_EPS).astype(BF16)

    h = h_ref[...]
    gate = jnp.dot(h, wg_ref[...], preferred_element_type=F32)
    up = jnp.dot(h, wu_ref[...], preferred_element_type=F32)
    w0, w1, w2 = cw_ref[0:1, :], cw_ref[1:2, :], cw_ref[2:3, :]
    bias = cb_ref[...]
    for s in range(n_slab):
        gs = gate[s * rows:(s + 1) * rows]
        if n_slab == 1:
            is_start = (i % tiles_per_seq) == 0

            @pl.when(is_start)
            def _():
                gbuf_ref[0:halo, :] = prev_ref[0]

            @pl.when(jnp.logical_not(is_start))
            def _():
                gbuf_ref[0:halo, :] = carry_ref[j]
        else:
            gbuf_ref[0:halo, :] = prev_ref[s]
        gbuf_ref[halo:halo + rows, :] = gs
        conv = bias + gbuf_ref[halo - 2:halo - 2 + rows, :] * w0
        conv = conv + gbuf_ref[halo - 1:halo - 1 + rows, :] * w1
        conv = conv + gs * w2
        act = (conv * jax.nn.sigmoid(conv)) * up[s * rows:(s + 1) * rows]
        act_ref[s * rows:(s + 1) * rows, :] = act.astype(BF16)
        tail_ref[s] = gs[rows - halo:rows]
    if n_slab == 1:
        carry_ref[j] = gate[rows - halo:rows]


def _ffn_up(x, gain, wg, wu, conv_w, conv_b, prev, *, seq):
    m, d_model = x.shape
    d_ff = wg.shape[1]
    tm = _tile(m, FFN_ROW_TILE)
    tf = _tile(d_ff, FFN_COL_TILE)
    if tm >= seq:
        rows, n_slab, tiles_per_seq = seq, tm // seq, 1
        prev_spec = pl.BlockSpec((n_slab, SUBLANES, tf), lambda i, j: (i, 0, j))
    else:
        rows, n_slab, tiles_per_seq = tm, 1, seq // tm
        prev_spec = pl.BlockSpec((1, SUBLANES, tf), lambda i, j: (i // tiles_per_seq, 0, j))
    nj = d_ff // tf
    return pl.pallas_call(
        functools.partial(_ffn_up_kernel, rows=rows, n_slab=n_slab, tiles_per_seq=tiles_per_seq),
        grid=(m // tm, nj),
        in_specs=[pl.BlockSpec((tm, d_model), lambda i, j: (i, 0)),
                  pl.BlockSpec((1, d_model), lambda i, j: (0, 0)),
                  pl.BlockSpec((d_model, tf), lambda i, j: (0, j)),
                  pl.BlockSpec((d_model, tf), lambda i, j: (0, j)),
                  pl.BlockSpec((CONV_W, tf), lambda i, j: (0, j)),
                  pl.BlockSpec((1, tf), lambda i, j: (0, j)),
                  prev_spec],
        out_specs=[pl.BlockSpec((tm, tf), lambda i, j: (i, j)),
                   pl.BlockSpec((n_slab, SUBLANES, tf), lambda i, j: (i, 0, j))],
        out_shape=[jax.ShapeDtypeStruct((m, d_ff), BF16),
                   jax.ShapeDtypeStruct((m // rows, SUBLANES, d_ff), F32)],
        scratch_shapes=[pltpu.VMEM((tm, d_model), BF16), pltpu.VMEM((rows + SUBLANES, tf), F32),
                        pltpu.VMEM((nj, SUBLANES, tf), F32)],
        compiler_params=_cparams(2), name="ffn_up",
    )(x, gain, wg, wu, conv_w, conv_b, prev)


def _final_norm_kernel(x_ref, g_ref, o_ref):
    o_ref[...] = _rms(x_ref[...], g_ref[...], NORM_EPS)


def _final_norm(x, gain):
    m, d = x.shape
    tm = _tile(m, ROW_TILE)
    return pl.pallas_call(
        _final_norm_kernel, grid=(m // tm,),
        in_specs=[pl.BlockSpec((tm, d), lambda i: (i, 0)), pl.BlockSpec((1, d), lambda i: (0, 0))],
        out_specs=pl.BlockSpec((tm, d), lambda i: (i, 0)),
        out_shape=jax.ShapeDtypeStruct((m, d), F32),
        compiler_params=_cparams(1), name="final_norm",
    )(x, gain)


def _conv_ffn(y, seq, prev_state, gain, wg, wu, conv_w, conv_b, wd):
    n_seq = y.shape[0] // seq
    d_ff = wg.shape[1]
    prev = jnp.concatenate([jnp.zeros((n_seq, SUBLANES - (CONV_W - 1), d_ff), F32), prev_state], axis=1)
    act, tails = _ffn_up(y, gain, wg, wu, conv_w, conv_b, prev, seq=seq)
    y = _mm_res(act, wd, y, tm_pref=MM_TILE // 2)
    state = tails.reshape(n_seq, -1, SUBLANES, d_ff)[:, -1, SUBLANES - (CONV_W - 1):, :]
    return y, state


def kernel(x_prompt, x_sample, cache_mla_ckv, cache_mla_kpe, cache_swa_k, cache_swa_v, cache_diff_k,
           cache_diff_v, state_ffn_conv, norm_mix, norm_ffn, final_norm, mla_w_dq, mla_q_norm, mla_w_uq,
           mla_w_dkv, mla_kv_norm, mla_w_ukv, mla_w_o, swa_w_qkv, swa_sinks, swa_w_o, diff_w_qkv,
           diff_lambda_q1, diff_lambda_k1, diff_lambda_q2, diff_lambda_k2, diff_subln, diff_w_o,
           ffn_w_gate, ffn_w_up, ffn_conv_w, ffn_conv_b, ffn_w_down):
    bp, sp, d_model = x_prompt.shape
    bs, ss, _ = x_sample.shape
    depth = norm_mix.shape[0]
    past = cache_mla_ckv.shape[2]
    d_ff = ffn_w_gate.shape[2]
    q_rank = mla_w_dq.shape[2]
    kv_rank = mla_w_ukv.shape[1]

    yp = x_prompt.reshape(bp * sp, d_model)
    ys = x_sample.reshape(bs * ss, d_model)
    row_p = _tile(bp * sp, ROW_TILE)
    row_s = _tile(bs * ss, ROW_TILE)

    def tables(head_dim):
        return (_rope_tables(sp, 0, head_dim, row_p), _rope_tables(ss, past, head_dim, row_s))

    tab64, tab128 = tables(64), tables(128)
    outs_p = {k: [] for k in ("ckv", "kpe", "swk", "swv", "dk", "dv", "conv")}
    outs_s = {k: [] for k in ("ckv", "kpe", "swk", "swv", "dk", "dv", "conv")}

    for i in range(depth):
        kind, j = i % N_MIXERS, i // N_MIXERS
        gain = norm_mix[i][None, :]
        if kind == 0:
            q_scale = (MLA_NOPE + MLA_ROPE) ** -0.5 * LOG2E
            pad = (-(q_rank + kv_rank + MLA_ROPE)) % LANES
            w_cat = jnp.concatenate([mla_w_dq[j], mla_w_dkv[j], jnp.zeros((d_model, pad), F32)],
                                    axis=1).astype(BF16)
            w_uq = mla_w_uq[j].reshape(q_rank, MLA_HEADS, MLA_NOPE + MLA_ROPE)
            w_uq = jnp.pad(w_uq, ((0, 0), (0, 0), (0, MLA_QK_PAD - MLA_NOPE - MLA_ROPE)))
            w_uq = w_uq.reshape(q_rank, MLA_HEADS * MLA_QK_PAD).astype(BF16)
            w_ukv = mla_w_ukv[j].reshape(kv_rank, MLA_HEADS, MLA_NOPE + MLA_V)
            wk = w_ukv[:, :, :MLA_NOPE].reshape(kv_rank, MLA_HEADS * MLA_NOPE).astype(BF16)
            wv = w_ukv[:, :, MLA_NOPE:].reshape(kv_rank, MLA_HEADS * MLA_V).astype(BF16)
            w_o = mla_w_o[j].astype(BF16)
            qn, kvn = mla_q_norm[j][None, :], mla_kv_norm[j][None, :]

            (cos_p, sin_p), (cos_s, sin_s) = tab64
            qa, ckv_p, kpe_p = _mla_down(yp, gain, w_cat, qn, kvn, cos_p, sin_p, q_rank, kv_rank)
            q = _mla_q_up(qa, w_uq, cos_p, sin_p, q_scale)
            k, vt = _mla_expand(ckv_p, kpe_p, wk, wv.T, v_transposed=True)
            o = _flash_attention(q, k, vt, batch=bp, seq=sp, n_kv_heads=MLA_HEADS, dk=MLA_QK_PAD,
                                 dv=MLA_V, groups=1, n_maps=1)
            yp = _mm_res(o, w_o, yp)

            qa, ckv_s, kpe_s = _mla_down(ys, gain, w_cat, qn, kvn, cos_s, sin_s, q_rank, kv_rank)
            q = _mla_q_up(qa, w_uq, cos_s, sin_s, q_scale)
            kn, vn = _mla_expand(ckv_s, kpe_s, wk, wv, v_transposed=False)
            kc, vc = _mla_expand(cache_mla_ckv[j].reshape(bs * past, kv_rank),
                                 cache_mla_kpe[j].reshape(bs * past, MLA_ROPE), wk, wv, v_transposed=False)
            o = _decode_attention(q, kc, vc, kn, vn, batch=bs, past=past, new=ss, n_kv_heads=MLA_HEADS,
                                  dk=MLA_QK_PAD, dv=MLA_V, groups=1, n_maps=1)
            ys = _mm_res(o, w_o, ys)
            outs_p["ckv"].append(ckv_p.reshape(bp, sp, kv_rank))
            outs_p["kpe"].append(kpe_p.reshape(bp, sp, MLA_ROPE))
            outs_s["ckv"].append(ckv_s.reshape(bs, ss, kv_rank))
            outs_s["kpe"].append(kpe_s.reshape(bs, ss, MLA_ROPE))
        elif kind == 1:
            qw, kw = SWA_HEADS * SWA_HEAD_DIM, SWA_KV_HEADS * SWA_HEAD_DIM
            w_qkv = swa_w_qkv[j].astype(BF16)
            w_o = swa_w_o[j].astype(BF16)
            sinks = swa_sinks[j]
            (cos_p, sin_p), (cos_s, sin_s) = tab64
            q, k, v, _, _ = _qkv_proj(yp, gain, w_qkv, cos_p, sin_p, qw, kw, kw, SWA_HEAD_DIM)
            o = _swa_attention(q, k, k, v, v, sinks, nq=SWA_Q_TILE, na=WINDOW, nb=SWA_Q_TILE, banded=True,
                               blocks_per_seq=sp // SWA_Q_TILE)
            yp = _mm_res(o, w_o, yp)
            k3 = k.reshape(bp, sp, SWA_KV_HEADS, SWA_HEAD_DIM)
            v3 = v.reshape(bp, sp, SWA_KV_HEADS, SWA_HEAD_DIM)
            outs_p["swk"].append(k3[:, sp - WINDOW:])
            outs_p["swv"].append(v3[:, sp - WINDOW:])

            q, k, v, _, _ = _qkv_proj(ys, gain, w_qkv, cos_s, sin_s, qw, kw, kw, SWA_HEAD_DIM)
            kc = cache_swa_k[j].reshape(bs * WINDOW, kw)
            vc = cache_swa_v[j].reshape(bs * WINDOW, kw)
            o = _swa_attention(q, kc, k, vc, v, sinks, nq=ss, na=WINDOW, nb=ss, banded=False,
                               blocks_per_seq=1)
            ys = _mm_res(o, w_o, ys)
            k_all = jnp.concatenate([cache_swa_k[j], k.reshape(bs, ss, SWA_KV_HEADS, SWA_HEAD_DIM)], axis=1)
            v_all = jnp.concatenate([cache_swa_v[j], v.reshape(bs, ss, SWA_KV_HEADS, SWA_HEAD_DIM)], axis=1)
            outs_s["swk"].append(k_all[:, ss:])
            outs_s["swv"].append(v_all[:, ss:])
        else:
            lam_init = 0.8 - 0.6 * math.exp(-0.3 * i)
            q_scale = DIFF_HEAD_DIM ** -0.5 * LOG2E
            groups = DIFF_HEADS // DIFF_KV_HEADS
            qw = DIFF_HEADS * 2 * DIFF_HEAD_DIM
            kw = DIFF_KV_HEADS * 2 * DIFF_HEAD_DIM
            w_qkv = diff_w_qkv[j].astype(BF16)
            w_o = diff_w_o[j].astype(BF16)
            lam = jnp.stack([diff_lambda_q1[j], diff_lambda_k1[j], diff_lambda_q2[j], diff_lambda_k2[j]])
            subln = diff_subln[j][None, :]
            common = dict(n_kv_heads=DIFF_KV_HEADS, dk=DIFF_HEAD_DIM, dv=2 * DIFF_HEAD_DIM, groups=groups,
                          n_maps=2, lam=lam, subln=subln, lam_init=lam_init)
            (cos_p, sin_p), (cos_s, sin_s) = tab128
            q, k, v, kb, vt = _qkv_proj(yp, gain, w_qkv, cos_p, sin_p, qw, kw, kw, DIFF_HEAD_DIM,
                                        q_scale=q_scale, v_head_dim=2 * DIFF_HEAD_DIM)
            o = _flash_attention(q, kb, vt, batch=bp, seq=sp, **common)
            yp = _mm_res(o, w_o, yp)
            outs_p["dk"].append(k.reshape(bp, sp, DIFF_KV_HEADS, 2, DIFF_HEAD_DIM))
            outs_p["dv"].append(v.reshape(bp, sp, DIFF_KV_HEADS, 2 * DIFF_HEAD_DIM))

            q, k, v, _, _ = _qkv_proj(ys, gain, w_qkv, cos_s, sin_s, qw, kw, kw, DIFF_HEAD_DIM,
                                      q_scale=q_scale)
            o = _decode_attention(q, cache_diff_k[j].reshape(bs * past, kw),
                                  cache_diff_v[j].reshape(bs * past, kw), k, v,
                                  batch=bs, past=past, new=ss, **common)
            ys = _mm_res(o, w_o, ys)
            outs_s["dk"].append(k.reshape(bs, ss, DIFF_KV_HEADS, 2, DIFF_HEAD_DIM))
            outs_s["dv"].append(v.reshape(bs, ss, DIFF_KV_HEADS, 2 * DIFF_HEAD_DIM))

        ffn_gain = norm_ffn[i][None, :]
        wg, wu, wd = ffn_w_gate[i].astype(BF16), ffn_w_up[i].astype(BF16), ffn_w_down[i].astype(BF16)
        cb = ffn_conv_b[i][None, :]
        yp, conv_p = _conv_ffn(yp, sp, jnp.zeros((bp, CONV_W - 1, d_ff), F32), ffn_gain, wg, wu,
                               ffn_conv_w[i], cb, wd)
        ys, conv_s = _conv_ffn(ys, ss, state_ffn_conv[i], ffn_gain, wg, wu, ffn_conv_w[i], cb, wd)
        outs_p["conv"].append(conv_p)
        outs_s["conv"].append(conv_s)

    fg = final_norm[None, :]
    y_prompt = _final_norm(yp, fg).reshape(bp, sp, d_model)
    y_sample = _final_norm(ys, fg).reshape(bs, ss, d_model)
    order = ("ckv", "kpe", "swk", "swv", "dk", "dv", "conv")
    return (y_prompt, y_sample) + tuple(jnp.stack(outs_p[k]) for k in order) + tuple(
        jnp.stack(outs_s[k]) for k in order)
```

```python
import functools
import math

import jax
import jax.numpy as jnp
from jax import lax
from jax.experimental import pallas as pl
from jax.experimental.pallas import tpu as pltpu

F32 = jnp.float32
BF16 = jnp.bfloat16

CHUNK = 64
ROPE_THETA = 10000.0
NORM_EPS = 1e-6
SUBLN_EPS = 1e-5
N_MIXERS = 3
MLA_HEADS = 16
MLA_NOPE = 128
MLA_ROPE = 64
MLA_V = 128
SWA_HEADS = 32
SWA_KV_HEADS = 4
SWA_HEAD_DIM = 64
WINDOW = 128
DIFF_HEADS = 8
DIFF_KV_HEADS = 4
DIFF_HEAD_DIM = 128
CONV_W = 3

LANES = 128
SUBLANES = 8
MXU_COLS = 256
MLA_QK_PAD = MXU_COLS
VMEM_LIMIT_BYTES = 56 * 1024 * 1024
LOG2E = 1.4426950408889634

ROW_TILE = 512
FFN_ROW_TILE = 1024
FFN_COL_TILE = 512
FFN_SUB_TILE = 256
ATTN_Q_TILE = 2048
ATTN_KV_TILE = 512
ATTN_CHAIN = 256
ATTN_LOOKAHEAD = 3
ONES_ROWS = 16
MM_TILE = 1024
MM_SUB_TILE = 256
SWA_Q_TILE = 2 * CHUNK


def _cparams(n_axes):
    return pltpu.CompilerParams(dimension_semantics=("arbitrary",) * n_axes,
                                vmem_limit_bytes=VMEM_LIMIT_BYTES)


def _resident(shape):
    return pl.BlockSpec(shape, lambda *_: (0,) * len(shape), pipeline_mode=pl.Buffered(1))


def _tile(n, pref):
    t = min(n, pref)
    assert n % t == 0, (n, pref)
    return t


def _rms(x, g, eps):
    ms = jnp.mean(x * x, axis=-1, keepdims=True)
    return (x * lax.rsqrt(ms + eps)) * g


def _rope_slab(x, cos, sin, head_dim):
    if head_dim == LANES:
        swapped = pltpu.roll(x, LANES // 2, axis=1)
    else:
        half = head_dim // 2
        lane = lax.broadcasted_iota(jnp.int32, x.shape, 1)
        first = (lane & (head_dim - 1)) < half
        swapped = jnp.where(first, pltpu.roll(x, LANES - half, axis=1), pltpu.roll(x, half, axis=1))
    return x * cos + swapped * sin


def _rope_tables(seq_len, offset, head_dim, rows):
    half = head_dim // 2
    pos = jnp.arange(seq_len, dtype=F32) + offset
    inv = ROPE_THETA ** (-jnp.arange(0, head_dim, 2, dtype=F32) / head_dim)
    ang = pos[:, None] * inv[None, :]
    cos, sin = jnp.cos(ang), jnp.sin(ang)
    reps = LANES // head_dim
    cos_l = jnp.tile(jnp.concatenate([cos, cos], axis=1), (1, reps))
    sin_l = jnp.tile(jnp.concatenate([-sin, sin], axis=1), (1, reps))
    n = max(rows // seq_len, 1)
    return jnp.tile(cos_l, (n, 1)), jnp.tile(sin_l, (n, 1))


def _mla_down_kernel(x_ref, g_ref, w_ref, qn_ref, kvn_ref, cos_ref, sin_ref,
                     qa_ref, ckv_ref, kpe_ref, *, q_rank, kv_rank):
    h = _rms(x_ref[...], g_ref[...], NORM_EPS).astype(BF16)
    d = jnp.dot(h, w_ref[...], preferred_element_type=F32)
    qa_ref[...] = _rms(d[:, :q_rank], qn_ref[...], NORM_EPS).astype(BF16)
    ckv_ref[...] = _rms(d[:, q_rank:q_rank + kv_rank], kvn_ref[...], NORM_EPS)
    slab = d[:, q_rank + kv_rank:q_rank + kv_rank + LANES]
    kpe_ref[...] = _rope_slab(slab, cos_ref[...], sin_ref[...], MLA_ROPE)[:, :MLA_ROPE]


def _mla_down(x, gain, w_cat, q_norm, kv_norm, cos, sin, q_rank, kv_rank):
    m, d_model = x.shape
    tm = _tile(m, ROW_TILE)
    n_tab = cos.shape[0] // tm
    wn = w_cat.shape[1]
    row = lambda i: (i, 0)
    fixed = lambda i: (0, 0)
    tab = lambda i: (i % n_tab, 0)
    return pl.pallas_call(
        functools.partial(_mla_down_kernel, q_rank=q_rank, kv_rank=kv_rank),
        grid=(m // tm,),
        in_specs=[pl.BlockSpec((tm, d_model), row), pl.BlockSpec((1, d_model), fixed),
                  _resident((d_model, wn)), pl.BlockSpec((1, q_rank), fixed),
                  pl.BlockSpec((1, kv_rank), fixed), pl.BlockSpec((tm, LANES), tab),
                  pl.BlockSpec((tm, LANES), tab)],
        out_specs=[pl.BlockSpec((tm, q_rank), row), pl.BlockSpec((tm, kv_rank), row),
                   pl.BlockSpec((tm, MLA_ROPE), row)],
        out_shape=[jax.ShapeDtypeStruct((m, q_rank), BF16), jax.ShapeDtypeStruct((m, kv_rank), F32),
                   jax.ShapeDtypeStruct((m, MLA_ROPE), F32)],
        compiler_params=_cparams(1), name="mla_down",
    )(x, gain, w_cat, q_norm, kv_norm, cos, sin)


def _mla_q_up_kernel(qa_ref, w_ref, cos_ref, sin_ref, q_ref, *, q_scale):
    qa = qa_ref[...]
    cos, sin = cos_ref[...], sin_ref[...]
    for h in range(MLA_HEADS):
        c0 = h * MLA_QK_PAD
        d = jnp.dot(qa, w_ref[:, c0:c0 + MLA_QK_PAD], preferred_element_type=F32)
        q_ref[:, c0:c0 + LANES] = (d[:, :LANES] * q_scale).astype(BF16)
        q_ref[:, c0 + LANES:c0 + MLA_QK_PAD] = (_rope_slab(d[:, LANES:], cos, sin, MLA_ROPE) * q_scale).astype(BF16)


def _mla_q_up(qa, w_pad, cos, sin, q_scale):
    m, q_rank = qa.shape
    tm = _tile(m, ROW_TILE)
    n_tab = cos.shape[0] // tm
    n = w_pad.shape[1]
    row = lambda i: (i, 0)
    fixed = lambda i: (0, 0)
    tab = lambda i: (i % n_tab, 0)
    return pl.pallas_call(
        functools.partial(_mla_q_up_kernel, q_scale=q_scale), grid=(m // tm,),
        in_specs=[pl.BlockSpec((tm, q_rank), row), _resident((q_rank, n)),
                  pl.BlockSpec((tm, LANES), tab), pl.BlockSpec((tm, LANES), tab)],
        out_specs=pl.BlockSpec((tm, n), row),
        out_shape=jax.ShapeDtypeStruct((m, n), BF16),
        compiler_params=_cparams(1), name="mla_q_up",
    )(qa, w_pad, cos, sin)


def _mla_expand_kernel(ckv_ref, kpe_ref, wk_ref, wv_ref, k_ref, v_ref, *, v_transposed):
    c = ckv_ref[...].astype(BF16)
    kpe = kpe_ref[...].astype(BF16)
    if v_transposed:
        vt = lax.dot_general(wv_ref[...], c, (((1,), (1,)), ((), ())), preferred_element_type=F32)
        dva = MLA_V + ONES_ROWS
        for h in range(MLA_HEADS):
            v_ref[0, h * dva:h * dva + MLA_V, :] = vt[h * MLA_V:(h + 1) * MLA_V].astype(BF16)
            v_ref[0, h * dva + MLA_V:(h + 1) * dva, :] = jnp.ones((ONES_ROWS, vt.shape[1]), BF16)
    else:
        v_ref[...] = jnp.dot(c, wv_ref[...], preferred_element_type=F32).astype(BF16)
    kn = jnp.dot(c, wk_ref[...], preferred_element_type=F32).astype(BF16)
    zeros = jnp.zeros((kpe.shape[0], MLA_QK_PAD - MLA_NOPE - MLA_ROPE), BF16)
    for h in range(MLA_HEADS):
        c0 = h * MLA_QK_PAD
        k_ref[:, c0:c0 + MLA_NOPE] = kn[:, h * MLA_NOPE:(h + 1) * MLA_NOPE]
        k_ref[:, c0 + MLA_NOPE:c0 + MLA_NOPE + MLA_ROPE] = kpe
        k_ref[:, c0 + MLA_NOPE + MLA_ROPE:c0 + MLA_QK_PAD] = zeros


def _mla_expand(ckv, kpe, wk, wv, *, v_transposed):
    m, kv_rank = ckv.shape
    tm = _tile(m, ATTN_KV_TILE if v_transposed else ROW_TILE)
    row = lambda i: (i, 0)
    nk, nv = MLA_HEADS * MLA_QK_PAD, MLA_HEADS * MLA_V
    if v_transposed:
        nva = MLA_HEADS * (MLA_V + ONES_ROWS)
        v_spec = pl.BlockSpec((1, nva, tm), lambda i: (i, 0, 0))
        v_shape = jax.ShapeDtypeStruct((m // tm, nva, tm), BF16)
    else:
        v_spec = pl.BlockSpec((tm, nv), row)
        v_shape = jax.ShapeDtypeStruct((m, nv), BF16)
    return pl.pallas_call(
        functools.partial(_mla_expand_kernel, v_transposed=v_transposed), grid=(m // tm,),
        in_specs=[pl.BlockSpec((tm, kv_rank), row), pl.BlockSpec((tm, MLA_ROPE), row),
                  _resident(wk.shape), _resident(wv.shape)],
        out_specs=[pl.BlockSpec((tm, nk), row), v_spec],
        out_shape=[jax.ShapeDtypeStruct((m, nk), BF16), v_shape],
        compiler_params=_cparams(1), name="mla_expand",
    )(ckv, kpe, wk, wv)


def _qkv_proj_kernel(x_ref, g_ref, w_ref, cos_ref, sin_ref, q_ref, k_ref, v_ref, kb_ref, vb_ref,
                     *, qw, kw, vw, head_dim, q_scale, v_head_dim):
    h = _rms(x_ref[...], g_ref[...], NORM_EPS).astype(BF16)
    cos, sin = cos_ref[...], sin_ref[...]
    for n0 in range(0, qw + kw + vw, MXU_COLS):
        dd = jnp.dot(h, w_ref[:, n0:n0 + MXU_COLS], preferred_element_type=F32)
        for c0 in range(n0, n0 + MXU_COLS, LANES):
            d = dd[:, c0 - n0:c0 - n0 + LANES]
            if c0 < qw:
                q_ref[:, c0:c0 + LANES] = (_rope_slab(d, cos, sin, head_dim) * q_scale).astype(BF16)
            elif c0 < qw + kw:
                r = _rope_slab(d, cos, sin, head_dim)
                k_ref[:, c0 - qw:c0 - qw + LANES] = r
                kb_ref[:, c0 - qw:c0 - qw + LANES] = r.astype(BF16)
            else:
                c1 = c0 - qw - kw
                v_ref[:, c1:c1 + LANES] = d
                if v_head_dim:
                    r0 = (c1 // v_head_dim) * (v_head_dim + ONES_ROWS) + c1 % v_head_dim
                    vb_ref[0, r0:r0 + LANES, :] = d.T.astype(BF16)
                    if (c1 + LANES) % v_head_dim == 0:
                        vb_ref[0, r0 + LANES:r0 + LANES + ONES_ROWS, :] = jnp.ones((ONES_ROWS, d.shape[0]), BF16)
                else:
                    vb_ref[:, c1:c1 + LANES] = d.astype(BF16)


def _qkv_proj(x, gain, w, cos, sin, qw, kw, vw, head_dim, *, q_scale=1.0, v_head_dim=None):
    m, d_model = x.shape
    tm = _tile(m, ATTN_KV_TILE if v_head_dim else ROW_TILE)
    n_tab = cos.shape[0] // tm
    row = lambda i: (i, 0)
    fixed = lambda i: (0, 0)
    tab = lambda i: (i % n_tab, 0)
    if v_head_dim:
        vwa = vw // v_head_dim * (v_head_dim + ONES_ROWS)
        vb_spec = pl.BlockSpec((1, vwa, tm), lambda i: (i, 0, 0))
        vb_shape = jax.ShapeDtypeStruct((m // tm, vwa, tm), BF16)
    else:
        vb_spec = pl.BlockSpec((tm, vw), row)
        vb_shape = jax.ShapeDtypeStruct((m, vw), BF16)
    return pl.pallas_call(
        functools.partial(_qkv_proj_kernel, qw=qw, kw=kw, vw=vw, head_dim=head_dim, q_scale=q_scale,
                          v_head_dim=v_head_dim),
        grid=(m // tm,),
        in_specs=[pl.BlockSpec((tm, d_model), row), pl.BlockSpec((1, d_model), fixed),
                  _resident(w.shape), pl.BlockSpec((tm, LANES), tab),
                  pl.BlockSpec((tm, LANES), tab)],
        out_specs=[pl.BlockSpec((tm, qw), row), pl.BlockSpec((tm, kw), row), pl.BlockSpec((tm, vw), row),
                   pl.BlockSpec((tm, kw), row), vb_spec],
        out_shape=[jax.ShapeDtypeStruct((m, qw), BF16), jax.ShapeDtypeStruct((m, kw), F32),
                   jax.ShapeDtypeStruct((m, vw), F32), jax.ShapeDtypeStruct((m, kw), BF16), vb_shape],
        compiler_params=_cparams(1), name="qkv_proj",
    )(x, gain, w, cos, sin)


def _mm_res_kernel(a_ref, w_ref, r_ref, o_ref, *, sub):
    starts = list(range(0, a_ref.shape[0], sub))
    pending = jnp.dot(a_ref[0:sub, :], w_ref[...], preferred_element_type=F32)
    for t, r0 in enumerate(starts):
        d = pending
        if t + 1 < len(starts):
            r1 = starts[t + 1]
            pending = jnp.dot(a_ref[r1:r1 + sub, :], w_ref[...], preferred_element_type=F32)
        o_ref[r0:r0 + sub, :] = r_ref[r0:r0 + sub, :] + d


def _mm_res(a, w, res, tm_pref=MM_TILE, tn_pref=MM_TILE):
    m, k = a.shape
    n = w.shape[1]
    tm, tn = _tile(m, tm_pref), _tile(n, tn_pref)
    return pl.pallas_call(
        functools.partial(_mm_res_kernel, sub=_tile(tm, MM_SUB_TILE)), grid=(m // tm, n // tn),
        in_specs=[pl.BlockSpec((tm, k), lambda i, j: (i, 0)), pl.BlockSpec((k, tn), lambda i, j: (0, j)),
                  pl.BlockSpec((tm, tn), lambda i, j: (i, j))],
        out_specs=pl.BlockSpec((tm, tn), lambda i, j: (i, j)),
        out_shape=jax.ShapeDtypeStruct((m, n), F32),
        compiler_params=_cparams(2), name="mm_res",
    )(a, w, res)


def _diff_lambda(lam_ref, lam_init):
    lam = lam_ref[...]
    a = jnp.sum(lam[0:1] * lam[1:2], axis=-1, keepdims=True)
    b = jnp.sum(lam[2:3] * lam[3:4], axis=-1, keepdims=True)
    return jnp.exp(a) - jnp.exp(b) + lam_init


def _combine_heads(o_list, o_ref, rows, lam_ref, subln_ref, *, n_maps, dv, lam_init):
    groups = len(o_list) // n_maps
    for g in range(groups):
        if n_maps == 1:
            o = o_list[g]
        else:
            lam = _diff_lambda(lam_ref, lam_init)
            o = o_list[2 * g] - lam * o_list[2 * g + 1]
            o = _rms(o, subln_ref[...], SUBLN_EPS) * (1.0 - lam_init)
        o_ref[rows, g * dv:(g + 1) * dv] = o.astype(o_ref.dtype)


def _flash_kernel(*refs, tq, tk, cw, dk, dv, groups, n_maps, lam_init):
    if n_maps == 2:
        lam_ref, subln_ref, q_ref, k_ref, vt_ref, o_ref, m_ref, acc_ref, st_ref = refs
    else:
        q_ref, k_ref, vt_ref, o_ref, m_ref, acc_ref, st_ref = refs
        lam_ref = subln_ref = None
    n_sub = groups * n_maps
    n_chain = tq // cw
    kv_per_q = tq // tk
    qi = pl.program_id(2)
    nt = (((1,), (1,)), ((), ()))

    m_ref[...] = jnp.full(m_ref.shape, -jnp.inf, F32)
    acc_ref[...] = jnp.zeros(acc_ref.shape, F32)

    def scores(item, k_blks, slot):
        u, n, k_lo, bi = item
        mi = u % n_maps
        q_n = q_ref[n * cw:(n + 1) * cw, u * dk:(u + 1) * dk]
        st = lax.dot_general(k_blks[bi][:, mi * dk:(mi + 1) * dk], q_n, nt, preferred_element_type=F32)
        if k_lo is not None:
            kc = (lax.broadcasted_iota(jnp.int32, st.shape, 0) + k_lo) // CHUNK
            qc = (lax.broadcasted_iota(jnp.int32, st.shape, 1) + n * cw) // CHUNK
            st = jnp.where(kc <= qc, st, -jnp.inf)
        st_ref[slot] = st

    def update(item, slot, vt_blks):
        u, n, _, bi = item
        st = st_ref[slot]
        idx = u * n_chain + n
        m_prev = m_ref[idx]
        m_new = jnp.maximum(m_prev, jnp.max(st, axis=0, keepdims=True))
        alpha = jnp.exp2(m_prev - m_new)
        pt = jnp.exp2(st - m_new).astype(BF16)
        acc_ref[idx] = acc_ref[idx] * alpha + jnp.dot(vt_blks[bi], pt, preferred_element_type=F32)
        m_ref[idx] = m_new

    def run_chains(items, first_block):
        blocks = sorted({it[3] for it in items})
        k_blks = {bi: k_ref[pl.ds(pl.multiple_of((first_block + bi) * tk, tk), tk), :] for bi in blocks}
        vt_blks = {bi: vt_ref[first_block + bi] for bi in blocks}
        n_slots = ATTN_LOOKAHEAD + 1
        for t in range(min(ATTN_LOOKAHEAD, len(items))):
            scores(items[t], k_blks, t % n_slots)
        for t, item in enumerate(items):
            if t + ATTN_LOOKAHEAD < len(items):
                scores(items[t + ATTN_LOOKAHEAD], k_blks, (t + ATTN_LOOKAHEAD) % n_slots)
            update(item, t % n_slots, vt_blks)

    def body(j, carry):
        items = [(u, n, None, bi) for bi in range(kv_per_q) for n in range(n_chain) for u in range(n_sub)]
        run_chains(items, j * kv_per_q)
        return carry

    lax.fori_loop(0, qi, body, 0)

    items = [(u, n, bi * tk if (bi + 1) * tk > n * cw else None, bi)
             for bi in range(kv_per_q) for n in range(n_chain) if bi * tk < (n + 1) * cw
             for u in range(n_sub)]
    run_chains(items, qi * kv_per_q)

    for n in range(n_chain):
        outs = []
        for u in range(n_sub):
            a = acc_ref[u * n_chain + n]
            outs.append((a[:dv] / a[dv:dv + 1]).T)
        _combine_heads(outs, o_ref, slice(n * cw, (n + 1) * cw), lam_ref, subln_ref,
                       n_maps=n_maps, dv=dv, lam_init=lam_init)


def _flash_attention(q, k, vt, *, batch, seq, n_kv_heads, dk, dv, groups, n_maps,
                     lam=None, subln=None, lam_init=0.0):
    tk = vt.shape[2]
    dva = dv + ONES_ROWS
    n_sub = groups * n_maps
    tq = _tile(seq, max(ATTN_Q_TILE if n_sub == 1 else ATTN_Q_TILE // 2, tk))
    cw = _tile(tq, ATTN_CHAIN)
    assert tq % tk == 0 and tk % CHUNK == 0 and cw % CHUNK == 0
    nq, nk = seq // tq, seq // tk
    n_chain = tq // cw
    kern = functools.partial(_flash_kernel, tq=tq, tk=tk, cw=cw, dk=dk, dv=dv, groups=groups,
                             n_maps=n_maps, lam_init=lam_init)
    in_specs = [pl.BlockSpec((tq, n_sub * dk), lambda b, h, i: (b * nq + i, h)),
                pl.BlockSpec((seq, n_maps * dk), lambda b, h, i: (b, h)),
                pl.BlockSpec((nk, dva, tk), lambda b, h, i: (b, h, 0))]
    args = [q, k, vt]
    if n_maps == 2:
        in_specs = [pl.BlockSpec(lam.shape, lambda b, h, i: (0, 0)),
                    pl.BlockSpec(subln.shape, lambda b, h, i: (0, 0))] + in_specs
        args = [lam, subln] + args
    return pl.pallas_call(
        kern, grid=(batch, n_kv_heads, nq), in_specs=in_specs,
        out_specs=pl.BlockSpec((tq, groups * dv), lambda b, h, i: (b * nq + i, h)),
        out_shape=jax.ShapeDtypeStruct((batch * seq, n_kv_heads * groups * dv), BF16),
        scratch_shapes=[pltpu.VMEM((n_sub * n_chain, 1, cw), F32),
                        pltpu.VMEM((n_sub * n_chain, dva, cw), F32),
                        pltpu.VMEM((ATTN_LOOKAHEAD + 1, tk, cw), F32)],
        compiler_params=_cparams(3), name="flash_attention",
    )(*args)


def _decode_kernel(*refs, dk, dv, groups, n_maps, lam_init):
    if n_maps == 2:
        lam_ref, subln_ref, q_ref, kc_ref, vc_ref, kn_ref, vn_ref, o_ref = refs
    else:
        q_ref, kc_ref, vc_ref, kn_ref, vn_ref, o_ref = refs
        lam_ref = subln_ref = None
    nt = (((1,), (1,)), ((), ()))
    vc = vc_ref[...].astype(BF16)
    vn = vn_ref[...].astype(BF16)
    outs = []
    for u in range(groups * n_maps):
        mi = u % n_maps
        q = q_ref[:, u * dk:(u + 1) * dk]
        kc = kc_ref[:, mi * dk:(mi + 1) * dk].astype(BF16)
        kn = kn_ref[:, mi * dk:(mi + 1) * dk].astype(BF16)
        s1 = lax.dot_general(q, kc, nt, preferred_element_type=F32)
        s2 = lax.dot_general(q, kn, nt, preferred_element_type=F32)
        m = jnp.maximum(jnp.max(s1, axis=1, keepdims=True), jnp.max(s2, axis=1, keepdims=True))
        p1 = jnp.exp2(s1 - m)
        p2 = jnp.exp2(s2 - m)
        l = jnp.sum(p1, axis=1, keepdims=True) + jnp.sum(p2, axis=1, keepdims=True)
        o = (jnp.dot(p1.astype(BF16), vc, preferred_element_type=F32)
             + jnp.dot(p2.astype(BF16), vn, preferred_element_type=F32))
        outs.append(o / l)
    _combine_heads(outs, o_ref, slice(None), lam_ref, subln_ref, n_maps=n_maps, dv=dv, lam_init=lam_init)


def _decode_attention(q, kc, vc, kn, vn, *, batch, past, new, n_kv_heads, dk, dv, groups, n_maps,
                      lam=None, subln=None, lam_init=0.0):
    n_sub = groups * n_maps
    kern = functools.partial(_decode_kernel, dk=dk, dv=dv, groups=groups, n_maps=n_maps, lam_init=lam_init)
    bh = lambda b, h: (b, h)
    in_specs = [pl.BlockSpec((new, n_sub * dk), bh), pl.BlockSpec((past, n_maps * dk), bh),
                pl.BlockSpec((past, dv), bh), pl.BlockSpec((new, n_maps * dk), bh),
                pl.BlockSpec((new, dv), bh)]
    args = [q, kc, vc, kn, vn]
    if n_maps == 2:
        in_specs = [pl.BlockSpec(lam.shape, lambda b, h: (0, 0)),
                    pl.BlockSpec(subln.shape, lambda b, h: (0, 0))] + in_specs
        args = [lam, subln] + args
    return pl.pallas_call(
        kern, grid=(batch, n_kv_heads), in_specs=in_specs,
        out_specs=pl.BlockSpec((new, groups * dv), bh),
        out_shape=jax.ShapeDtypeStruct((batch * new, n_kv_heads * groups * dv), BF16),
        compiler_params=_cparams(2), name="decode_attention",
    )(*args)


def _swa_kernel(sinks_ref, q_ref, ka_ref, kb_ref, va_ref, vb_ref, o_ref, *, banded, blocks_per_seq):
    nq = q_ref.shape[0]
    na, nb = ka_ref.shape[0], kb_ref.shape[0]
    group = SWA_HEADS // SWA_KV_HEADS
    d = SWA_HEAD_DIM
    scale = d ** -0.5
    nt = (((1,), (1,)), ((), ()))
    if banded:
        rows = lax.broadcasted_iota(jnp.int32, (group * nq, na + nb), 0)
        cols = lax.broadcasted_iota(jnp.int32, (group * nq, na + nb), 1)
        qc = (rows % nq) // CHUNK
        kc = cols // CHUNK
        first = (pl.program_id(0) % blocks_per_seq) == 0
        lo = jnp.where(first, na // CHUNK, 0)
        valid = (kc >= qc) & (kc <= qc + na // CHUNK) & (kc >= lo)
    for h in range(SWA_KV_HEADS):
        hs = slice(h * d, (h + 1) * d)
        k = jnp.concatenate([ka_ref[:, hs], kb_ref[:, hs]], axis=0).astype(BF16)
        v = jnp.concatenate([va_ref[:, hs], vb_ref[:, hs]], axis=0).astype(BF16)
        q = jnp.concatenate([q_ref[:, (h * group + g) * d:(h * group + g + 1) * d] for g in range(group)],
                            axis=0)
        sink = jnp.concatenate([jnp.full((nq, 1), sinks_ref[h * group + g], F32) for g in range(group)],
                               axis=0)
        s = lax.dot_general(q, k, nt, preferred_element_type=F32) * scale
        if banded:
            s = jnp.where(valid, s, -jnp.inf)
        m = jnp.maximum(jnp.max(s, axis=1, keepdims=True), sink)
        p = jnp.exp(s - m)
        l = jnp.sum(p, axis=1, keepdims=True) + jnp.exp(sink - m)
        o = jnp.dot(p.astype(BF16), v, preferred_element_type=F32) / l
        for g in range(0, group, 2):
            pair = jnp.concatenate([o[g * nq:(g + 1) * nq], o[(g + 1) * nq:(g + 2) * nq]], axis=1)
            c0 = (h * group + g) * d
            o_ref[:, c0:c0 + 2 * d] = pair.astype(o_ref.dtype)


def _swa_attention(q, ka, kb, va, vb, sinks, *, nq, na, nb, banded, blocks_per_seq):
    m, qw = q.shape
    kvw = ka.shape[1]
    n_steps = m // nq
    cur = lambda t: (t, 0)
    a_map = (lambda t: (jnp.maximum(t - 1, 0), 0)) if banded else cur
    return pl.pallas_call(
        functools.partial(_swa_kernel, banded=banded, blocks_per_seq=blocks_per_seq),
        grid=(n_steps,),
        in_specs=[pl.BlockSpec(memory_space=pltpu.SMEM), pl.BlockSpec((nq, qw), cur),
                  pl.BlockSpec((na, kvw), a_map), pl.BlockSpec((nb, kvw), cur),
                  pl.BlockSpec((na, kvw), a_map), pl.BlockSpec((nb, kvw), cur)],
        out_specs=pl.BlockSpec((nq, qw), cur),
        out_shape=jax.ShapeDtypeStruct((m, qw), BF16),
        compiler_params=_cparams(1), name="swa_attention",
    )(sinks, q, ka, kb, va, vb)


def _ffn_up_kernel(x_ref, g_ref, wg_ref, wu_ref, cw_ref, cb_ref, prev_ref, act_ref, tail_ref,
                   h_ref, halo_ref, carry_ref, *, rows, n_slab, tiles_per_seq, sub):
    i, j = pl.program_id(0), pl.program_id(1)
    halo = SUBLANES

    @pl.when(j == 0)
    def _():
        h_ref[...] = _rms(x_ref[...], g_ref[...], NORM_EPS).astype(BF16)

    w0, w1, w2 = cw_ref[0:1, :], cw_ref[1:2, :], cw_ref[2:3, :]
    bias = cb_ref[...]
    tm = h_ref.shape[0]

    def matmuls(r0):
        h = h_ref[r0:r0 + sub, :]
        return (jnp.dot(h, wg_ref[...], preferred_element_type=F32),
                jnp.dot(h, wu_ref[...], preferred_element_type=F32))

    def conv_act(gs, us, before):
        r8 = lax.broadcasted_iota(jnp.int32, (halo, gs.shape[1]), 0)
        g1 = pltpu.roll(gs, 1, axis=0)
        g2 = pltpu.roll(gs, 2, axis=0)
        head1 = jnp.where(r8 == 0, before[halo - 1:halo], g1[:halo])
        head2 = jnp.where(r8 == 0, before[halo - 2:halo - 1], jnp.where(r8 == 1, before[halo - 1:halo], g2[:halo]))
        g1 = jnp.concatenate([head1, g1[halo:]], axis=0)
        g2 = jnp.concatenate([head2, g2[halo:]], axis=0)
        conv = ((bias + g2 * w0) + g1 * w1) + gs * w2
        return (conv * jax.nn.sigmoid(conv)) * us

    if n_slab == 1:
        is_start = (i % tiles_per_seq) == 0

        @pl.when(is_start)
        def _():
            halo_ref[...] = prev_ref[0]

        @pl.when(jnp.logical_not(is_start))
        def _():
            halo_ref[...] = carry_ref[j]

    starts = list(range(0, tm, sub))
    pending = matmuls(starts[0])
    before = halo_ref[...] if n_slab == 1 else None
    for t, r0 in enumerate(starts):
        gate, up = pending
        if t + 1 < len(starts):
            pending = matmuls(starts[t + 1])
        for s0 in range(0, sub, rows):
            gs, us = gate[s0:s0 + rows], up[s0:s0 + rows]
            if n_slab > 1:
                before = prev_ref[(r0 + s0) // rows]
            act_ref[r0 + s0:r0 + s0 + rows, :] = conv_act(gs, us, before).astype(BF16)
            before = gs[rows - halo:rows]
            if n_slab > 1:
                tail_ref[(r0 + s0) // rows] = before
    if n_slab == 1:
        tail_ref[0] = before
        carry_ref[j] = before


def _ffn_up(x, gain, wg, wu, conv_w, conv_b, prev, *, seq):
    m, d_model = x.shape
    d_ff = wg.shape[1]
    tm = _tile(m, FFN_ROW_TILE)
    tf = _tile(d_ff, FFN_COL_TILE)
    if tm >= seq:
        rows, n_slab, tiles_per_seq = seq, tm // seq, 1
        prev_spec = pl.BlockSpec((n_slab, SUBLANES, tf), lambda i, j: (i, 0, j))
    else:
        rows, n_slab, tiles_per_seq = tm, 1, seq // tm
        prev_spec = pl.BlockSpec((1, SUBLANES, tf), lambda i, j: (i // tiles_per_seq, 0, j))
    sub = _tile(tm, FFN_SUB_TILE)
    if n_slab == 1:
        rows = sub
    assert sub % rows == 0
    nj = d_ff // tf
    return pl.pallas_call(
        functools.partial(_ffn_up_kernel, rows=rows, n_slab=n_slab, tiles_per_seq=tiles_per_seq, sub=sub),
        grid=(m // tm, nj),
        in_specs=[pl.BlockSpec((tm, d_model), lambda i, j: (i, 0)),
                  pl.BlockSpec((1, d_model), lambda i, j: (0, 0)),
                  pl.BlockSpec((d_model, tf), lambda i, j: (0, j)),
                  pl.BlockSpec((d_model, tf), lambda i, j: (0, j)),
                  pl.BlockSpec((CONV_W, tf), lambda i, j: (0, j)),
                  pl.BlockSpec((1, tf), lambda i, j: (0, j)),
                  prev_spec],
        out_specs=[pl.BlockSpec((tm, tf), lambda i, j: (i, j)),
                   pl.BlockSpec((n_slab, SUBLANES, tf), lambda i, j: (i, 0, j))],
        out_shape=[jax.ShapeDtypeStruct((m, d_ff), BF16),
                   jax.ShapeDtypeStruct((m // tm * n_slab, SUBLANES, d_ff), F32)],
        scratch_shapes=[pltpu.VMEM((tm, d_model), BF16), pltpu.VMEM((SUBLANES, tf), F32),
                        pltpu.VMEM((nj, SUBLANES, tf), F32)],
        compiler_params=_cparams(2), name="ffn_up",
    )(x, gain, wg, wu, conv_w, conv_b, prev)


def _final_norm_kernel(x_ref, g_ref, o_ref):
    o_ref[...] = _rms(x_ref[...], g_ref[...], NORM_EPS)


def _final_norm(x, gain):
    m, d = x.shape
    tm = _tile(m, ROW_TILE)
    return pl.pallas_call(
        _final_norm_kernel, grid=(m // tm,),
        in_specs=[pl.BlockSpec((tm, d), lambda i: (i, 0)), pl.BlockSpec((1, d), lambda i: (0, 0))],
        out_specs=pl.BlockSpec((tm, d), lambda i: (i, 0)),
        out_shape=jax.ShapeDtypeStruct((m, d), F32),
        compiler_params=_cparams(1), name="final_norm",
    )(x, gain)


def _conv_ffn(y, seq, prev_state, gain, wg, wu, conv_w, conv_b, wd):
    n_seq = y.shape[0] // seq
    d_ff = wg.shape[1]
    prev = jnp.concatenate([jnp.zeros((n_seq, SUBLANES - (CONV_W - 1), d_ff), F32), prev_state], axis=1)
    act, tails = _ffn_up(y, gain, wg, wu, conv_w, conv_b, prev, seq=seq)
    y = _mm_res(act, wd, y, tm_pref=MM_TILE // 2)
    state = tails.reshape(n_seq, -1, SUBLANES, d_ff)[:, -1, SUBLANES - (CONV_W - 1):, :]
    return y, state


def kernel(x_prompt, x_sample, cache_mla_ckv, cache_mla_kpe, cache_swa_k, cache_swa_v, cache_diff_k,
           cache_diff_v, state_ffn_conv, norm_mix, norm_ffn, final_norm, mla_w_dq, mla_q_norm, mla_w_uq,
           mla_w_dkv, mla_kv_norm, mla_w_ukv, mla_w_o, swa_w_qkv, swa_sinks, swa_w_o, diff_w_qkv,
           diff_lambda_q1, diff_lambda_k1, diff_lambda_q2, diff_lambda_k2, diff_subln, diff_w_o,
           ffn_w_gate, ffn_w_up, ffn_conv_w, ffn_conv_b, ffn_w_down):
    bp, sp, d_model = x_prompt.shape
    bs, ss, _ = x_sample.shape
    depth = norm_mix.shape[0]
    past = cache_mla_ckv.shape[2]
    d_ff = ffn_w_gate.shape[2]
    q_rank = mla_w_dq.shape[2]
    kv_rank = mla_w_ukv.shape[1]

    yp = x_prompt.reshape(bp * sp, d_model)
    ys = x_sample.reshape(bs * ss, d_model)
    row_p = _tile(bp * sp, ROW_TILE)
    row_s = _tile(bs * ss, ROW_TILE)

    def tables(head_dim):
        return (_rope_tables(sp, 0, head_dim, row_p), _rope_tables(ss, past, head_dim, row_s))

    tab64, tab128 = tables(64), tables(128)
    outs_p = {k: [] for k in ("ckv", "kpe", "swk", "swv", "dk", "dv", "conv")}
    outs_s = {k: [] for k in ("ckv", "kpe", "swk", "swv", "dk", "dv", "conv")}

    for i in range(depth):
        kind, j = i % N_MIXERS, i // N_MIXERS
        gain = norm_mix[i][None, :]
        if kind == 0:
            q_scale = (MLA_NOPE + MLA_ROPE) ** -0.5 * LOG2E
            pad = (-(q_rank + kv_rank + MLA_ROPE)) % LANES
            w_cat = jnp.concatenate([mla_w_dq[j], mla_w_dkv[j], jnp.zeros((d_model, pad), F32)],
                                    axis=1).astype(BF16)
            w_uq = mla_w_uq[j].reshape(q_rank, MLA_HEADS, MLA_NOPE + MLA_ROPE)
            w_uq = jnp.pad(w_uq, ((0, 0), (0, 0), (0, MLA_QK_PAD - MLA_NOPE - MLA_ROPE)))
            w_uq = w_uq.reshape(q_rank, MLA_HEADS * MLA_QK_PAD).astype(BF16)
            w_ukv = mla_w_ukv[j].reshape(kv_rank, MLA_HEADS, MLA_NOPE + MLA_V)
            wk = w_ukv[:, :, :MLA_NOPE].reshape(kv_rank, MLA_HEADS * MLA_NOPE).astype(BF16)
            wv = w_ukv[:, :, MLA_NOPE:].reshape(kv_rank, MLA_HEADS * MLA_V).astype(BF16)
            w_o = mla_w_o[j].astype(BF16)
            qn, kvn = mla_q_norm[j][None, :], mla_kv_norm[j][None, :]

            (cos_p, sin_p), (cos_s, sin_s) = tab64
            qa, ckv_p, kpe_p = _mla_down(yp, gain, w_cat, qn, kvn, cos_p, sin_p, q_rank, kv_rank)
            q = _mla_q_up(qa, w_uq, cos_p, sin_p, q_scale)
            k, vt = _mla_expand(ckv_p, kpe_p, wk, wv.T, v_transposed=True)
            o = _flash_attention(q, k, vt, batch=bp, seq=sp, n_kv_heads=MLA_HEADS, dk=MLA_QK_PAD,
                                 dv=MLA_V, groups=1, n_maps=1)
            yp = _mm_res(o, w_o, yp)

            qa, ckv_s, kpe_s = _mla_down(ys, gain, w_cat, qn, kvn, cos_s, sin_s, q_rank, kv_rank)
            q = _mla_q_up(qa, w_uq, cos_s, sin_s, q_scale)
            kn, vn = _mla_expand(ckv_s, kpe_s, wk, wv, v_transposed=False)
            kc, vc = _mla_expand(cache_mla_ckv[j].reshape(bs * past, kv_rank),
                                 cache_mla_kpe[j].reshape(bs * past, MLA_ROPE), wk, wv, v_transposed=False)
            o = _decode_attention(q, kc, vc, kn, vn, batch=bs, past=past, new=ss, n_kv_heads=MLA_HEADS,
                                  dk=MLA_QK_PAD, dv=MLA_V, groups=1, n_maps=1)
            ys = _mm_res(o, w_o, ys)
            outs_p["ckv"].append(ckv_p.reshape(bp, sp, kv_rank))
            outs_p["kpe"].append(kpe_p.reshape(bp, sp, MLA_ROPE))
            outs_s["ckv"].append(ckv_s.reshape(bs, ss, kv_rank))
            outs_s["kpe"].append(kpe_s.reshape(bs, ss, MLA_ROPE))
        elif kind == 1:
            qw, kw = SWA_HEADS * SWA_HEAD_DIM, SWA_KV_HEADS * SWA_HEAD_DIM
            w_qkv = swa_w_qkv[j].astype(BF16)
            w_o = swa_w_o[j].astype(BF16)
            sinks = swa_sinks[j]
            (cos_p, sin_p), (cos_s, sin_s) = tab64
            q, k, v, _, _ = _qkv_proj(yp, gain, w_qkv, cos_p, sin_p, qw, kw, kw, SWA_HEAD_DIM)
            o = _swa_attention(q, k, k, v, v, sinks, nq=SWA_Q_TILE, na=WINDOW, nb=SWA_Q_TILE, banded=True,
                               blocks_per_seq=sp // SWA_Q_TILE)
            yp = _mm_res(o, w_o, yp)
            k3 = k.reshape(bp, sp, SWA_KV_HEADS, SWA_HEAD_DIM)
            v3 = v.reshape(bp, sp, SWA_KV_HEADS, SWA_HEAD_DIM)
            outs_p["swk"].append(k3[:, sp - WINDOW:])
            outs_p["swv"].append(v3[:, sp - WINDOW:])

            q, k, v, _, _ = _qkv_proj(ys, gain, w_qkv, cos_s, sin_s, qw, kw, kw, SWA_HEAD_DIM)
            kc = cache_swa_k[j].reshape(bs * WINDOW, kw)
            vc = cache_swa_v[j].reshape(bs * WINDOW, kw)
            o = _swa_attention(q, kc, k, vc, v, sinks, nq=ss, na=WINDOW, nb=ss, banded=False,
                               blocks_per_seq=1)
            ys = _mm_res(o, w_o, ys)
            k_all = jnp.concatenate([cache_swa_k[j], k.reshape(bs, ss, SWA_KV_HEADS, SWA_HEAD_DIM)], axis=1)
            v_all = jnp.concatenate([cache_swa_v[j], v.reshape(bs, ss, SWA_KV_HEADS, SWA_HEAD_DIM)], axis=1)
            outs_s["swk"].append(k_all[:, ss:])
            outs_s["swv"].append(v_all[:, ss:])
        else:
            lam_init = 0.8 - 0.6 * math.exp(-0.3 * i)
            q_scale = DIFF_HEAD_DIM ** -0.5 * LOG2E
            groups = DIFF_HEADS // DIFF_KV_HEADS
            qw = DIFF_HEADS * 2 * DIFF_HEAD_DIM
            kw = DIFF_KV_HEADS * 2 * DIFF_HEAD_DIM
            w_qkv = diff_w_qkv[j].astype(BF16)
            w_o = diff_w_o[j].astype(BF16)
            lam = jnp.stack([diff_lambda_q1[j], diff_lambda_k1[j], diff_lambda_q2[j], diff_lambda_k2[j]])
            subln = diff_subln[j][None, :]
            common = dict(n_kv_heads=DIFF_KV_HEADS, dk=DIFF_HEAD_DIM, dv=2 * DIFF_HEAD_DIM, groups=groups,
                          n_maps=2, lam=lam, subln=subln, lam_init=lam_init)
            (cos_p, sin_p), (cos_s, sin_s) = tab128
            q, k, v, kb, vt = _qkv_proj(yp, gain, w_qkv, cos_p, sin_p, qw, kw, kw, DIFF_HEAD_DIM,
                                        q_scale=q_scale, v_head_dim=2 * DIFF_HEAD_DIM)
            o = _flash_attention(q, kb, vt, batch=bp, seq=sp, **common)
            yp = _mm_res(o, w_o, yp)
            outs_p["dk"].append(k.reshape(bp, sp, DIFF_KV_HEADS, 2, DIFF_HEAD_DIM))
            outs_p["dv"].append(v.reshape(bp, sp, DIFF_KV_HEADS, 2 * DIFF_HEAD_DIM))

            q, k, v, _, _ = _qkv_proj(ys, gain, w_qkv, cos_s, sin_s, qw, kw, kw, DIFF_HEAD_DIM,
                                      q_scale=q_scale)
            o = _decode_attention(q, cache_diff_k[j].reshape(bs * past, kw),
                                  cache_diff_v[j].reshape(bs * past, kw), k, v,
                                  batch=bs, past=past, new=ss, **common)
            ys = _mm_res(o, w_o, ys)
            outs_s["dk"].append(k.reshape(bs, ss, DIFF_KV_HEADS, 2, DIFF_HEAD_DIM))
            outs_s["dv"].append(v.reshape(bs, ss, DIFF_KV_HEADS, 2 * DIFF_HEAD_DIM))

        ffn_gain = norm_ffn[i][None, :]
        wg, wu, wd = ffn_w_gate[i].astype(BF16), ffn_w_up[i].astype(BF16), ffn_w_down[i].astype(BF16)
        cb = ffn_conv_b[i][None, :]
        yp, conv_p = _conv_ffn(yp, sp, jnp.zeros((bp, CONV_W - 1, d_ff), F32), ffn_gain, wg, wu,
                               ffn_conv_w[i], cb, wd)
        ys, conv_s = _conv_ffn(ys, ss, state_ffn_conv[i], ffn_gain, wg, wu, ffn_conv_w[i], cb, wd)
        outs_p["conv"].append(conv_p)
        outs_s["conv"].append(conv_s)

    fg = final_norm[None, :]
    y_prompt = _final_norm(yp, fg).reshape(bp, sp, d_model)
    y_sample = _final_norm(ys, fg).reshape(bs, ss, d_model)
    order = ("ckv", "kpe", "swk", "swv", "dk", "dv", "conv")
    return (y_prompt, y_sample) + tuple(jnp.stack(outs_p[k]) for k in order) + tuple(
        jnp.stack(outs_s[k]) for k in order)
```

```python
import functools
import math

import jax
import jax.numpy as jnp
from jax import lax
from jax.experimental import pallas as pl
from jax.experimental.pallas import tpu as pltpu

F32 = jnp.float32
BF16 = jnp.bfloat16

CHUNK = 64
ROPE_THETA = 10000.0
NORM_EPS = 1e-6
SUBLN_EPS = 1e-5
N_MIXERS = 3
MLA_HEADS = 16
MLA_NOPE = 128
MLA_ROPE = 64
MLA_V = 128
SWA_HEADS = 32
SWA_KV_HEADS = 4
SWA_HEAD_DIM = 64
WINDOW = 128
DIFF_HEADS = 8
DIFF_KV_HEADS = 4
DIFF_HEAD_DIM = 128
CONV_W = 3

LANES = 128
SUBLANES = 8
MXU_COLS = 256
MLA_QK_PAD = MXU_COLS
VMEM_LIMIT_BYTES = 56 * 1024 * 1024
LOG2E = 1.4426950408889634

ROW_TILE = 512
FFN_ROW_TILE = 1024
FFN_COL_TILE = 512
FFN_SUB_TILE = 256
ATTN_Q_TILE = 2048
ATTN_KV_TILE = 512
ATTN_CHAIN = 256
ATTN_LOOKAHEAD = 3
ONES_ROWS = 16
MM_TILE = 512
MM_WEIGHT_BLOCK_BYTES = 12 * 1024 * 1024
MM_SUB_TILE = 256


def _cparams(n_axes):
    return pltpu.CompilerParams(dimension_semantics=("arbitrary",) * n_axes,
                                vmem_limit_bytes=VMEM_LIMIT_BYTES)


def _resident(shape):
    return pl.BlockSpec(shape, lambda *_: (0,) * len(shape), pipeline_mode=pl.Buffered(1))


def _tile(n, pref):
    t = min(n, pref)
    assert n % t == 0, (n, pref)
    return t


def _rms(x, g, eps):
    ms = jnp.mean(x * x, axis=-1, keepdims=True)
    return (x * lax.rsqrt(ms + eps)) * g


def _rope_slab(x, cos, sin, head_dim):
    if head_dim == LANES:
        swapped = pltpu.roll(x, LANES // 2, axis=1)
    else:
        half = head_dim // 2
        lane = lax.broadcasted_iota(jnp.int32, x.shape, 1)
        first = (lane & (head_dim - 1)) < half
        swapped = jnp.where(first, pltpu.roll(x, LANES - half, axis=1), pltpu.roll(x, half, axis=1))
    return x * cos + swapped * sin


def _rope_tables(seq_len, offset, head_dim, rows):
    half = head_dim // 2
    pos = jnp.arange(seq_len, dtype=F32) + offset
    inv = ROPE_THETA ** (-jnp.arange(0, head_dim, 2, dtype=F32) / head_dim)
    ang = pos[:, None] * inv[None, :]
    cos, sin = jnp.cos(ang), jnp.sin(ang)
    reps = LANES // head_dim
    cos_l = jnp.tile(jnp.concatenate([cos, cos], axis=1), (1, reps))
    sin_l = jnp.tile(jnp.concatenate([-sin, sin], axis=1), (1, reps))
    n = max(rows // seq_len, 1)
    return jnp.tile(cos_l, (n, 1)), jnp.tile(sin_l, (n, 1))


def _mla_down_kernel(x_ref, g_ref, w_ref, qn_ref, kvn_ref, cos_ref, sin_ref,
                     qa_ref, ckv_ref, kpe_ref, *, q_rank, kv_rank):
    h = _rms(x_ref[...], g_ref[...], NORM_EPS).astype(BF16)
    d = jnp.dot(h, w_ref[...], preferred_element_type=F32)
    qa_ref[...] = _rms(d[:, :q_rank], qn_ref[...], NORM_EPS).astype(BF16)
    ckv_ref[...] = _rms(d[:, q_rank:q_rank + kv_rank], kvn_ref[...], NORM_EPS)
    slab = d[:, q_rank + kv_rank:q_rank + kv_rank + LANES]
    kpe_ref[...] = _rope_slab(slab, cos_ref[...], sin_ref[...], MLA_ROPE)[:, :MLA_ROPE]


def _mla_down(x, gain, w_cat, q_norm, kv_norm, cos, sin, q_rank, kv_rank):
    m, d_model = x.shape
    tm = _tile(m, ROW_TILE)
    n_tab = cos.shape[0] // tm
    wn = w_cat.shape[1]
    row = lambda i: (i, 0)
    fixed = lambda i: (0, 0)
    tab = lambda i: (i % n_tab, 0)
    return pl.pallas_call(
        functools.partial(_mla_down_kernel, q_rank=q_rank, kv_rank=kv_rank),
        grid=(m // tm,),
        in_specs=[pl.BlockSpec((tm, d_model), row), pl.BlockSpec((1, d_model), fixed),
                  _resident((d_model, wn)), pl.BlockSpec((1, q_rank), fixed),
                  pl.BlockSpec((1, kv_rank), fixed), pl.BlockSpec((tm, LANES), tab),
                  pl.BlockSpec((tm, LANES), tab)],
        out_specs=[pl.BlockSpec((tm, q_rank), row), pl.BlockSpec((tm, kv_rank), row),
                   pl.BlockSpec((tm, MLA_ROPE), row)],
        out_shape=[jax.ShapeDtypeStruct((m, q_rank), BF16), jax.ShapeDtypeStruct((m, kv_rank), F32),
                   jax.ShapeDtypeStruct((m, MLA_ROPE), F32)],
        compiler_params=_cparams(1), name="mla_down",
    )(x, gain, w_cat, q_norm, kv_norm, cos, sin)


def _mla_q_up_kernel(qa_ref, w_ref, cos_ref, sin_ref, q_ref, *, q_scale):
    qa = qa_ref[...]
    cos, sin = cos_ref[...], sin_ref[...]
    for h in range(MLA_HEADS):
        c0 = h * MLA_QK_PAD
        d = jnp.dot(qa, w_ref[:, c0:c0 + MLA_QK_PAD], preferred_element_type=F32)
        q_ref[:, c0:c0 + LANES] = (d[:, :LANES] * q_scale).astype(BF16)
        q_ref[:, c0 + LANES:c0 + MLA_QK_PAD] = (_rope_slab(d[:, LANES:], cos, sin, MLA_ROPE) * q_scale).astype(BF16)


def _mla_q_up(qa, w_pad, cos, sin, q_scale):
    m, q_rank = qa.shape
    tm = _tile(m, ROW_TILE)
    n_tab = cos.shape[0] // tm
    n = w_pad.shape[1]
    row = lambda i: (i, 0)
    fixed = lambda i: (0, 0)
    tab = lambda i: (i % n_tab, 0)
    return pl.pallas_call(
        functools.partial(_mla_q_up_kernel, q_scale=q_scale), grid=(m // tm,),
        in_specs=[pl.BlockSpec((tm, q_rank), row), _resident((q_rank, n)),
                  pl.BlockSpec((tm, LANES), tab), pl.BlockSpec((tm, LANES), tab)],
        out_specs=pl.BlockSpec((tm, n), row),
        out_shape=jax.ShapeDtypeStruct((m, n), BF16),
        compiler_params=_cparams(1), name="mla_q_up",
    )(qa, w_pad, cos, sin)


def _mla_expand_kernel(ckv_ref, kpe_ref, wk_ref, wv_ref, k_ref, v_ref, *, v_transposed):
    c = ckv_ref[...].astype(BF16)
    kpe = kpe_ref[...].astype(BF16)
    if v_transposed:
        vt = lax.dot_general(wv_ref[...], c, (((1,), (1,)), ((), ())), preferred_element_type=F32)
        dva = MLA_V + ONES_ROWS
        for h in range(MLA_HEADS):
            v_ref[0, h * dva:h * dva + MLA_V, :] = vt[h * MLA_V:(h + 1) * MLA_V].astype(BF16)
            v_ref[0, h * dva + MLA_V:(h + 1) * dva, :] = jnp.ones((ONES_ROWS, vt.shape[1]), BF16)
    else:
        v_ref[...] = jnp.dot(c, wv_ref[...], preferred_element_type=F32).astype(BF16)
    kn = jnp.dot(c, wk_ref[...], preferred_element_type=F32).astype(BF16)
    zeros = jnp.zeros((kpe.shape[0], MLA_QK_PAD - MLA_NOPE - MLA_ROPE), BF16)
    for h in range(MLA_HEADS):
        c0 = h * MLA_QK_PAD
        k_ref[:, c0:c0 + MLA_NOPE] = kn[:, h * MLA_NOPE:(h + 1) * MLA_NOPE]
        k_ref[:, c0 + MLA_NOPE:c0 + MLA_NOPE + MLA_ROPE] = kpe
        k_ref[:, c0 + MLA_NOPE + MLA_ROPE:c0 + MLA_QK_PAD] = zeros


def _mla_expand(ckv, kpe, wk, wv, *, v_transposed):
    m, kv_rank = ckv.shape
    tm = _tile(m, ATTN_KV_TILE if v_transposed else ROW_TILE)
    row = lambda i: (i, 0)
    nk, nv = MLA_HEADS * MLA_QK_PAD, MLA_HEADS * MLA_V
    if v_transposed:
        nva = MLA_HEADS * (MLA_V + ONES_ROWS)
        v_spec = pl.BlockSpec((1, nva, tm), lambda i: (i, 0, 0))
        v_shape = jax.ShapeDtypeStruct((m // tm, nva, tm), BF16)
    else:
        v_spec = pl.BlockSpec((tm, nv), row)
        v_shape = jax.ShapeDtypeStruct((m, nv), BF16)
    return pl.pallas_call(
        functools.partial(_mla_expand_kernel, v_transposed=v_transposed), grid=(m // tm,),
        in_specs=[pl.BlockSpec((tm, kv_rank), row), pl.BlockSpec((tm, MLA_ROPE), row),
                  _resident(wk.shape), _resident(wv.shape)],
        out_specs=[pl.BlockSpec((tm, nk), row), v_spec],
        out_shape=[jax.ShapeDtypeStruct((m, nk), BF16), v_shape],
        compiler_params=_cparams(1), name="mla_expand",
    )(ckv, kpe, wk, wv)


def _qkv_proj_kernel(x_ref, g_ref, w_ref, cos_ref, sin_ref, q_ref, k_ref, v_ref, kb_ref, vb_ref,
                     *, qw, kw, vw, head_dim, q_scale, vt_head):
    h = _rms(x_ref[...], g_ref[...], NORM_EPS).astype(BF16)
    cos, sin = cos_ref[...], sin_ref[...]
    vt_dst = None
    if vt_head:
        vt_dst = vb_ref.at[0] if len(vb_ref.shape) == 3 else vb_ref
    for n0 in range(0, qw + kw + vw, MXU_COLS):
        dd = jnp.dot(h, w_ref[:, n0:n0 + MXU_COLS], preferred_element_type=F32)
        for c0 in range(n0, n0 + MXU_COLS, LANES):
            d = dd[:, c0 - n0:c0 - n0 + LANES]
            if c0 < qw:
                q_ref[:, c0:c0 + LANES] = (_rope_slab(d, cos, sin, head_dim) * q_scale).astype(BF16)
            elif c0 < qw + kw:
                r = _rope_slab(d, cos, sin, head_dim)
                k_ref[:, c0 - qw:c0 - qw + LANES] = r
                kb_ref[:, c0 - qw:c0 - qw + LANES] = r.astype(BF16)
            else:
                c1 = c0 - qw - kw
                v_ref[:, c1:c1 + LANES] = d
                if vt_head:
                    vd, ones_rows = vt_head
                    dt = d.T.astype(BF16)
                    for hv in range(c1 // vd, (c1 + LANES - 1) // vd + 1):
                        lo, hi = max(c1, hv * vd), min(c1 + LANES, (hv + 1) * vd)
                        r0 = hv * (vd + ones_rows) + lo - hv * vd
                        vt_dst[r0:r0 + hi - lo, :] = dt[lo - c1:hi - c1]
                        if hi == (hv + 1) * vd:
                            vt_dst[r0 + hi - lo:r0 + hi - lo + ones_rows, :] = jnp.ones((ones_rows, d.shape[0]), BF16)
                else:
                    vb_ref[:, c1:c1 + LANES] = d.astype(BF16)


def _qkv_proj(x, gain, w, cos, sin, qw, kw, vw, head_dim, *, q_scale=1.0, vt_head=None, vt_blocked=True):
    m, d_model = x.shape
    tm = _tile(m, ATTN_KV_TILE if vt_head else ROW_TILE)
    n_tab = cos.shape[0] // tm
    row = lambda i: (i, 0)
    fixed = lambda i: (0, 0)
    tab = lambda i: (i % n_tab, 0)
    if vt_head:
        vwa = vw // vt_head[0] * (vt_head[0] + vt_head[1])
        if vt_blocked:
            vb_spec = pl.BlockSpec((1, vwa, tm), lambda i: (i, 0, 0))
            vb_shape = jax.ShapeDtypeStruct((m // tm, vwa, tm), BF16)
        else:
            vb_spec = pl.BlockSpec((vwa, tm), lambda i: (0, i))
            vb_shape = jax.ShapeDtypeStruct((vwa, m), BF16)
    else:
        vb_spec = pl.BlockSpec((tm, vw), row)
        vb_shape = jax.ShapeDtypeStruct((m, vw), BF16)
    return pl.pallas_call(
        functools.partial(_qkv_proj_kernel, qw=qw, kw=kw, vw=vw, head_dim=head_dim, q_scale=q_scale,
                          vt_head=vt_head),
        grid=(m // tm,),
        in_specs=[pl.BlockSpec((tm, d_model), row), pl.BlockSpec((1, d_model), fixed),
                  _resident(w.shape), pl.BlockSpec((tm, LANES), tab),
                  pl.BlockSpec((tm, LANES), tab)],
        out_specs=[pl.BlockSpec((tm, qw), row), pl.BlockSpec((tm, kw), row), pl.BlockSpec((tm, vw), row),
                   pl.BlockSpec((tm, kw), row), vb_spec],
        out_shape=[jax.ShapeDtypeStruct((m, qw), BF16), jax.ShapeDtypeStruct((m, kw), F32),
                   jax.ShapeDtypeStruct((m, vw), F32), jax.ShapeDtypeStruct((m, kw), BF16), vb_shape],
        compiler_params=_cparams(1), name="qkv_proj",
    )(x, gain, w, cos, sin)


def _mm_res_kernel(a_ref, w_ref, r_ref, o_ref, *, sub):
    starts = list(range(0, a_ref.shape[0], sub))
    pending = jnp.dot(a_ref[0:sub, :], w_ref[...], preferred_element_type=F32)
    for t, r0 in enumerate(starts):
        d = pending
        if t + 1 < len(starts):
            r1 = starts[t + 1]
            pending = jnp.dot(a_ref[r1:r1 + sub, :], w_ref[...], preferred_element_type=F32)
        o_ref[r0:r0 + sub, :] = r_ref[r0:r0 + sub, :] + d


def _mm_res(a, w, res):
    m, k = a.shape
    n = w.shape[1]
    tm = _tile(m, MM_TILE)
    tn = n
    while tn * k * w.dtype.itemsize > MM_WEIGHT_BLOCK_BYTES and tn % (2 * MXU_COLS) == 0:
        tn //= 2
    return pl.pallas_call(
        functools.partial(_mm_res_kernel, sub=_tile(tm, MM_SUB_TILE)), grid=(n // tn, m // tm),
        in_specs=[pl.BlockSpec((tm, k), lambda j, i: (i, 0)), pl.BlockSpec((k, tn), lambda j, i: (0, j)),
                  pl.BlockSpec((tm, tn), lambda j, i: (i, j))],
        out_specs=pl.BlockSpec((tm, tn), lambda j, i: (i, j)),
        out_shape=jax.ShapeDtypeStruct((m, n), F32),
        compiler_params=_cparams(2), name="mm_res",
    )(a, w, res)


def _diff_lambda(lam_ref, lam_init):
    lam = lam_ref[...]
    a = jnp.sum(lam[0:1] * lam[1:2], axis=-1, keepdims=True)
    b = jnp.sum(lam[2:3] * lam[3:4], axis=-1, keepdims=True)
    return jnp.exp(a) - jnp.exp(b) + lam_init


def _combine_heads(o_list, o_ref, rows, lam_ref, subln_ref, *, n_maps, dv, lam_init):
    groups = len(o_list) // n_maps
    for g in range(groups):
        if n_maps == 1:
            o = o_list[g]
        else:
            lam = _diff_lambda(lam_ref, lam_init)
            o = o_list[2 * g] - lam * o_list[2 * g + 1]
            o = _rms(o, subln_ref[...], SUBLN_EPS) * (1.0 - lam_init)
        o_ref[rows, g * dv:(g + 1) * dv] = o.astype(o_ref.dtype)


def _flash_kernel(*refs, tq, tk, cw, dk, dv, groups, n_maps, lam_init):
    if n_maps == 2:
        lam_ref, subln_ref, q_ref, k_ref, vt_ref, o_ref, m_ref, acc_ref, st_ref = refs
    else:
        q_ref, k_ref, vt_ref, o_ref, m_ref, acc_ref, st_ref = refs
        lam_ref = subln_ref = None
    n_sub = groups * n_maps
    n_chain = tq // cw
    kv_per_q = tq // tk
    qi = pl.program_id(2)
    nt = (((1,), (1,)), ((), ()))

    m_ref[...] = jnp.full(m_ref.shape, -jnp.inf, F32)
    acc_ref[...] = jnp.zeros(acc_ref.shape, F32)

    def scores(item, k_blks, slot):
        u, n, k_lo, bi = item
        mi = u % n_maps
        q_n = q_ref[n * cw:(n + 1) * cw, u * dk:(u + 1) * dk]
        st = lax.dot_general(k_blks[bi][:, mi * dk:(mi + 1) * dk], q_n, nt, preferred_element_type=F32)
        if k_lo is not None:
            kc = (lax.broadcasted_iota(jnp.int32, st.shape, 0) + k_lo) // CHUNK
            qc = (lax.broadcasted_iota(jnp.int32, st.shape, 1) + n * cw) // CHUNK
            st = jnp.where(kc <= qc, st, -jnp.inf)
        st_ref[slot] = st

    def update(item, slot, vt_blks):
        u, n, _, bi = item
        st = st_ref[slot]
        idx = u * n_chain + n
        m_prev = m_ref[idx]
        m_new = jnp.maximum(m_prev, jnp.max(st, axis=0, keepdims=True))
        alpha = jnp.exp2(m_prev - m_new)
        pt = jnp.exp2(st - m_new).astype(BF16)
        acc_ref[idx] = acc_ref[idx] * alpha + jnp.dot(vt_blks[bi], pt, preferred_element_type=F32)
        m_ref[idx] = m_new

    def run_chains(items, first_block):
        blocks = sorted({it[3] for it in items})
        k_blks = {bi: k_ref[pl.ds(pl.multiple_of((first_block + bi) * tk, tk), tk), :] for bi in blocks}
        vt_blks = {bi: vt_ref[first_block + bi] for bi in blocks}
        n_slots = ATTN_LOOKAHEAD + 1
        for t in range(min(ATTN_LOOKAHEAD, len(items))):
            scores(items[t], k_blks, t % n_slots)
        for t, item in enumerate(items):
            if t + ATTN_LOOKAHEAD < len(items):
                scores(items[t + ATTN_LOOKAHEAD], k_blks, (t + ATTN_LOOKAHEAD) % n_slots)
            update(item, t % n_slots, vt_blks)

    def body(j, carry):
        items = [(u, n, None, bi) for bi in range(kv_per_q) for n in range(n_chain) for u in range(n_sub)]
        run_chains(items, j * kv_per_q)
        return carry

    lax.fori_loop(0, qi, body, 0)

    items = [(u, n, bi * tk if (bi + 1) * tk > n * cw else None, bi)
             for bi in range(kv_per_q) for n in range(n_chain) if bi * tk < (n + 1) * cw
             for u in range(n_sub)]
    run_chains(items, qi * kv_per_q)

    for n in range(n_chain):
        outs = []
        for u in range(n_sub):
            a = acc_ref[u * n_chain + n]
            outs.append((a[:dv] / a[dv:dv + 1]).T)
        _combine_heads(outs, o_ref, slice(n * cw, (n + 1) * cw), lam_ref, subln_ref,
                       n_maps=n_maps, dv=dv, lam_init=lam_init)


def _flash_attention(q, k, vt, *, batch, seq, n_kv_heads, dk, dv, groups, n_maps,
                     lam=None, subln=None, lam_init=0.0):
    tk = vt.shape[2]
    dva = dv + ONES_ROWS
    n_sub = groups * n_maps
    tq = _tile(seq, max(ATTN_Q_TILE if n_sub == 1 else ATTN_Q_TILE // 2, tk))
    cw = _tile(tq, ATTN_CHAIN)
    assert tq % tk == 0 and tk % CHUNK == 0 and cw % CHUNK == 0
    nq, nk = seq // tq, seq // tk
    n_chain = tq // cw
    kern = functools.partial(_flash_kernel, tq=tq, tk=tk, cw=cw, dk=dk, dv=dv, groups=groups,
                             n_maps=n_maps, lam_init=lam_init)
    in_specs = [pl.BlockSpec((tq, n_sub * dk), lambda b, h, i: (b * nq + i, h)),
                pl.BlockSpec((seq, n_maps * dk), lambda b, h, i: (b, h)),
                pl.BlockSpec((nk, dva, tk), lambda b, h, i: (b, h, 0))]
    args = [q, k, vt]
    if n_maps == 2:
        in_specs = [pl.BlockSpec(lam.shape, lambda b, h, i: (0, 0)),
                    pl.BlockSpec(subln.shape, lambda b, h, i: (0, 0))] + in_specs
        args = [lam, subln] + args
    return pl.pallas_call(
        kern, grid=(batch, n_kv_heads, nq), in_specs=in_specs,
        out_specs=pl.BlockSpec((tq, groups * dv), lambda b, h, i: (b * nq + i, h)),
        out_shape=jax.ShapeDtypeStruct((batch * seq, n_kv_heads * groups * dv), BF16),
        scratch_shapes=[pltpu.VMEM((n_sub * n_chain, 1, cw), F32),
                        pltpu.VMEM((n_sub * n_chain, dva, cw), F32),
                        pltpu.VMEM((ATTN_LOOKAHEAD + 1, tk, cw), F32)],
        compiler_params=_cparams(3), name="flash_attention",
    )(*args)


def _decode_kernel(*refs, dk, dv, groups, n_maps, lam_init):
    if n_maps == 2:
        lam_ref, subln_ref, q_ref, kc_ref, vc_ref, kn_ref, vn_ref, o_ref = refs
    else:
        q_ref, kc_ref, vc_ref, kn_ref, vn_ref, o_ref = refs
        lam_ref = subln_ref = None
    nt = (((1,), (1,)), ((), ()))
    vc = vc_ref[...].astype(BF16)
    vn = vn_ref[...].astype(BF16)
    outs = []
    for u in range(groups * n_maps):
        mi = u % n_maps
        q = q_ref[:, u * dk:(u + 1) * dk]
        kc = kc_ref[:, mi * dk:(mi + 1) * dk].astype(BF16)
        kn = kn_ref[:, mi * dk:(mi + 1) * dk].astype(BF16)
        s1 = lax.dot_general(q, kc, nt, preferred_element_type=F32)
        s2 = lax.dot_general(q, kn, nt, preferred_element_type=F32)
        m = jnp.maximum(jnp.max(s1, axis=1, keepdims=True), jnp.max(s2, axis=1, keepdims=True))
        p1 = jnp.exp2(s1 - m)
        p2 = jnp.exp2(s2 - m)
        l = jnp.sum(p1, axis=1, keepdims=True) + jnp.sum(p2, axis=1, keepdims=True)
        o = (jnp.dot(p1.astype(BF16), vc, preferred_element_type=F32)
             + jnp.dot(p2.astype(BF16), vn, preferred_element_type=F32))
        outs.append(o / l)
    _combine_heads(outs, o_ref, slice(None), lam_ref, subln_ref, n_maps=n_maps, dv=dv, lam_init=lam_init)


def _decode_attention(q, kc, vc, kn, vn, *, batch, past, new, n_kv_heads, dk, dv, groups, n_maps,
                      lam=None, subln=None, lam_init=0.0):
    n_sub = groups * n_maps
    kern = functools.partial(_decode_kernel, dk=dk, dv=dv, groups=groups, n_maps=n_maps, lam_init=lam_init)
    bh = lambda b, h: (b, h)
    in_specs = [pl.BlockSpec((new, n_sub * dk), bh), pl.BlockSpec((past, n_maps * dk), bh),
                pl.BlockSpec((past, dv), bh), pl.BlockSpec((new, n_maps * dk), bh),
                pl.BlockSpec((new, dv), bh)]
    args = [q, kc, vc, kn, vn]
    if n_maps == 2:
        in_specs = [pl.BlockSpec(lam.shape, lambda b, h: (0, 0)),
                    pl.BlockSpec(subln.shape, lambda b, h: (0, 0))] + in_specs
        args = [lam, subln] + args
    return pl.pallas_call(
        kern, grid=(batch, n_kv_heads), in_specs=in_specs,
        out_specs=pl.BlockSpec((new, groups * dv), bh),
        out_shape=jax.ShapeDtypeStruct((batch * new, n_kv_heads * groups * dv), BF16),
        compiler_params=_cparams(2), name="decode_attention",
    )(*args)


def _swa_decode_kernel(sinks_ref, q_ref, ka_ref, kb_ref, va_ref, vb_ref, o_ref):
    nq = q_ref.shape[0]
    group = SWA_HEADS // SWA_KV_HEADS
    d = SWA_HEAD_DIM
    nt = (((1,), (1,)), ((), ()))
    for h in range(SWA_KV_HEADS):
        hs = slice(h * d, (h + 1) * d)
        k = jnp.concatenate([ka_ref[:, hs], kb_ref[:, hs]], axis=0).astype(BF16)
        v = jnp.concatenate([va_ref[:, hs], vb_ref[:, hs]], axis=0).astype(BF16)
        q = jnp.concatenate([q_ref[:, (h * group + g) * d:(h * group + g + 1) * d] for g in range(group)],
                            axis=0)
        sink = jnp.concatenate([jnp.full((nq, 1), sinks_ref[h * group + g] * LOG2E, F32)
                                for g in range(group)], axis=0)
        s = lax.dot_general(q, k, nt, preferred_element_type=F32)
        m = jnp.maximum(jnp.max(s, axis=1, keepdims=True), sink)
        p = jnp.exp2(s - m)
        l = jnp.sum(p, axis=1, keepdims=True) + jnp.exp2(sink - m)
        o = jnp.dot(p.astype(BF16), v, preferred_element_type=F32) / l
        for g in range(0, group, 2):
            pair = jnp.concatenate([o[g * nq:(g + 1) * nq], o[(g + 1) * nq:(g + 2) * nq]], axis=1)
            c0 = (h * group + g) * d
            o_ref[:, c0:c0 + 2 * d] = pair.astype(o_ref.dtype)


def _swa_decode_attention(q, ka, kb, va, vb, sinks, *, nq, na, nb):
    m, qw = q.shape
    kvw = ka.shape[1]
    cur = lambda t: (t, 0)
    return pl.pallas_call(
        _swa_decode_kernel, grid=(m // nq,),
        in_specs=[pl.BlockSpec(memory_space=pltpu.SMEM), pl.BlockSpec((nq, qw), cur),
                  pl.BlockSpec((na, kvw), cur), pl.BlockSpec((nb, kvw), cur),
                  pl.BlockSpec((na, kvw), cur), pl.BlockSpec((nb, kvw), cur)],
        out_specs=pl.BlockSpec((nq, qw), cur),
        out_shape=jax.ShapeDtypeStruct((m, qw), BF16),
        compiler_params=_cparams(1), name="swa_decode_attention",
    )(sinks, q, ka, kb, va, vb)


def _swa_kernel(sinks_ref, q_ref, ka_ref, kb_ref, vta_ref, vtb_ref, o_ref, st_ref, *, blocks_per_seq):
    nq, na = q_ref.shape[0], ka_ref.shape[0]
    nk = na + kb_ref.shape[0]
    group = SWA_HEADS // SWA_KV_HEADS
    d = SWA_HEAD_DIM
    vrows = vta_ref.shape[0] // SWA_KV_HEADS
    nt = (((1,), (1,)), ((), ()))
    kc = lax.broadcasted_iota(jnp.int32, (nk, group * nq), 0) // CHUNK
    qc = (lax.broadcasted_iota(jnp.int32, (nk, group * nq), 1) % nq) // CHUNK
    first = (pl.program_id(0) % blocks_per_seq) == 0
    lo = jnp.where(first, na // CHUNK, 0)
    valid = (kc >= qc) & (kc <= qc + na // CHUNK) & (kc >= lo)

    def scores(h):
        hs = slice(h * d, (h + 1) * d)
        k = jnp.concatenate([ka_ref[:, hs], kb_ref[:, hs]], axis=0)
        q = jnp.concatenate([q_ref[:, (h * group + g) * d:(h * group + g + 1) * d] for g in range(group)],
                            axis=0)
        st = lax.dot_general(k, q, nt, preferred_element_type=F32)
        st_ref[h % 2] = jnp.where(valid, st, -jnp.inf)

    scores(0)
    for h in range(SWA_KV_HEADS):
        if h + 1 < SWA_KV_HEADS:
            scores(h + 1)
        st = st_ref[h % 2]
        sink = jnp.concatenate([jnp.full((1, nq), sinks_ref[h * group + g] * LOG2E, F32)
                                for g in range(group)], axis=1)
        m = jnp.maximum(jnp.max(st, axis=0, keepdims=True), sink)
        pt = jnp.exp2(st - m).astype(BF16)
        vt = jnp.concatenate([vta_ref[h * vrows:(h + 1) * vrows, :], vtb_ref[h * vrows:(h + 1) * vrows, :]],
                             axis=1)
        ot = jnp.dot(vt, pt, preferred_element_type=F32)
        l = ot[d:d + 1] + jnp.exp2(sink - m)
        o = ot[:d] / l
        for g in range(0, group, 2):
            pair = jnp.concatenate([o[:, g * nq:(g + 1) * nq], o[:, (g + 1) * nq:(g + 2) * nq]], axis=0)
            c0 = (h * group + g) * d
            o_ref[:, c0:c0 + 2 * d] = pair.T.astype(o_ref.dtype)


def _swa_attention(q, k, vt, sinks, *, seq):
    m, qw = q.shape
    kvw = k.shape[1]
    nq = WINDOW
    cur = lambda t: (t, 0)
    prev = lambda t: (jnp.maximum(t - 1, 0), 0)
    cur_t = lambda t: (0, t)
    prev_t = lambda t: (0, jnp.maximum(t - 1, 0))
    return pl.pallas_call(
        functools.partial(_swa_kernel, blocks_per_seq=seq // nq),
        grid=(m // nq,),
        in_specs=[pl.BlockSpec(memory_space=pltpu.SMEM), pl.BlockSpec((nq, qw), cur),
                  pl.BlockSpec((nq, kvw), prev), pl.BlockSpec((nq, kvw), cur),
                  pl.BlockSpec((vt.shape[0], nq), prev_t), pl.BlockSpec((vt.shape[0], nq), cur_t)],
        out_specs=pl.BlockSpec((nq, qw), cur),
        out_shape=jax.ShapeDtypeStruct((m, qw), BF16),
        scratch_shapes=[pltpu.VMEM((2, 2 * nq, SWA_HEADS // SWA_KV_HEADS * nq), F32)],
        compiler_params=_cparams(1), name="swa_attention",
    )(sinks, q, k, k, vt, vt)


def _ffn_up_kernel(x_ref, g_ref, wg_ref, wu_ref, cw_ref, cb_ref, prev_ref, act_ref, tail_ref,
                   h_ref, halo_ref, carry_ref, *, rows, n_slab, tiles_per_seq, sub):
    i, j = pl.program_id(0), pl.program_id(1)
    halo = SUBLANES

    @pl.when(j == 0)
    def _():
        h_ref[...] = _rms(x_ref[...], g_ref[...], NORM_EPS).astype(BF16)

    w0, w1, w2 = cw_ref[0:1, :], cw_ref[1:2, :], cw_ref[2:3, :]
    bias = cb_ref[...]
    tm = h_ref.shape[0]

    def matmuls(r0):
        h = h_ref[r0:r0 + sub, :]
        return (jnp.dot(h, wg_ref[...], preferred_element_type=F32),
                jnp.dot(h, wu_ref[...], preferred_element_type=F32))

    def conv_act(gs, us, before):
        r8 = lax.broadcasted_iota(jnp.int32, (halo, gs.shape[1]), 0)
        g1 = pltpu.roll(gs, 1, axis=0)
        g2 = pltpu.roll(gs, 2, axis=0)
        head1 = jnp.where(r8 == 0, before[halo - 1:halo], g1[:halo])
        head2 = jnp.where(r8 == 0, before[halo - 2:halo - 1], jnp.where(r8 == 1, before[halo - 1:halo], g2[:halo]))
        g1 = jnp.concatenate([head1, g1[halo:]], axis=0)
        g2 = jnp.concatenate([head2, g2[halo:]], axis=0)
        conv = ((bias + g2 * w0) + g1 * w1) + gs * w2
        return (conv * jax.nn.sigmoid(conv)) * us

    if n_slab == 1:
        is_start = (i % tiles_per_seq) == 0

        @pl.when(is_start)
        def _():
            halo_ref[...] = prev_ref[0]

        @pl.when(jnp.logical_not(is_start))
        def _():
            halo_ref[...] = carry_ref[j]

    starts = list(range(0, tm, sub))
    pending = matmuls(starts[0])
    before = halo_ref[...] if n_slab == 1 else None
    for t, r0 in enumerate(starts):
        gate, up = pending
        if t + 1 < len(starts):
            pending = matmuls(starts[t + 1])
        for s0 in range(0, sub, rows):
            gs, us = gate[s0:s0 + rows], up[s0:s0 + rows]
            if n_slab > 1:
                before = prev_ref[(r0 + s0) // rows]
            act_ref[r0 + s0:r0 + s0 + rows, :] = conv_act(gs, us, before).astype(BF16)
            before = gs[rows - halo:rows]
            if n_slab > 1:
                tail_ref[(r0 + s0) // rows] = before
    if n_slab == 1:
        tail_ref[0] = before
        carry_ref[j] = before


def _ffn_up(x, gain, wg, wu, conv_w, conv_b, prev, *, seq):
    m, d_model = x.shape
    d_ff = wg.shape[1]
    tm = _tile(m, FFN_ROW_TILE)
    tf = _tile(d_ff, FFN_COL_TILE)
    if tm >= seq:
        rows, n_slab, tiles_per_seq = seq, tm // seq, 1
        prev_spec = pl.BlockSpec((n_slab, SUBLANES, tf), lambda i, j: (i, 0, j))
    else:
        rows, n_slab, tiles_per_seq = tm, 1, seq // tm
        prev_spec = pl.BlockSpec((1, SUBLANES, tf), lambda i, j: (i // tiles_per_seq, 0, j))
    sub = _tile(tm, FFN_SUB_TILE)
    if n_slab == 1:
        rows = sub
    assert sub % rows == 0
    nj = d_ff // tf
    return pl.pallas_call(
        functools.partial(_ffn_up_kernel, rows=rows, n_slab=n_slab, tiles_per_seq=tiles_per_seq, sub=sub),
        grid=(m // tm, nj),
        in_specs=[pl.BlockSpec((tm, d_model), lambda i, j: (i, 0)),
                  pl.BlockSpec((1, d_model), lambda i, j: (0, 0)),
                  pl.BlockSpec((d_model, tf), lambda i, j: (0, j)),
                  pl.BlockSpec((d_model, tf), lambda i, j: (0, j)),
                  pl.BlockSpec((CONV_W, tf), lambda i, j: (0, j)),
                  pl.BlockSpec((1, tf), lambda i, j: (0, j)),
                  prev_spec],
        out_specs=[pl.BlockSpec((tm, tf), lambda i, j: (i, j)),
                   pl.BlockSpec((n_slab, SUBLANES, tf), lambda i, j: (i, 0, j))],
        out_shape=[jax.ShapeDtypeStruct((m, d_ff), BF16),
                   jax.ShapeDtypeStruct((m // tm * n_slab, SUBLANES, d_ff), F32)],
        scratch_shapes=[pltpu.VMEM((tm, d_model), BF16), pltpu.VMEM((SUBLANES, tf), F32),
                        pltpu.VMEM((nj, SUBLANES, tf), F32)],
        compiler_params=_cparams(2), name="ffn_up",
    )(x, gain, wg, wu, conv_w, conv_b, prev)


def _final_norm_kernel(x_ref, g_ref, o_ref):
    o_ref[...] = _rms(x_ref[...], g_ref[...], NORM_EPS)


def _final_norm(x, gain):
    m, d = x.shape
    tm = _tile(m, ROW_TILE)
    return pl.pallas_call(
        _final_norm_kernel, grid=(m // tm,),
        in_specs=[pl.BlockSpec((tm, d), lambda i: (i, 0)), pl.BlockSpec((1, d), lambda i: (0, 0))],
        out_specs=pl.BlockSpec((tm, d), lambda i: (i, 0)),
        out_shape=jax.ShapeDtypeStruct((m, d), F32),
        compiler_params=_cparams(1), name="final_norm",
    )(x, gain)


def _conv_ffn(y, seq, prev_state, gain, wg, wu, conv_w, conv_b, wd):
    n_seq = y.shape[0] // seq
    d_ff = wg.shape[1]
    prev = jnp.concatenate([jnp.zeros((n_seq, SUBLANES - (CONV_W - 1), d_ff), F32), prev_state], axis=1)
    act, tails = _ffn_up(y, gain, wg, wu, conv_w, conv_b, prev, seq=seq)
    y = _mm_res(act, wd, y)
    state = tails.reshape(n_seq, -1, SUBLANES, d_ff)[:, -1, SUBLANES - (CONV_W - 1):, :]
    return y, state


def kernel(x_prompt, x_sample, cache_mla_ckv, cache_mla_kpe, cache_swa_k, cache_swa_v, cache_diff_k,
           cache_diff_v, state_ffn_conv, norm_mix, norm_ffn, final_norm, mla_w_dq, mla_q_norm, mla_w_uq,
           mla_w_dkv, mla_kv_norm, mla_w_ukv, mla_w_o, swa_w_qkv, swa_sinks, swa_w_o, diff_w_qkv,
           diff_lambda_q1, diff_lambda_k1, diff_lambda_q2, diff_lambda_k2, diff_subln, diff_w_o,
           ffn_w_gate, ffn_w_up, ffn_conv_w, ffn_conv_b, ffn_w_down):
    bp, sp, d_model = x_prompt.shape
    bs, ss, _ = x_sample.shape
    depth = norm_mix.shape[0]
    past = cache_mla_ckv.shape[2]
    d_ff = ffn_w_gate.shape[2]
    q_rank = mla_w_dq.shape[2]
    kv_rank = mla_w_ukv.shape[1]

    yp = x_prompt.reshape(bp * sp, d_model)
    ys = x_sample.reshape(bs * ss, d_model)
    row_p = _tile(bp * sp, ROW_TILE)
    row_s = _tile(bs * ss, ROW_TILE)

    def tables(head_dim):
        return (_rope_tables(sp, 0, head_dim, row_p), _rope_tables(ss, past, head_dim, row_s))

    tab64, tab128 = tables(64), tables(128)
    outs_p = {k: [] for k in ("ckv", "kpe", "swk", "swv", "dk", "dv", "conv")}
    outs_s = {k: [] for k in ("ckv", "kpe", "swk", "swv", "dk", "dv", "conv")}

    for i in range(depth):
        kind, j = i % N_MIXERS, i // N_MIXERS
        gain = norm_mix[i][None, :]
        if kind == 0:
            q_scale = (MLA_NOPE + MLA_ROPE) ** -0.5 * LOG2E
            pad = (-(q_rank + kv_rank + MLA_ROPE)) % LANES
            w_cat = jnp.concatenate([mla_w_dq[j], mla_w_dkv[j], jnp.zeros((d_model, pad), F32)],
                                    axis=1).astype(BF16)
            w_uq = mla_w_uq[j].reshape(q_rank, MLA_HEADS, MLA_NOPE + MLA_ROPE)
            w_uq = jnp.pad(w_uq, ((0, 0), (0, 0), (0, MLA_QK_PAD - MLA_NOPE - MLA_ROPE)))
            w_uq = w_uq.reshape(q_rank, MLA_HEADS * MLA_QK_PAD).astype(BF16)
            w_ukv = mla_w_ukv[j].reshape(kv_rank, MLA_HEADS, MLA_NOPE + MLA_V)
            wk = w_ukv[:, :, :MLA_NOPE].reshape(kv_rank, MLA_HEADS * MLA_NOPE).astype(BF16)
            wv = w_ukv[:, :, MLA_NOPE:].reshape(kv_rank, MLA_HEADS * MLA_V).astype(BF16)
            w_o = mla_w_o[j].astype(BF16)
            qn, kvn = mla_q_norm[j][None, :], mla_kv_norm[j][None, :]

            (cos_p, sin_p), (cos_s, sin_s) = tab64
            qa, ckv_p, kpe_p = _mla_down(yp, gain, w_cat, qn, kvn, cos_p, sin_p, q_rank, kv_rank)
            q = _mla_q_up(qa, w_uq, cos_p, sin_p, q_scale)
            k, vt = _mla_expand(ckv_p, kpe_p, wk, wv.T, v_transposed=True)
            o = _flash_attention(q, k, vt, batch=bp, seq=sp, n_kv_heads=MLA_HEADS, dk=MLA_QK_PAD,
                                 dv=MLA_V, groups=1, n_maps=1)
            yp = _mm_res(o, w_o, yp)

            qa, ckv_s, kpe_s = _mla_down(ys, gain, w_cat, qn, kvn, cos_s, sin_s, q_rank, kv_rank)
            q = _mla_q_up(qa, w_uq, cos_s, sin_s, q_scale)
            kn, vn = _mla_expand(ckv_s, kpe_s, wk, wv, v_transposed=False)
            kc, vc = _mla_expand(cache_mla_ckv[j].reshape(bs * past, kv_rank),
                                 cache_mla_kpe[j].reshape(bs * past, MLA_ROPE), wk, wv, v_transposed=False)
            o = _decode_attention(q, kc, vc, kn, vn, batch=bs, past=past, new=ss, n_kv_heads=MLA_HEADS,
                                  dk=MLA_QK_PAD, dv=MLA_V, groups=1, n_maps=1)
            ys = _mm_res(o, w_o, ys)
            outs_p["ckv"].append(ckv_p.reshape(bp, sp, kv_rank))
            outs_p["kpe"].append(kpe_p.reshape(bp, sp, MLA_ROPE))
            outs_s["ckv"].append(ckv_s.reshape(bs, ss, kv_rank))
            outs_s["kpe"].append(kpe_s.reshape(bs, ss, MLA_ROPE))
        elif kind == 1:
            qw, kw = SWA_HEADS * SWA_HEAD_DIM, SWA_KV_HEADS * SWA_HEAD_DIM
            w_qkv = swa_w_qkv[j].astype(BF16)
            w_o = swa_w_o[j].astype(BF16)
            sinks = swa_sinks[j]
            (cos_p, sin_p), (cos_s, sin_s) = tab64
            q_scale = SWA_HEAD_DIM ** -0.5 * LOG2E
            q, k, v, kb, vt = _qkv_proj(yp, gain, w_qkv, cos_p, sin_p, qw, kw, kw, SWA_HEAD_DIM,
                                        q_scale=q_scale, vt_head=(SWA_HEAD_DIM, LANES - SWA_HEAD_DIM),
                                        vt_blocked=False)
            o = _swa_attention(q, kb, vt, sinks, seq=sp)
            yp = _mm_res(o, w_o, yp)
            k3 = k.reshape(bp, sp, SWA_KV_HEADS, SWA_HEAD_DIM)
            v3 = v.reshape(bp, sp, SWA_KV_HEADS, SWA_HEAD_DIM)
            outs_p["swk"].append(k3[:, sp - WINDOW:])
            outs_p["swv"].append(v3[:, sp - WINDOW:])

            q, k, v, _, _ = _qkv_proj(ys, gain, w_qkv, cos_s, sin_s, qw, kw, kw, SWA_HEAD_DIM, q_scale=q_scale)
            kc = cache_swa_k[j].reshape(bs * WINDOW, kw)
            vc = cache_swa_v[j].reshape(bs * WINDOW, kw)
            o = _swa_decode_attention(q, kc, k, vc, v, sinks, nq=ss, na=WINDOW, nb=ss)
            ys = _mm_res(o, w_o, ys)
            k_all = jnp.concatenate([cache_swa_k[j], k.reshape(bs, ss, SWA_KV_HEADS, SWA_HEAD_DIM)], axis=1)
            v_all = jnp.concatenate([cache_swa_v[j], v.reshape(bs, ss, SWA_KV_HEADS, SWA_HEAD_DIM)], axis=1)
            outs_s["swk"].append(k_all[:, ss:])
            outs_s["swv"].append(v_all[:, ss:])
        else:
            lam_init = 0.8 - 0.6 * math.exp(-0.3 * i)
            q_scale = DIFF_HEAD_DIM ** -0.5 * LOG2E
            groups = DIFF_HEADS // DIFF_KV_HEADS
            qw = DIFF_HEADS * 2 * DIFF_HEAD_DIM
            kw = DIFF_KV_HEADS * 2 * DIFF_HEAD_DIM
            w_qkv = diff_w_qkv[j].astype(BF16)
            w_o = diff_w_o[j].astype(BF16)
            lam = jnp.stack([diff_lambda_q1[j], diff_lambda_k1[j], diff_lambda_q2[j], diff_lambda_k2[j]])
            subln = diff_subln[j][None, :]
            common = dict(n_kv_heads=DIFF_KV_HEADS, dk=DIFF_HEAD_DIM, dv=2 * DIFF_HEAD_DIM, groups=groups,
                          n_maps=2, lam=lam, subln=subln, lam_init=lam_init)
            (cos_p, sin_p), (cos_s, sin_s) = tab128
            q, k, v, kb, vt = _qkv_proj(yp, gain, w_qkv, cos_p, sin_p, qw, kw, kw, DIFF_HEAD_DIM,
                                        q_scale=q_scale, vt_head=(2 * DIFF_HEAD_DIM, ONES_ROWS))
            o = _flash_attention(q, kb, vt, batch=bp, seq=sp, **common)
            yp = _mm_res(o, w_o, yp)
            outs_p["dk"].append(k.reshape(bp, sp, DIFF_KV_HEADS, 2, DIFF_HEAD_DIM))
            outs_p["dv"].append(v.reshape(bp, sp, DIFF_KV_HEADS, 2 * DIFF_HEAD_DIM))

            q, k, v, _, _ = _qkv_proj(ys, gain, w_qkv, cos_s, sin_s, qw, kw, kw, DIFF_HEAD_DIM,
                                      q_scale=q_scale)
            o = _decode_attention(q, cache_diff_k[j].reshape(bs * past, kw),
                                  cache_diff_v[j].reshape(bs * past, kw), k, v,
                                  batch=bs, past=past, new=ss, **common)
            ys = _mm_res(o, w_o, ys)
            outs_s["dk"].append(k.reshape(bs, ss, DIFF_KV_HEADS, 2, DIFF_HEAD_DIM))
            outs_s["dv"].append(v.reshape(bs, ss, DIFF_KV_HEADS, 2 * DIFF_HEAD_DIM))

        ffn_gain = norm_ffn[i][None, :]
        wg, wu, wd = ffn_w_gate[i].astype(BF16), ffn_w_up[i].astype(BF16), ffn_w_down[i].astype(BF16)
        cb = ffn_conv_b[i][None, :]
        yp, conv_p = _conv_ffn(yp, sp, jnp.zeros((bp, CONV_W - 1, d_ff), F32), ffn_gain, wg, wu,
                               ffn_conv_w[i], cb, wd)
        ys, conv_s = _conv_ffn(ys, ss, state_ffn_conv[i], ffn_gain, wg, wu, ffn_conv_w[i], cb, wd)
        outs_p["conv"].append(conv_p)
        outs_s["conv"].append(conv_s)

    fg = final_norm[None, :]
    y_prompt = _final_norm(yp, fg).reshape(bp, sp, d_model)
    y_sample = _final_norm(ys, fg).reshape(bs, ss, d_model)
    order = ("ckv", "kpe", "swk", "swv", "dk", "dv", "conv")
    return (y_prompt, y_sample) + tuple(jnp.stack(outs_p[k]) for k in order) + tuple(
        jnp.stack(outs_s[k]) for k in order)
```

```python
import functools
import math

import jax
import jax.numpy as jnp
from jax import lax
from jax.experimental import pallas as pl
from jax.experimental.pallas import tpu as pltpu

F32 = jnp.float32
BF16 = jnp.bfloat16

CHUNK = 64
ROPE_THETA = 10000.0
NORM_EPS = 1e-6
SUBLN_EPS = 1e-5
N_MIXERS = 3
MLA_HEADS = 16
MLA_NOPE = 128
MLA_ROPE = 64
MLA_V = 128
SWA_HEADS = 32
SWA_KV_HEADS = 4
SWA_HEAD_DIM = 64
WINDOW = 128
DIFF_HEADS = 8
DIFF_KV_HEADS = 4
DIFF_HEAD_DIM = 128
CONV_W = 3

LANES = 128
SUBLANES = 8
MXU_COLS = 256
MLA_QK_PAD = MXU_COLS
VMEM_LIMIT_BYTES = 56 * 1024 * 1024
LOG2E = 1.4426950408889634

ROW_TILE = 512
FFN_ROW_TILE = 1024
FFN_COL_TILE = 512
FFN_SUB_TILE = 256
ATTN_Q_TILE = 2048
ATTN_KV_TILE = 512
ATTN_CHAIN = 256
ATTN_LOOKAHEAD = 3
ONES_ROWS = 16
MLA_DECODE_KV_TILE = 1024
MM_TILE = 512
MM_WEIGHT_BLOCK_BYTES = 12 * 1024 * 1024
MM_SUB_TILE = 256


def _cparams(n_axes):
    return pltpu.CompilerParams(dimension_semantics=("arbitrary",) * n_axes,
                                vmem_limit_bytes=VMEM_LIMIT_BYTES)


def _resident(shape):
    return pl.BlockSpec(shape, lambda *_: (0,) * len(shape), pipeline_mode=pl.Buffered(1))


def _tile(n, pref):
    t = min(n, pref)
    assert n % t == 0, (n, pref)
    return t


def _rms(x, g, eps):
    ms = jnp.mean(x * x, axis=-1, keepdims=True)
    return (x * lax.rsqrt(ms + eps)) * g


def _rope_slab(x, cos, sin, head_dim):
    if head_dim == LANES:
        swapped = pltpu.roll(x, LANES // 2, axis=1)
    else:
        half = head_dim // 2
        lane = lax.broadcasted_iota(jnp.int32, x.shape, 1)
        first = (lane & (head_dim - 1)) < half
        swapped = jnp.where(first, pltpu.roll(x, LANES - half, axis=1), pltpu.roll(x, half, axis=1))
    return x * cos + swapped * sin


def _rope_tables(seq_len, offset, head_dim, rows):
    half = head_dim // 2
    pos = jnp.arange(seq_len, dtype=F32) + offset
    inv = ROPE_THETA ** (-jnp.arange(0, head_dim, 2, dtype=F32) / head_dim)
    ang = pos[:, None] * inv[None, :]
    cos, sin = jnp.cos(ang), jnp.sin(ang)
    reps = LANES // head_dim
    cos_l = jnp.tile(jnp.concatenate([cos, cos], axis=1), (1, reps))
    sin_l = jnp.tile(jnp.concatenate([-sin, sin], axis=1), (1, reps))
    n = max(rows // seq_len, 1)
    return jnp.tile(cos_l, (n, 1)), jnp.tile(sin_l, (n, 1))


def _mla_down_kernel(x_ref, g_ref, w_ref, qn_ref, kvn_ref, cos_ref, sin_ref,
                     qa_ref, ckv_ref, kpe_ref, *, q_rank, kv_rank):
    h = _rms(x_ref[...], g_ref[...], NORM_EPS).astype(BF16)
    d = jnp.dot(h, w_ref[...], preferred_element_type=F32)
    qa_ref[...] = _rms(d[:, :q_rank], qn_ref[...], NORM_EPS).astype(BF16)
    ckv_ref[...] = _rms(d[:, q_rank:q_rank + kv_rank], kvn_ref[...], NORM_EPS)
    slab = d[:, q_rank + kv_rank:q_rank + kv_rank + LANES]
    kpe_ref[...] = _rope_slab(slab, cos_ref[...], sin_ref[...], MLA_ROPE)[:, :MLA_ROPE]


def _mla_down(x, gain, w_cat, q_norm, kv_norm, cos, sin, q_rank, kv_rank):
    m, d_model = x.shape
    tm = _tile(m, ROW_TILE)
    n_tab = cos.shape[0] // tm
    wn = w_cat.shape[1]
    row = lambda i: (i, 0)
    fixed = lambda i: (0, 0)
    tab = lambda i: (i % n_tab, 0)
    return pl.pallas_call(
        functools.partial(_mla_down_kernel, q_rank=q_rank, kv_rank=kv_rank),
        grid=(m // tm,),
        in_specs=[pl.BlockSpec((tm, d_model), row), pl.BlockSpec((1, d_model), fixed),
                  _resident((d_model, wn)), pl.BlockSpec((1, q_rank), fixed),
                  pl.BlockSpec((1, kv_rank), fixed), pl.BlockSpec((tm, LANES), tab),
                  pl.BlockSpec((tm, LANES), tab)],
        out_specs=[pl.BlockSpec((tm, q_rank), row), pl.BlockSpec((tm, kv_rank), row),
                   pl.BlockSpec((tm, MLA_ROPE), row)],
        out_shape=[jax.ShapeDtypeStruct((m, q_rank), BF16), jax.ShapeDtypeStruct((m, kv_rank), F32),
                   jax.ShapeDtypeStruct((m, MLA_ROPE), F32)],
        compiler_params=_cparams(1), name="mla_down",
    )(x, gain, w_cat, q_norm, kv_norm, cos, sin)


def _mla_q_up_kernel(qa_ref, w_ref, cos_ref, sin_ref, q_ref, *, q_scale):
    qa = qa_ref[...]
    cos, sin = cos_ref[...], sin_ref[...]
    for h in range(MLA_HEADS):
        c0 = h * MLA_QK_PAD
        d = jnp.dot(qa, w_ref[:, c0:c0 + MLA_QK_PAD], preferred_element_type=F32)
        q_ref[:, c0:c0 + LANES] = (d[:, :LANES] * q_scale).astype(BF16)
        q_ref[:, c0 + LANES:c0 + MLA_QK_PAD] = (_rope_slab(d[:, LANES:], cos, sin, MLA_ROPE) * q_scale).astype(BF16)


def _mla_q_up(qa, w_pad, cos, sin, q_scale):
    m, q_rank = qa.shape
    tm = _tile(m, ROW_TILE)
    n_tab = cos.shape[0] // tm
    n = w_pad.shape[1]
    row = lambda i: (i, 0)
    fixed = lambda i: (0, 0)
    tab = lambda i: (i % n_tab, 0)
    return pl.pallas_call(
        functools.partial(_mla_q_up_kernel, q_scale=q_scale), grid=(m // tm,),
        in_specs=[pl.BlockSpec((tm, q_rank), row), _resident((q_rank, n)),
                  pl.BlockSpec((tm, LANES), tab), pl.BlockSpec((tm, LANES), tab)],
        out_specs=pl.BlockSpec((tm, n), row),
        out_shape=jax.ShapeDtypeStruct((m, n), BF16),
        compiler_params=_cparams(1), name="mla_q_up",
    )(qa, w_pad, cos, sin)


def _mla_expand_kernel(ckv_ref, kpe_ref, wk_ref, wvt_ref, k_ref, vt_ref):
    c = ckv_ref[...].astype(BF16)
    kpe = kpe_ref[...].astype(BF16)
    vt = lax.dot_general(wvt_ref[...], c, (((1,), (1,)), ((), ())), preferred_element_type=F32)
    dva = MLA_V + ONES_ROWS
    for h in range(MLA_HEADS):
        vt_ref[0, h * dva:h * dva + MLA_V, :] = vt[h * MLA_V:(h + 1) * MLA_V].astype(BF16)
        vt_ref[0, h * dva + MLA_V:(h + 1) * dva, :] = jnp.ones((ONES_ROWS, vt.shape[1]), BF16)
    kn = jnp.dot(c, wk_ref[...], preferred_element_type=F32).astype(BF16)
    zeros = jnp.zeros((kpe.shape[0], MLA_QK_PAD - MLA_NOPE - MLA_ROPE), BF16)
    for h in range(MLA_HEADS):
        c0 = h * MLA_QK_PAD
        k_ref[:, c0:c0 + MLA_NOPE] = kn[:, h * MLA_NOPE:(h + 1) * MLA_NOPE]
        k_ref[:, c0 + MLA_NOPE:c0 + MLA_NOPE + MLA_ROPE] = kpe
        k_ref[:, c0 + MLA_NOPE + MLA_ROPE:c0 + MLA_QK_PAD] = zeros


def _mla_expand(ckv, kpe, wk, wvt):
    m, kv_rank = ckv.shape
    tm = _tile(m, ATTN_KV_TILE)
    row = lambda i: (i, 0)
    nk, nva = MLA_HEADS * MLA_QK_PAD, MLA_HEADS * (MLA_V + ONES_ROWS)
    return pl.pallas_call(
        _mla_expand_kernel, grid=(m // tm,),
        in_specs=[pl.BlockSpec((tm, kv_rank), row), pl.BlockSpec((tm, MLA_ROPE), row),
                  _resident(wk.shape), _resident(wvt.shape)],
        out_specs=[pl.BlockSpec((tm, nk), row), pl.BlockSpec((1, nva, tm), lambda i: (i, 0, 0))],
        out_shape=[jax.ShapeDtypeStruct((m, nk), BF16), jax.ShapeDtypeStruct((m // tm, nva, tm), BF16)],
        compiler_params=_cparams(1), name="mla_expand",
    )(ckv, kpe, wk, wvt)


def _qkv_proj_kernel(x_ref, g_ref, w_ref, cos_ref, sin_ref, q_ref, k_ref, v_ref, kb_ref, vb_ref,
                     *, qw, kw, vw, head_dim, q_scale, vt_head):
    h = _rms(x_ref[...], g_ref[...], NORM_EPS).astype(BF16)
    cos, sin = cos_ref[...], sin_ref[...]
    vt_dst = None
    if vt_head:
        vt_dst = vb_ref.at[0] if len(vb_ref.shape) == 3 else vb_ref
    for n0 in range(0, qw + kw + vw, MXU_COLS):
        dd = jnp.dot(h, w_ref[:, n0:n0 + MXU_COLS], preferred_element_type=F32)
        for c0 in range(n0, n0 + MXU_COLS, LANES):
            d = dd[:, c0 - n0:c0 - n0 + LANES]
            if c0 < qw:
                q_ref[:, c0:c0 + LANES] = (_rope_slab(d, cos, sin, head_dim) * q_scale).astype(BF16)
            elif c0 < qw + kw:
                r = _rope_slab(d, cos, sin, head_dim)
                k_ref[:, c0 - qw:c0 - qw + LANES] = r
                kb_ref[:, c0 - qw:c0 - qw + LANES] = r.astype(BF16)
            else:
                c1 = c0 - qw - kw
                v_ref[:, c1:c1 + LANES] = d
                if vt_head:
                    vd, ones_rows = vt_head
                    dt = d.T.astype(BF16)
                    for hv in range(c1 // vd, (c1 + LANES - 1) // vd + 1):
                        lo, hi = max(c1, hv * vd), min(c1 + LANES, (hv + 1) * vd)
                        r0 = hv * (vd + ones_rows) + lo - hv * vd
                        vt_dst[r0:r0 + hi - lo, :] = dt[lo - c1:hi - c1]
                        if hi == (hv + 1) * vd:
                            vt_dst[r0 + hi - lo:r0 + hi - lo + ones_rows, :] = jnp.ones((ones_rows, d.shape[0]), BF16)
                else:
                    vb_ref[:, c1:c1 + LANES] = d.astype(BF16)


def _qkv_proj(x, gain, w, cos, sin, qw, kw, vw, head_dim, *, q_scale=1.0, vt_head=None, vt_blocked=True):
    m, d_model = x.shape
    tm = _tile(m, ATTN_KV_TILE if vt_head else ROW_TILE)
    n_tab = cos.shape[0] // tm
    row = lambda i: (i, 0)
    fixed = lambda i: (0, 0)
    tab = lambda i: (i % n_tab, 0)
    if vt_head:
        vwa = vw // vt_head[0] * (vt_head[0] + vt_head[1])
        if vt_blocked:
            vb_spec = pl.BlockSpec((1, vwa, tm), lambda i: (i, 0, 0))
            vb_shape = jax.ShapeDtypeStruct((m // tm, vwa, tm), BF16)
        else:
            vb_spec = pl.BlockSpec((vwa, tm), lambda i: (0, i))
            vb_shape = jax.ShapeDtypeStruct((vwa, m), BF16)
    else:
        vb_spec = pl.BlockSpec((tm, vw), row)
        vb_shape = jax.ShapeDtypeStruct((m, vw), BF16)
    return pl.pallas_call(
        functools.partial(_qkv_proj_kernel, qw=qw, kw=kw, vw=vw, head_dim=head_dim, q_scale=q_scale,
                          vt_head=vt_head),
        grid=(m // tm,),
        in_specs=[pl.BlockSpec((tm, d_model), row), pl.BlockSpec((1, d_model), fixed),
                  _resident(w.shape), pl.BlockSpec((tm, LANES), tab),
                  pl.BlockSpec((tm, LANES), tab)],
        out_specs=[pl.BlockSpec((tm, qw), row), pl.BlockSpec((tm, kw), row), pl.BlockSpec((tm, vw), row),
                   pl.BlockSpec((tm, kw), row), vb_spec],
        out_shape=[jax.ShapeDtypeStruct((m, qw), BF16), jax.ShapeDtypeStruct((m, kw), F32),
                   jax.ShapeDtypeStruct((m, vw), F32), jax.ShapeDtypeStruct((m, kw), BF16), vb_shape],
        compiler_params=_cparams(1), name="qkv_proj",
    )(x, gain, w, cos, sin)


def _mm_res_kernel(a_ref, w_ref, r_ref, o_ref, *, sub):
    starts = list(range(0, a_ref.shape[0], sub))
    pending = jnp.dot(a_ref[0:sub, :], w_ref[...], preferred_element_type=F32)
    for t, r0 in enumerate(starts):
        d = pending
        if t + 1 < len(starts):
            r1 = starts[t + 1]
            pending = jnp.dot(a_ref[r1:r1 + sub, :], w_ref[...], preferred_element_type=F32)
        o_ref[r0:r0 + sub, :] = r_ref[r0:r0 + sub, :] + d


def _mm_res(a, w, res):
    m, k = a.shape
    n = w.shape[1]
    tm = _tile(m, MM_TILE)
    tn = n
    while tn * k * w.dtype.itemsize > MM_WEIGHT_BLOCK_BYTES and tn % (2 * MXU_COLS) == 0:
        tn //= 2
    return pl.pallas_call(
        functools.partial(_mm_res_kernel, sub=_tile(tm, MM_SUB_TILE)), grid=(n // tn, m // tm),
        in_specs=[pl.BlockSpec((tm, k), lambda j, i: (i, 0)), pl.BlockSpec((k, tn), lambda j, i: (0, j)),
                  pl.BlockSpec((tm, tn), lambda j, i: (i, j))],
        out_specs=pl.BlockSpec((tm, tn), lambda j, i: (i, j)),
        out_shape=jax.ShapeDtypeStruct((m, n), F32),
        compiler_params=_cparams(2), name="mm_res",
    )(a, w, res)


def _diff_lambda(lam_ref, lam_init):
    lam = lam_ref[...]
    a = jnp.sum(lam[0:1] * lam[1:2], axis=-1, keepdims=True)
    b = jnp.sum(lam[2:3] * lam[3:4], axis=-1, keepdims=True)
    return jnp.exp(a) - jnp.exp(b) + lam_init


def _combine_heads(o_list, o_ref, rows, lam_ref, subln_ref, *, n_maps, dv, lam_init):
    groups = len(o_list) // n_maps
    for g in range(groups):
        if n_maps == 1:
            o = o_list[g]
        else:
            lam = _diff_lambda(lam_ref, lam_init)
            o = o_list[2 * g] - lam * o_list[2 * g + 1]
            o = _rms(o, subln_ref[...], SUBLN_EPS) * (1.0 - lam_init)
        o_ref[rows, g * dv:(g + 1) * dv] = o.astype(o_ref.dtype)


def _flash_kernel(*refs, tq, tk, cw, dk, dv, groups, n_maps, lam_init):
    if n_maps == 2:
        lam_ref, subln_ref, q_ref, k_ref, vt_ref, o_ref, m_ref, acc_ref, st_ref = refs
    else:
        q_ref, k_ref, vt_ref, o_ref, m_ref, acc_ref, st_ref = refs
        lam_ref = subln_ref = None
    n_sub = groups * n_maps
    n_chain = tq // cw
    kv_per_q = tq // tk
    qi = pl.program_id(2)
    nt = (((1,), (1,)), ((), ()))

    m_ref[...] = jnp.full(m_ref.shape, -jnp.inf, F32)
    acc_ref[...] = jnp.zeros(acc_ref.shape, F32)

    def scores(item, k_blks, slot):
        u, n, k_lo, bi = item
        mi = u % n_maps
        q_n = q_ref[n * cw:(n + 1) * cw, u * dk:(u + 1) * dk]
        st = lax.dot_general(k_blks[bi][:, mi * dk:(mi + 1) * dk], q_n, nt, preferred_element_type=F32)
        if k_lo is not None:
            kc = (lax.broadcasted_iota(jnp.int32, st.shape, 0) + k_lo) // CHUNK
            qc = (lax.broadcasted_iota(jnp.int32, st.shape, 1) + n * cw) // CHUNK
            st = jnp.where(kc <= qc, st, -jnp.inf)
        st_ref[slot] = st

    def update(item, slot, vt_blks):
        u, n, _, bi = item
        st = st_ref[slot]
        idx = u * n_chain + n
        m_prev = m_ref[idx]
        m_new = jnp.maximum(m_prev, jnp.max(st, axis=0, keepdims=True))
        alpha = jnp.exp2(m_prev - m_new)
        pt = jnp.exp2(st - m_new).astype(BF16)
        acc_ref[idx] = acc_ref[idx] * alpha + jnp.dot(vt_blks[bi], pt, preferred_element_type=F32)
        m_ref[idx] = m_new

    def run_chains(items, first_block):
        blocks = sorted({it[3] for it in items})
        k_blks = {bi: k_ref[pl.ds(pl.multiple_of((first_block + bi) * tk, tk), tk), :] for bi in blocks}
        vt_blks = {bi: vt_ref[first_block + bi] for bi in blocks}
        n_slots = ATTN_LOOKAHEAD + 1
        for t in range(min(ATTN_LOOKAHEAD, len(items))):
            scores(items[t], k_blks, t % n_slots)
        for t, item in enumerate(items):
            if t + ATTN_LOOKAHEAD < len(items):
                scores(items[t + ATTN_LOOKAHEAD], k_blks, (t + ATTN_LOOKAHEAD) % n_slots)
            update(item, t % n_slots, vt_blks)

    def body(j, carry):
        items = [(u, n, None, bi) for bi in range(kv_per_q) for n in range(n_chain) for u in range(n_sub)]
        run_chains(items, j * kv_per_q)
        return carry

    lax.fori_loop(0, qi, body, 0)

    items = [(u, n, bi * tk if (bi + 1) * tk > n * cw else None, bi)
             for bi in range(kv_per_q) for n in range(n_chain) if bi * tk < (n + 1) * cw
             for u in range(n_sub)]
    run_chains(items, qi * kv_per_q)

    for n in range(n_chain):
        outs = []
        for u in range(n_sub):
            a = acc_ref[u * n_chain + n]
            outs.append((a[:dv] / a[dv:dv + 1]).T)
        _combine_heads(outs, o_ref, slice(n * cw, (n + 1) * cw), lam_ref, subln_ref,
                       n_maps=n_maps, dv=dv, lam_init=lam_init)


def _flash_attention(q, k, vt, *, batch, seq, n_kv_heads, dk, dv, groups, n_maps,
                     lam=None, subln=None, lam_init=0.0):
    tk = vt.shape[2]
    dva = dv + ONES_ROWS
    n_sub = groups * n_maps
    tq = _tile(seq, max(ATTN_Q_TILE if n_sub == 1 else ATTN_Q_TILE // 2, tk))
    cw = _tile(tq, ATTN_CHAIN)
    assert tq % tk == 0 and tk % CHUNK == 0 and cw % CHUNK == 0
    nq, nk = seq // tq, seq // tk
    n_chain = tq // cw
    kern = functools.partial(_flash_kernel, tq=tq, tk=tk, cw=cw, dk=dk, dv=dv, groups=groups,
                             n_maps=n_maps, lam_init=lam_init)
    in_specs = [pl.BlockSpec((tq, n_sub * dk), lambda b, h, i: (b * nq + i, h)),
                pl.BlockSpec((seq, n_maps * dk), lambda b, h, i: (b, h)),
                pl.BlockSpec((nk, dva, tk), lambda b, h, i: (b, h, 0))]
    args = [q, k, vt]
    if n_maps == 2:
        in_specs = [pl.BlockSpec(lam.shape, lambda b, h, i: (0, 0)),
                    pl.BlockSpec(subln.shape, lambda b, h, i: (0, 0))] + in_specs
        args = [lam, subln] + args
    return pl.pallas_call(
        kern, grid=(batch, n_kv_heads, nq), in_specs=in_specs,
        out_specs=pl.BlockSpec((tq, groups * dv), lambda b, h, i: (b * nq + i, h)),
        out_shape=jax.ShapeDtypeStruct((batch * seq, n_kv_heads * groups * dv), BF16),
        scratch_shapes=[pltpu.VMEM((n_sub * n_chain, 1, cw), F32),
                        pltpu.VMEM((n_sub * n_chain, dva, cw), F32),
                        pltpu.VMEM((ATTN_LOOKAHEAD + 1, tk, cw), F32)],
        compiler_params=_cparams(3), name="flash_attention",
    )(*args)


def _decode_kernel(*refs, dk, dv, groups, n_maps, lam_init):
    if n_maps == 2:
        lam_ref, subln_ref, q_ref, kc_ref, vc_ref, kn_ref, vn_ref, o_ref = refs
    else:
        q_ref, kc_ref, vc_ref, kn_ref, vn_ref, o_ref = refs
        lam_ref = subln_ref = None
    nt = (((1,), (1,)), ((), ()))
    vc = vc_ref[...].astype(BF16)
    vn = vn_ref[...].astype(BF16)
    outs = []
    for u in range(groups * n_maps):
        mi = u % n_maps
        q = q_ref[:, u * dk:(u + 1) * dk]
        kc = kc_ref[:, mi * dk:(mi + 1) * dk].astype(BF16)
        kn = kn_ref[:, mi * dk:(mi + 1) * dk].astype(BF16)
        s1 = lax.dot_general(q, kc, nt, preferred_element_type=F32)
        s2 = lax.dot_general(q, kn, nt, preferred_element_type=F32)
        m = jnp.maximum(jnp.max(s1, axis=1, keepdims=True), jnp.max(s2, axis=1, keepdims=True))
        p1 = jnp.exp2(s1 - m)
        p2 = jnp.exp2(s2 - m)
        l = jnp.sum(p1, axis=1, keepdims=True) + jnp.sum(p2, axis=1, keepdims=True)
        o = (jnp.dot(p1.astype(BF16), vc, preferred_element_type=F32)
             + jnp.dot(p2.astype(BF16), vn, preferred_element_type=F32))
        outs.append(o / l)
    _combine_heads(outs, o_ref, slice(None), lam_ref, subln_ref, n_maps=n_maps, dv=dv, lam_init=lam_init)


def _decode_attention(q, kc, vc, kn, vn, *, batch, past, new, n_kv_heads, dk, dv, groups, n_maps,
                      lam=None, subln=None, lam_init=0.0):
    n_sub = groups * n_maps
    kern = functools.partial(_decode_kernel, dk=dk, dv=dv, groups=groups, n_maps=n_maps, lam_init=lam_init)
    bh = lambda b, h: (b, h)
    in_specs = [pl.BlockSpec((new, n_sub * dk), bh), pl.BlockSpec((past, n_maps * dk), bh),
                pl.BlockSpec((past, dv), bh), pl.BlockSpec((new, n_maps * dk), bh),
                pl.BlockSpec((new, dv), bh)]
    args = [q, kc, vc, kn, vn]
    if n_maps == 2:
        in_specs = [pl.BlockSpec(lam.shape, lambda b, h: (0, 0)),
                    pl.BlockSpec(subln.shape, lambda b, h: (0, 0))] + in_specs
        args = [lam, subln] + args
    return pl.pallas_call(
        kern, grid=(batch, n_kv_heads), in_specs=in_specs,
        out_specs=pl.BlockSpec((new, groups * dv), bh),
        out_shape=jax.ShapeDtypeStruct((batch * new, n_kv_heads * groups * dv), BF16),
        compiler_params=_cparams(2), name="decode_attention",
    )(*args)


def _mla_decode_kernel(q_ref, ckv_c_ref, kpe_c_ref, ckv_n_ref, kpe_n_ref, wk_ref, wv_ref, o_ref,
                       kcat_ref, qcat_ref, *, block):
    new = q_ref.shape[0]
    past, rank = ckv_c_ref.shape
    nt = (((1,), (1,)), ((), ()))
    kcat_ref[0:past, 0:rank] = ckv_c_ref[...].astype(BF16)
    kcat_ref[past:past + new, 0:rank] = ckv_n_ref[...].astype(BF16)
    kcat_ref[0:past, rank:rank + MLA_ROPE] = kpe_c_ref[...].astype(BF16)
    kcat_ref[past:past + new, rank:rank + MLA_ROPE] = kpe_n_ref[...].astype(BF16)
    kcat_ref[:, rank + MLA_ROPE:rank + LANES] = jnp.zeros((past + new, LANES - MLA_ROPE), BF16)
    for h in range(MLA_HEADS):
        c0 = h * MLA_QK_PAD
        q_lat = lax.dot_general(q_ref[:, c0:c0 + MLA_NOPE], wk_ref[:, h * MLA_NOPE:(h + 1) * MLA_NOPE], nt,
                                preferred_element_type=F32)
        qcat_ref[h * new:(h + 1) * new, 0:rank] = q_lat.astype(BF16)
        qcat_ref[h * new:(h + 1) * new, rank:rank + LANES] = q_ref[:, c0 + MLA_NOPE:c0 + MLA_QK_PAD]
    q = qcat_ref[...]
    rows = MLA_HEADS * new
    m = jnp.full((rows, 1), -jnp.inf, F32)
    l = jnp.zeros((rows, 1), F32)
    acc = jnp.zeros((rows, rank), F32)
    starts = list(range(0, past, block)) + [past]
    for k0 in starts:
        nk = min(block, past - k0) if k0 < past else new
        kb = kcat_ref[k0:k0 + nk, :]
        s = lax.dot_general(q, kb, nt, preferred_element_type=F32)
        m_new = jnp.maximum(m, jnp.max(s, axis=1, keepdims=True))
        alpha = jnp.exp2(m - m_new)
        p = jnp.exp2(s - m_new)
        l = alpha * l + jnp.sum(p, axis=1, keepdims=True)
        acc = alpha * acc + jnp.dot(p.astype(BF16), kb[:, 0:rank], preferred_element_type=F32)
        m = m_new
    o_lat = (acc / l).astype(BF16)
    for h in range(MLA_HEADS):
        o_ref[:, h * MLA_V:(h + 1) * MLA_V] = jnp.dot(
            o_lat[h * new:(h + 1) * new], wv_ref[:, h * MLA_V:(h + 1) * MLA_V],
            preferred_element_type=F32).astype(o_ref.dtype)


def _mla_decode_attention(q, ckv_cache, kpe_cache, ckv_new, kpe_new, wk, wv, *, batch, past, new):
    rank = ckv_cache.shape[1]
    per = lambda b: (b, 0)
    return pl.pallas_call(
        functools.partial(_mla_decode_kernel, block=_tile(past, MLA_DECODE_KV_TILE)), grid=(batch,),
        in_specs=[pl.BlockSpec((new, q.shape[1]), per), pl.BlockSpec((past, rank), per),
                  pl.BlockSpec((past, MLA_ROPE), per), pl.BlockSpec((new, rank), per),
                  pl.BlockSpec((new, MLA_ROPE), per), _resident(wk.shape), _resident(wv.shape)],
        out_specs=pl.BlockSpec((new, MLA_HEADS * MLA_V), per),
        out_shape=jax.ShapeDtypeStruct((batch * new, MLA_HEADS * MLA_V), BF16),
        scratch_shapes=[pltpu.VMEM((past + new, rank + LANES), BF16),
                        pltpu.VMEM((MLA_HEADS * new, rank + LANES), BF16)],
        compiler_params=_cparams(1), name="mla_decode_attention",
    )(q, ckv_cache, kpe_cache, ckv_new, kpe_new, wk, wv)


def _swa_decode_kernel(sinks_ref, q_ref, ka_ref, kb_ref, va_ref, vb_ref, o_ref):
    nq = q_ref.shape[0]
    group = SWA_HEADS // SWA_KV_HEADS
    d = SWA_HEAD_DIM
    nt = (((1,), (1,)), ((), ()))
    for h in range(SWA_KV_HEADS):
        hs = slice(h * d, (h + 1) * d)
        k = jnp.concatenate([ka_ref[:, hs], kb_ref[:, hs]], axis=0).astype(BF16)
        v = jnp.concatenate([va_ref[:, hs], vb_ref[:, hs]], axis=0).astype(BF16)
        q = jnp.concatenate([q_ref[:, (h * group + g) * d:(h * group + g + 1) * d] for g in range(group)],
                            axis=0)
        sink = jnp.concatenate([jnp.full((nq, 1), sinks_ref[h * group + g] * LOG2E, F32)
                                for g in range(group)], axis=0)
        s = lax.dot_general(q, k, nt, preferred_element_type=F32)
        m = jnp.maximum(jnp.max(s, axis=1, keepdims=True), sink)
        p = jnp.exp2(s - m)
        l = jnp.sum(p, axis=1, keepdims=True) + jnp.exp2(sink - m)
        o = jnp.dot(p.astype(BF16), v, preferred_element_type=F32) / l
        for g in range(0, group, 2):
            pair = jnp.concatenate([o[g * nq:(g + 1) * nq], o[(g + 1) * nq:(g + 2) * nq]], axis=1)
            c0 = (h * group + g) * d
            o_ref[:, c0:c0 + 2 * d] = pair.astype(o_ref.dtype)


def _swa_decode_attention(q, ka, kb, va, vb, sinks, *, nq, na, nb):
    m, qw = q.shape
    kvw = ka.shape[1]
    cur = lambda t: (t, 0)
    return pl.pallas_call(
        _swa_decode_kernel, grid=(m // nq,),
        in_specs=[pl.BlockSpec(memory_space=pltpu.SMEM), pl.BlockSpec((nq, qw), cur),
                  pl.BlockSpec((na, kvw), cur), pl.BlockSpec((nb, kvw), cur),
                  pl.BlockSpec((na, kvw), cur), pl.BlockSpec((nb, kvw), cur)],
        out_specs=pl.BlockSpec((nq, qw), cur),
        out_shape=jax.ShapeDtypeStruct((m, qw), BF16),
        compiler_params=_cparams(1), name="swa_decode_attention",
    )(sinks, q, ka, kb, va, vb)


def _swa_kernel(sinks_ref, q_ref, ka_ref, kb_ref, vta_ref, vtb_ref, o_ref, st_ref, *, blocks_per_seq):
    nq, na = q_ref.shape[0], ka_ref.shape[0]
    nk = na + kb_ref.shape[0]
    group = SWA_HEADS // SWA_KV_HEADS
    d = SWA_HEAD_DIM
    vrows = vta_ref.shape[0] // SWA_KV_HEADS
    nt = (((1,), (1,)), ((), ()))
    kc = lax.broadcasted_iota(jnp.int32, (nk, group * nq), 0) // CHUNK
    qc = (lax.broadcasted_iota(jnp.int32, (nk, group * nq), 1) % nq) // CHUNK
    first = (pl.program_id(0) % blocks_per_seq) == 0
    lo = jnp.where(first, na // CHUNK, 0)
    valid = (kc >= qc) & (kc <= qc + na // CHUNK) & (kc >= lo)

    def scores(h):
        hs = slice(h * d, (h + 1) * d)
        k = jnp.concatenate([ka_ref[:, hs], kb_ref[:, hs]], axis=0)
        q = jnp.concatenate([q_ref[:, (h * group + g) * d:(h * group + g + 1) * d] for g in range(group)],
                            axis=0)
        st = lax.dot_general(k, q, nt, preferred_element_type=F32)
        st_ref[h % 2] = jnp.where(valid, st, -jnp.inf)

    scores(0)
    for h in range(SWA_KV_HEADS):
        if h + 1 < SWA_KV_HEADS:
            scores(h + 1)
        st = st_ref[h % 2]
        sink = jnp.concatenate([jnp.full((1, nq), sinks_ref[h * group + g] * LOG2E, F32)
                                for g in range(group)], axis=1)
        m = jnp.maximum(jnp.max(st, axis=0, keepdims=True), sink)
        pt = jnp.exp2(st - m).astype(BF16)
        vt = jnp.concatenate([vta_ref[h * vrows:(h + 1) * vrows, :], vtb_ref[h * vrows:(h + 1) * vrows, :]],
                             axis=1)
        ot = jnp.dot(vt, pt, preferred_element_type=F32)
        l = ot[d:d + 1] + jnp.exp2(sink - m)
        o = ot[:d] / l
        for g in range(0, group, 2):
            pair = jnp.concatenate([o[:, g * nq:(g + 1) * nq], o[:, (g + 1) * nq:(g + 2) * nq]], axis=0)
            c0 = (h * group + g) * d
            o_ref[:, c0:c0 + 2 * d] = pair.T.astype(o_ref.dtype)


def _swa_attention(q, k, vt, sinks, *, seq):
    m, qw = q.shape
    kvw = k.shape[1]
    nq = WINDOW
    cur = lambda t: (t, 0)
    prev = lambda t: (jnp.maximum(t - 1, 0), 0)
    cur_t = lambda t: (0, t)
    prev_t = lambda t: (0, jnp.maximum(t - 1, 0))
    return pl.pallas_call(
        functools.partial(_swa_kernel, blocks_per_seq=seq // nq),
        grid=(m // nq,),
        in_specs=[pl.BlockSpec(memory_space=pltpu.SMEM), pl.BlockSpec((nq, qw), cur),
                  pl.BlockSpec((nq, kvw), prev), pl.BlockSpec((nq, kvw), cur),
                  pl.BlockSpec((vt.shape[0], nq), prev_t), pl.BlockSpec((vt.shape[0], nq), cur_t)],
        out_specs=pl.BlockSpec((nq, qw), cur),
        out_shape=jax.ShapeDtypeStruct((m, qw), BF16),
        scratch_shapes=[pltpu.VMEM((2, 2 * nq, SWA_HEADS // SWA_KV_HEADS * nq), F32)],
        compiler_params=_cparams(1), name="swa_attention",
    )(sinks, q, k, k, vt, vt)


def _ffn_up_kernel(x_ref, g_ref, wg_ref, wu_ref, cw_ref, cb_ref, prev_ref, act_ref, tail_ref,
                   h_ref, halo_ref, carry_ref, *, rows, n_slab, tiles_per_seq, sub):
    i, j = pl.program_id(0), pl.program_id(1)
    halo = SUBLANES

    @pl.when(j == 0)
    def _():
        h_ref[...] = _rms(x_ref[...], g_ref[...], NORM_EPS).astype(BF16)

    w0, w1, w2 = cw_ref[0:1, :], cw_ref[1:2, :], cw_ref[2:3, :]
    bias = cb_ref[...]
    tm = h_ref.shape[0]

    def matmuls(r0):
        h = h_ref[r0:r0 + sub, :]
        return (jnp.dot(h, wg_ref[...], preferred_element_type=F32),
                jnp.dot(h, wu_ref[...], preferred_element_type=F32))

    def conv_act(gs, us, before):
        r8 = lax.broadcasted_iota(jnp.int32, (halo, gs.shape[1]), 0)
        g1 = pltpu.roll(gs, 1, axis=0)
        g2 = pltpu.roll(gs, 2, axis=0)
        head1 = jnp.where(r8 == 0, before[halo - 1:halo], g1[:halo])
        head2 = jnp.where(r8 == 0, before[halo - 2:halo - 1], jnp.where(r8 == 1, before[halo - 1:halo], g2[:halo]))
        g1 = jnp.concatenate([head1, g1[halo:]], axis=0)
        g2 = jnp.concatenate([head2, g2[halo:]], axis=0)
        conv = ((bias + g2 * w0) + g1 * w1) + gs * w2
        return (conv * jax.nn.sigmoid(conv)) * us

    if n_slab == 1:
        is_start = (i % tiles_per_seq) == 0

        @pl.when(is_start)
        def _():
            halo_ref[...] = prev_ref[0]

        @pl.when(jnp.logical_not(is_start))
        def _():
            halo_ref[...] = carry_ref[j]

    starts = list(range(0, tm, sub))
    pending = matmuls(starts[0])
    before = halo_ref[...] if n_slab == 1 else None
    for t, r0 in enumerate(starts):
        gate, up = pending
        if t + 1 < len(starts):
            pending = matmuls(starts[t + 1])
        for s0 in range(0, sub, rows):
            gs, us = gate[s0:s0 + rows], up[s0:s0 + rows]
            if n_slab > 1:
                before = prev_ref[(r0 + s0) // rows]
            act_ref[r0 + s0:r0 + s0 + rows, :] = conv_act(gs, us, before).astype(BF16)
            before = gs[rows - halo:rows]
            if n_slab > 1:
                tail_ref[(r0 + s0) // rows] = before
    if n_slab == 1:
        tail_ref[0] = before
        carry_ref[j] = before


def _ffn_up(x, gain, wg, wu, conv_w, conv_b, prev, *, seq):
    m, d_model = x.shape
    d_ff = wg.shape[1]
    tm = _tile(m, FFN_ROW_TILE)
    tf = _tile(d_ff, FFN_COL_TILE)
    if tm >= seq:
        rows, n_slab, tiles_per_seq = seq, tm // seq, 1
        prev_spec = pl.BlockSpec((n_slab, SUBLANES, tf), lambda i, j: (i, 0, j))
    else:
        rows, n_slab, tiles_per_seq = tm, 1, seq // tm
        prev_spec = pl.BlockSpec((1, SUBLANES, tf), lambda i, j: (i // tiles_per_seq, 0, j))
    sub = _tile(tm, FFN_SUB_TILE)
    if n_slab == 1:
        rows = sub
    assert sub % rows == 0
    nj = d_ff // tf
    return pl.pallas_call(
        functools.partial(_ffn_up_kernel, rows=rows, n_slab=n_slab, tiles_per_seq=tiles_per_seq, sub=sub),
        grid=(m // tm, nj),
        in_specs=[pl.BlockSpec((tm, d_model), lambda i, j: (i, 0)),
                  pl.BlockSpec((1, d_model), lambda i, j: (0, 0)),
                  pl.BlockSpec((d_model, tf), lambda i, j: (0, j)),
                  pl.BlockSpec((d_model, tf), lambda i, j: (0, j)),
                  pl.BlockSpec((CONV_W, tf), lambda i, j: (0, j)),
                  pl.BlockSpec((1, tf), lambda i, j: (0, j)),
                  prev_spec],
        out_specs=[pl.BlockSpec((tm, tf), lambda i, j: (i, j)),
                   pl.BlockSpec((n_slab, SUBLANES, tf), lambda i, j: (i, 0, j))],
        out_shape=[jax.ShapeDtypeStruct((m, d_ff), BF16),
                   jax.ShapeDtypeStruct((m // tm * n_slab, SUBLANES, d_ff), F32)],
        scratch_shapes=[pltpu.VMEM((tm, d_model), BF16), pltpu.VMEM((SUBLANES, tf), F32),
                        pltpu.VMEM((nj, SUBLANES, tf), F32)],
        compiler_params=_cparams(2), name="ffn_up",
    )(x, gain, wg, wu, conv_w, conv_b, prev)


def _final_norm_kernel(x_ref, g_ref, o_ref):
    o_ref[...] = _rms(x_ref[...], g_ref[...], NORM_EPS)


def _final_norm(x, gain):
    m, d = x.shape
    tm = _tile(m, ROW_TILE)
    return pl.pallas_call(
        _final_norm_kernel, grid=(m // tm,),
        in_specs=[pl.BlockSpec((tm, d), lambda i: (i, 0)), pl.BlockSpec((1, d), lambda i: (0, 0))],
        out_specs=pl.BlockSpec((tm, d), lambda i: (i, 0)),
        out_shape=jax.ShapeDtypeStruct((m, d), F32),
        compiler_params=_cparams(1), name="final_norm",
    )(x, gain)


def _conv_ffn(y, seq, prev_state, gain, wg, wu, conv_w, conv_b, wd):
    n_seq = y.shape[0] // seq
    d_ff = wg.shape[1]
    prev = jnp.concatenate([jnp.zeros((n_seq, SUBLANES - (CONV_W - 1), d_ff), F32), prev_state], axis=1)
    act, tails = _ffn_up(y, gain, wg, wu, conv_w, conv_b, prev, seq=seq)
    y = _mm_res(act, wd, y)
    state = tails.reshape(n_seq, -1, SUBLANES, d_ff)[:, -1, SUBLANES - (CONV_W - 1):, :]
    return y, state


def kernel(x_prompt, x_sample, cache_mla_ckv, cache_mla_kpe, cache_swa_k, cache_swa_v, cache_diff_k,
           cache_diff_v, state_ffn_conv, norm_mix, norm_ffn, final_norm, mla_w_dq, mla_q_norm, mla_w_uq,
           mla_w_dkv, mla_kv_norm, mla_w_ukv, mla_w_o, swa_w_qkv, swa_sinks, swa_w_o, diff_w_qkv,
           diff_lambda_q1, diff_lambda_k1, diff_lambda_q2, diff_lambda_k2, diff_subln, diff_w_o,
           ffn_w_gate, ffn_w_up, ffn_conv_w, ffn_conv_b, ffn_w_down):
    bp, sp, d_model = x_prompt.shape
    bs, ss, _ = x_sample.shape
    depth = norm_mix.shape[0]
    past = cache_mla_ckv.shape[2]
    d_ff = ffn_w_gate.shape[2]
    q_rank = mla_w_dq.shape[2]
    kv_rank = mla_w_ukv.shape[1]

    yp = x_prompt.reshape(bp * sp, d_model)
    ys = x_sample.reshape(bs * ss, d_model)
    row_p = _tile(bp * sp, ROW_TILE)
    row_s = _tile(bs * ss, ROW_TILE)

    def tables(head_dim):
        return (_rope_tables(sp, 0, head_dim, row_p), _rope_tables(ss, past, head_dim, row_s))

    tab64, tab128 = tables(64), tables(128)
    outs_p = {k: [] for k in ("ckv", "kpe", "swk", "swv", "dk", "dv", "conv")}
    outs_s = {k: [] for k in ("ckv", "kpe", "swk", "swv", "dk", "dv", "conv")}

    for i in range(depth):
        kind, j = i % N_MIXERS, i // N_MIXERS
        gain = norm_mix[i][None, :]
        if kind == 0:
            q_scale = (MLA_NOPE + MLA_ROPE) ** -0.5 * LOG2E
            pad = (-(q_rank + kv_rank + MLA_ROPE)) % LANES
            w_cat = jnp.concatenate([mla_w_dq[j], mla_w_dkv[j], jnp.zeros((d_model, pad), F32)],
                                    axis=1).astype(BF16)
            w_uq = mla_w_uq[j].reshape(q_rank, MLA_HEADS, MLA_NOPE + MLA_ROPE)
            w_uq = jnp.pad(w_uq, ((0, 0), (0, 0), (0, MLA_QK_PAD - MLA_NOPE - MLA_ROPE)))
            w_uq = w_uq.reshape(q_rank, MLA_HEADS * MLA_QK_PAD).astype(BF16)
            w_ukv = mla_w_ukv[j].reshape(kv_rank, MLA_HEADS, MLA_NOPE + MLA_V)
            wk = w_ukv[:, :, :MLA_NOPE].reshape(kv_rank, MLA_HEADS * MLA_NOPE).astype(BF16)
            wv = w_ukv[:, :, MLA_NOPE:].reshape(kv_rank, MLA_HEADS * MLA_V).astype(BF16)
            w_o = mla_w_o[j].astype(BF16)
            qn, kvn = mla_q_norm[j][None, :], mla_kv_norm[j][None, :]

            (cos_p, sin_p), (cos_s, sin_s) = tab64
            qa, ckv_p, kpe_p = _mla_down(yp, gain, w_cat, qn, kvn, cos_p, sin_p, q_rank, kv_rank)
            q = _mla_q_up(qa, w_uq, cos_p, sin_p, q_scale)
            k, vt = _mla_expand(ckv_p, kpe_p, wk, wv.T)
            o = _flash_attention(q, k, vt, batch=bp, seq=sp, n_kv_heads=MLA_HEADS, dk=MLA_QK_PAD,
                                 dv=MLA_V, groups=1, n_maps=1)
            yp = _mm_res(o, w_o, yp)

            qa, ckv_s, kpe_s = _mla_down(ys, gain, w_cat, qn, kvn, cos_s, sin_s, q_rank, kv_rank)
            q = _mla_q_up(qa, w_uq, cos_s, sin_s, q_scale)
            o = _mla_decode_attention(q, cache_mla_ckv[j].reshape(bs * past, kv_rank),
                                      cache_mla_kpe[j].reshape(bs * past, MLA_ROPE), ckv_s, kpe_s, wk, wv,
                                      batch=bs, past=past, new=ss)
            ys = _mm_res(o, w_o, ys)
            outs_p["ckv"].append(ckv_p.reshape(bp, sp, kv_rank))
            outs_p["kpe"].append(kpe_p.reshape(bp, sp, MLA_ROPE))
            outs_s["ckv"].append(ckv_s.reshape(bs, ss, kv_rank))
            outs_s["kpe"].append(kpe_s.reshape(bs, ss, MLA_ROPE))
        elif kind == 1:
            qw, kw = SWA_HEADS * SWA_HEAD_DIM, SWA_KV_HEADS * SWA_HEAD_DIM
            w_qkv = swa_w_qkv[j].astype(BF16)
            w_o = swa_w_o[j].astype(BF16)
            sinks = swa_sinks[j]
            (cos_p, sin_p), (cos_s, sin_s) = tab64
            q_scale = SWA_HEAD_DIM ** -0.5 * LOG2E
            q, k, v, kb, vt = _qkv_proj(yp, gain, w_qkv, cos_p, sin_p, qw, kw, kw, SWA_HEAD_DIM,
                                        q_scale=q_scale, vt_head=(SWA_HEAD_DIM, LANES - SWA_HEAD_DIM),
                                        vt_blocked=False)
            o = _swa_attention(q, kb, vt, sinks, seq=sp)
            yp = _mm_res(o, w_o, yp)
            k3 = k.reshape(bp, sp, SWA_KV_HEADS, SWA_HEAD_DIM)
            v3 = v.reshape(bp, sp, SWA_KV_HEADS, SWA_HEAD_DIM)
            outs_p["swk"].append(k3[:, sp - WINDOW:])
            outs_p["swv"].append(v3[:, sp - WINDOW:])

            q, k, v, _, _ = _qkv_proj(ys, gain, w_qkv, cos_s, sin_s, qw, kw, kw, SWA_HEAD_DIM, q_scale=q_scale)
            kc = cache_swa_k[j].reshape(bs * WINDOW, kw)
            vc = cache_swa_v[j].reshape(bs * WINDOW, kw)
            o = _swa_decode_attention(q, kc, k, vc, v, sinks, nq=ss, na=WINDOW, nb=ss)
            ys = _mm_res(o, w_o, ys)
            k_all = jnp.concatenate([cache_swa_k[j], k.reshape(bs, ss, SWA_KV_HEADS, SWA_HEAD_DIM)], axis=1)
            v_all = jnp.concatenate([cache_swa_v[j], v.reshape(bs, ss, SWA_KV_HEADS, SWA_HEAD_DIM)], axis=1)
            outs_s["swk"].append(k_all[:, ss:])
            outs_s["swv"].append(v_all[:, ss:])
        else:
            lam_init = 0.8 - 0.6 * math.exp(-0.3 * i)
            q_scale = DIFF_HEAD_DIM ** -0.5 * LOG2E
            groups = DIFF_HEADS // DIFF_KV_HEADS
            qw = DIFF_HEADS * 2 * DIFF_HEAD_DIM
            kw = DIFF_KV_HEADS * 2 * DIFF_HEAD_DIM
            w_qkv = diff_w_qkv[j].astype(BF16)
            w_o = diff_w_o[j].astype(BF16)
            lam = jnp.stack([diff_lambda_q1[j], diff_lambda_k1[j], diff_lambda_q2[j], diff_lambda_k2[j]])
            subln = diff_subln[j][None, :]
            common = dict(n_kv_heads=DIFF_KV_HEADS, dk=DIFF_HEAD_DIM, dv=2 * DIFF_HEAD_DIM, groups=groups,
                          n_maps=2, lam=lam, subln=subln, lam_init=lam_init)
            (cos_p, sin_p), (cos_s, sin_s) = tab128
            q, k, v, kb, vt = _qkv_proj(yp, gain, w_qkv, cos_p, sin_p, qw, kw, kw, DIFF_HEAD_DIM,
                                        q_scale=q_scale, vt_head=(2 * DIFF_HEAD_DIM, ONES_ROWS))
            o = _flash_attention(q, kb, vt, batch=bp, seq=sp, **common)
            yp = _mm_res(o, w_o, yp)
            outs_p["dk"].append(k.reshape(bp, sp, DIFF_KV_HEADS, 2, DIFF_HEAD_DIM))
            outs_p["dv"].append(v.reshape(bp, sp, DIFF_KV_HEADS, 2 * DIFF_HEAD_DIM))

            q, k, v, _, _ = _qkv_proj(ys, gain, w_qkv, cos_s, sin_s, qw, kw, kw, DIFF_HEAD_DIM,
                                      q_scale=q_scale)
            o = _decode_attention(q, cache_diff_k[j].reshape(bs * past, kw),
                                  cache_diff_v[j].reshape(bs * past, kw), k, v,
                                  batch=bs, past=past, new=ss, **common)
            ys = _mm_res(o, w_o, ys)
            outs_s["dk"].append(k.reshape(bs, ss, DIFF_KV_HEADS, 2, DIFF_HEAD_DIM))
            outs_s["dv"].append(v.reshape(bs, ss, DIFF_KV_HEADS, 2 * DIFF_HEAD_DIM))

        ffn_gain = norm_ffn[i][None, :]
        wg, wu, wd = ffn_w_gate[i].astype(BF16), ffn_w_up[i].astype(BF16), ffn_w_down[i].astype(BF16)
        cb = ffn_conv_b[i][None, :]
        yp, conv_p = _conv_ffn(yp, sp, jnp.zeros((bp, CONV_W - 1, d_ff), F32), ffn_gain, wg, wu,
                               ffn_conv_w[i], cb, wd)
        ys, conv_s = _conv_ffn(ys, ss, state_ffn_conv[i], ffn_gain, wg, wu, ffn_conv_w[i], cb, wd)
        outs_p["conv"].append(conv_p)
        outs_s["conv"].append(conv_s)

    fg = final_norm[None, :]
    y_prompt = _final_norm(yp, fg).reshape(bp, sp, d_model)
    y_sample = _final_norm(ys, fg).reshape(bs, ss, d_model)
    order = ("ckv", "kpe", "swk", "swv", "dk", "dv", "conv")
    return (y_prompt, y_sample) + tuple(jnp.stack(outs_p[k]) for k in order) + tuple(
        jnp.stack(outs_s[k]) for k in order)
```

```python
import functools
import math

import jax
import jax.numpy as jnp
from jax import lax
from jax.experimental import pallas as pl
from jax.experimental.pallas import tpu as pltpu

F32 = jnp.float32
BF16 = jnp.bfloat16

CHUNK = 64
ROPE_THETA = 10000.0
NORM_EPS = 1e-6
SUBLN_EPS = 1e-5
N_MIXERS = 3
MLA_HEADS = 16
MLA_NOPE = 128
MLA_ROPE = 64
MLA_V = 128
SWA_HEADS = 32
SWA_KV_HEADS = 4
SWA_HEAD_DIM = 64
WINDOW = 128
DIFF_HEADS = 8
DIFF_KV_HEADS = 4
DIFF_HEAD_DIM = 128
CONV_W = 3

LANES = 128
SUBLANES = 8
MXU_COLS = 256
MLA_QK_PAD = MXU_COLS
VMEM_LIMIT_BYTES = 56 * 1024 * 1024
LOG2E = 1.4426950408889634

ROW_TILE = 512
FFN_ROW_TILE = 1024
FFN_COL_TILE = 512
FFN_SUB_TILE = 256
ATTN_Q_TILE = 2048
ATTN_KV_TILE = 512
ATTN_CHAIN = 256
ATTN_LOOKAHEAD = 3
ONES_ROWS = 16
MLA_DECODE_KV_TILE = 1024
MM_TILE = 512
MM_WEIGHT_BLOCK_BYTES = 12 * 1024 * 1024
MM_SUB_TILE = 256


def _cparams(n_axes):
    return pltpu.CompilerParams(dimension_semantics=("arbitrary",) * n_axes,
                                vmem_limit_bytes=VMEM_LIMIT_BYTES)


def _resident(shape):
    return pl.BlockSpec(shape, lambda *_: (0,) * len(shape), pipeline_mode=pl.Buffered(1))


def _tile(n, pref):
    t = min(n, pref)
    assert n % t == 0, (n, pref)
    return t


def _rms(x, g, eps):
    ms = jnp.mean(x * x, axis=-1, keepdims=True)
    return (x * lax.rsqrt(ms + eps)) * g


def _rope_slab(x, cos, sin, head_dim):
    if head_dim == LANES:
        swapped = pltpu.roll(x, LANES // 2, axis=1)
    else:
        half = head_dim // 2
        lane = lax.broadcasted_iota(jnp.int32, x.shape, 1)
        first = (lane & (head_dim - 1)) < half
        swapped = jnp.where(first, pltpu.roll(x, LANES - half, axis=1), pltpu.roll(x, half, axis=1))
    return x * cos + swapped * sin


def _rope_tables(seq_len, offset, head_dim, rows):
    half = head_dim // 2
    pos = jnp.arange(seq_len, dtype=F32) + offset
    inv = ROPE_THETA ** (-jnp.arange(0, head_dim, 2, dtype=F32) / head_dim)
    ang = pos[:, None] * inv[None, :]
    cos, sin = jnp.cos(ang), jnp.sin(ang)
    reps = LANES // head_dim
    cos_l = jnp.tile(jnp.concatenate([cos, cos], axis=1), (1, reps))
    sin_l = jnp.tile(jnp.concatenate([-sin, sin], axis=1), (1, reps))
    n = max(rows // seq_len, 1)
    return jnp.tile(cos_l, (n, 1)), jnp.tile(sin_l, (n, 1))


def _mla_down_kernel(x_ref, g_ref, w_ref, qn_ref, kvn_ref, cos_ref, sin_ref,
                     qa_ref, ckv_ref, kpe_ref, *, q_rank, kv_rank, sub):
    def project(r0):
        h = _rms(x_ref[r0:r0 + sub, :], g_ref[...], NORM_EPS).astype(BF16)
        return jnp.dot(h, w_ref[...], preferred_element_type=F32)

    starts = list(range(0, x_ref.shape[0], sub))
    pending = project(starts[0])
    for t, r0 in enumerate(starts):
        d = pending
        if t + 1 < len(starts):
            pending = project(starts[t + 1])
        rows = slice(r0, r0 + sub)
        qa_ref[rows, :] = _rms(d[:, :q_rank], qn_ref[...], NORM_EPS).astype(BF16)
        ckv_ref[rows, :] = _rms(d[:, q_rank:q_rank + kv_rank], kvn_ref[...], NORM_EPS)
        slab = d[:, q_rank + kv_rank:q_rank + kv_rank + LANES]
        kpe_ref[rows, :] = _rope_slab(slab, cos_ref[rows, :], sin_ref[rows, :], MLA_ROPE)[:, :MLA_ROPE]


def _mla_down(x, gain, w_cat, q_norm, kv_norm, cos, sin, q_rank, kv_rank):
    m, d_model = x.shape
    tm = _tile(m, ROW_TILE)
    n_tab = cos.shape[0] // tm
    wn = w_cat.shape[1]
    row = lambda i: (i, 0)
    fixed = lambda i: (0, 0)
    tab = lambda i: (i % n_tab, 0)
    return pl.pallas_call(
        functools.partial(_mla_down_kernel, q_rank=q_rank, kv_rank=kv_rank, sub=_tile(tm, MM_SUB_TILE // 2)),
        grid=(m // tm,),
        in_specs=[pl.BlockSpec((tm, d_model), row), pl.BlockSpec((1, d_model), fixed),
                  _resident((d_model, wn)), pl.BlockSpec((1, q_rank), fixed),
                  pl.BlockSpec((1, kv_rank), fixed), pl.BlockSpec((tm, LANES), tab),
                  pl.BlockSpec((tm, LANES), tab)],
        out_specs=[pl.BlockSpec((tm, q_rank), row), pl.BlockSpec((tm, kv_rank), row),
                   pl.BlockSpec((tm, MLA_ROPE), row)],
        out_shape=[jax.ShapeDtypeStruct((m, q_rank), BF16), jax.ShapeDtypeStruct((m, kv_rank), F32),
                   jax.ShapeDtypeStruct((m, MLA_ROPE), F32)],
        compiler_params=_cparams(1), name="mla_down",
    )(x, gain, w_cat, q_norm, kv_norm, cos, sin)


def _mla_q_up_kernel(qa_ref, w_ref, cos_ref, sin_ref, q_ref, *, q_scale):
    qa = qa_ref[...]
    cos, sin = cos_ref[...], sin_ref[...]
    for h in range(MLA_HEADS):
        c0 = h * MLA_QK_PAD
        d = jnp.dot(qa, w_ref[:, c0:c0 + MLA_QK_PAD], preferred_element_type=F32)
        q_ref[:, c0:c0 + LANES] = (d[:, :LANES] * q_scale).astype(BF16)
        q_ref[:, c0 + LANES:c0 + MLA_QK_PAD] = (_rope_slab(d[:, LANES:], cos, sin, MLA_ROPE) * q_scale).astype(BF16)


def _mla_q_up(qa, w_pad, cos, sin, q_scale):
    m, q_rank = qa.shape
    tm = _tile(m, ROW_TILE)
    n_tab = cos.shape[0] // tm
    n = w_pad.shape[1]
    row = lambda i: (i, 0)
    fixed = lambda i: (0, 0)
    tab = lambda i: (i % n_tab, 0)
    return pl.pallas_call(
        functools.partial(_mla_q_up_kernel, q_scale=q_scale), grid=(m // tm,),
        in_specs=[pl.BlockSpec((tm, q_rank), row), _resident((q_rank, n)),
                  pl.BlockSpec((tm, LANES), tab), pl.BlockSpec((tm, LANES), tab)],
        out_specs=pl.BlockSpec((tm, n), row),
        out_shape=jax.ShapeDtypeStruct((m, n), BF16),
        compiler_params=_cparams(1), name="mla_q_up",
    )(qa, w_pad, cos, sin)


def _mla_expand_kernel(ckv_ref, kpe_ref, wk_ref, wvt_ref, k_ref, vt_ref):
    c = ckv_ref[...].astype(BF16)
    kpe = kpe_ref[...].astype(BF16)
    tm = c.shape[0]
    dva = MLA_V + ONES_ROWS
    ones = jnp.ones((ONES_ROWS, tm), BF16)
    zeros = jnp.zeros((tm, MLA_QK_PAD - MLA_NOPE - MLA_ROPE), BF16)
    pair = MXU_COLS // MLA_NOPE

    def project(h0):
        kn = jnp.dot(c, wk_ref[:, h0 * MLA_NOPE:(h0 + pair) * MLA_NOPE], preferred_element_type=F32)
        vt = lax.dot_general(wvt_ref[h0 * MLA_V:(h0 + pair) * MLA_V, :], c, (((1,), (1,)), ((), ())),
                             preferred_element_type=F32)
        return kn, vt

    pending = project(0)
    for h0 in range(0, MLA_HEADS, pair):
        kn, vt = pending
        if h0 + pair < MLA_HEADS:
            pending = project(h0 + pair)
        for i in range(pair):
            h = h0 + i
            c0 = h * MLA_QK_PAD
            k_ref[:, c0:c0 + MLA_NOPE] = kn[:, i * MLA_NOPE:(i + 1) * MLA_NOPE].astype(BF16)
            k_ref[:, c0 + MLA_NOPE:c0 + MLA_NOPE + MLA_ROPE] = kpe
            k_ref[:, c0 + MLA_NOPE + MLA_ROPE:c0 + MLA_QK_PAD] = zeros
            vt_ref[0, h * dva:h * dva + MLA_V, :] = vt[i * MLA_V:(i + 1) * MLA_V].astype(BF16)
            vt_ref[0, h * dva + MLA_V:(h + 1) * dva, :] = ones


def _mla_expand(ckv, kpe, wk, wvt):
    m, kv_rank = ckv.shape
    tm = _tile(m, ATTN_KV_TILE)
    row = lambda i: (i, 0)
    nk, nva = MLA_HEADS * MLA_QK_PAD, MLA_HEADS * (MLA_V + ONES_ROWS)
    return pl.pallas_call(
        _mla_expand_kernel, grid=(m // tm,),
        in_specs=[pl.BlockSpec((tm, kv_rank), row), pl.BlockSpec((tm, MLA_ROPE), row),
                  _resident(wk.shape), _resident(wvt.shape)],
        out_specs=[pl.BlockSpec((tm, nk), row), pl.BlockSpec((1, nva, tm), lambda i: (i, 0, 0))],
        out_shape=[jax.ShapeDtypeStruct((m, nk), BF16), jax.ShapeDtypeStruct((m // tm, nva, tm), BF16)],
        compiler_params=_cparams(1), name="mla_expand",
    )(ckv, kpe, wk, wvt)


def _qkv_proj_kernel(x_ref, g_ref, w_ref, cos_ref, sin_ref, q_ref, k_ref, v_ref, kb_ref, vb_ref,
                     *, qw, kw, vw, head_dim, q_scale, vt_head):
    h = _rms(x_ref[...], g_ref[...], NORM_EPS).astype(BF16)
    cos, sin = cos_ref[...], sin_ref[...]
    vt_dst = None
    if vt_head:
        vt_dst = vb_ref.at[0] if len(vb_ref.shape) == 3 else vb_ref
    for n0 in range(0, qw + kw + vw, MXU_COLS):
        dd = jnp.dot(h, w_ref[:, n0:n0 + MXU_COLS], preferred_element_type=F32)
        for c0 in range(n0, n0 + MXU_COLS, LANES):
            d = dd[:, c0 - n0:c0 - n0 + LANES]
            if c0 < qw:
                q_ref[:, c0:c0 + LANES] = (_rope_slab(d, cos, sin, head_dim) * q_scale).astype(BF16)
            elif c0 < qw + kw:
                r = _rope_slab(d, cos, sin, head_dim)
                k_ref[:, c0 - qw:c0 - qw + LANES] = r
                kb_ref[:, c0 - qw:c0 - qw + LANES] = r.astype(BF16)
            else:
                c1 = c0 - qw - kw
                v_ref[:, c1:c1 + LANES] = d
                if vt_head:
                    vd, ones_rows = vt_head
                    dt = d.T.astype(BF16)
                    for hv in range(c1 // vd, (c1 + LANES - 1) // vd + 1):
                        lo, hi = max(c1, hv * vd), min(c1 + LANES, (hv + 1) * vd)
                        r0 = hv * (vd + ones_rows) + lo - hv * vd
                        vt_dst[r0:r0 + hi - lo, :] = dt[lo - c1:hi - c1]
                        if hi == (hv + 1) * vd:
                            vt_dst[r0 + hi - lo:r0 + hi - lo + ones_rows, :] = jnp.ones((ones_rows, d.shape[0]), BF16)
                else:
                    vb_ref[:, c1:c1 + LANES] = d.astype(BF16)


def _qkv_proj(x, gain, w, cos, sin, qw, kw, vw, head_dim, *, q_scale=1.0, vt_head=None, vt_blocked=True):
    m, d_model = x.shape
    tm = _tile(m, ATTN_KV_TILE if vt_head else ROW_TILE)
    n_tab = cos.shape[0] // tm
    row = lambda i: (i, 0)
    fixed = lambda i: (0, 0)
    tab = lambda i: (i % n_tab, 0)
    if vt_head:
        vwa = vw // vt_head[0] * (vt_head[0] + vt_head[1])
        if vt_blocked:
            vb_spec = pl.BlockSpec((1, vwa, tm), lambda i: (i, 0, 0))
            vb_shape = jax.ShapeDtypeStruct((m // tm, vwa, tm), BF16)
        else:
            vb_spec = pl.BlockSpec((vwa, tm), lambda i: (0, i))
            vb_shape = jax.ShapeDtypeStruct((vwa, m), BF16)
    else:
        vb_spec = pl.BlockSpec((tm, vw), row)
        vb_shape = jax.ShapeDtypeStruct((m, vw), BF16)
    return pl.pallas_call(
        functools.partial(_qkv_proj_kernel, qw=qw, kw=kw, vw=vw, head_dim=head_dim, q_scale=q_scale,
                          vt_head=vt_head),
        grid=(m // tm,),
        in_specs=[pl.BlockSpec((tm, d_model), row), pl.BlockSpec((1, d_model), fixed),
                  _resident(w.shape), pl.BlockSpec((tm, LANES), tab),
                  pl.BlockSpec((tm, LANES), tab)],
        out_specs=[pl.BlockSpec((tm, qw), row), pl.BlockSpec((tm, kw), row), pl.BlockSpec((tm, vw), row),
                   pl.BlockSpec((tm, kw), row), vb_spec],
        out_shape=[jax.ShapeDtypeStruct((m, qw), BF16), jax.ShapeDtypeStruct((m, kw), F32),
                   jax.ShapeDtypeStruct((m, vw), F32), jax.ShapeDtypeStruct((m, kw), BF16), vb_shape],
        compiler_params=_cparams(1), name="qkv_proj",
    )(x, gain, w, cos, sin)


def _mm_res_kernel(a_ref, w_ref, r_ref, o_ref, *, sub):
    starts = list(range(0, a_ref.shape[0], sub))
    pending = jnp.dot(a_ref[0:sub, :], w_ref[...], preferred_element_type=F32)
    for t, r0 in enumerate(starts):
        d = pending
        if t + 1 < len(starts):
            r1 = starts[t + 1]
            pending = jnp.dot(a_ref[r1:r1 + sub, :], w_ref[...], preferred_element_type=F32)
        o_ref[r0:r0 + sub, :] = r_ref[r0:r0 + sub, :] + d


def _mm_res(a, w, res):
    m, k = a.shape
    n = w.shape[1]
    tm = _tile(m, MM_TILE)
    tn = n
    while tn * k * w.dtype.itemsize > MM_WEIGHT_BLOCK_BYTES and tn % (2 * MXU_COLS) == 0:
        tn //= 2
    return pl.pallas_call(
        functools.partial(_mm_res_kernel, sub=_tile(tm, MM_SUB_TILE)), grid=(n // tn, m // tm),
        in_specs=[pl.BlockSpec((tm, k), lambda j, i: (i, 0)), pl.BlockSpec((k, tn), lambda j, i: (0, j)),
                  pl.BlockSpec((tm, tn), lambda j, i: (i, j))],
        out_specs=pl.BlockSpec((tm, tn), lambda j, i: (i, j)),
        out_shape=jax.ShapeDtypeStruct((m, n), F32),
        compiler_params=_cparams(2), name="mm_res",
    )(a, w, res)


def _diff_lambda(lam_ref, lam_init):
    lam = lam_ref[...]
    a = jnp.sum(lam[0:1] * lam[1:2], axis=-1, keepdims=True)
    b = jnp.sum(lam[2:3] * lam[3:4], axis=-1, keepdims=True)
    return jnp.exp(a) - jnp.exp(b) + lam_init


def _combine_heads(o_list, o_ref, rows, lam_ref, subln_ref, *, n_maps, dv, lam_init):
    groups = len(o_list) // n_maps
    for g in range(groups):
        if n_maps == 1:
            o = o_list[g]
        else:
            lam = _diff_lambda(lam_ref, lam_init)
            o = o_list[2 * g] - lam * o_list[2 * g + 1]
            o = _rms(o, subln_ref[...], SUBLN_EPS) * (1.0 - lam_init)
        o_ref[rows, g * dv:(g + 1) * dv] = o.astype(o_ref.dtype)


def _flash_kernel(*refs, tq, tk, cw, dk, dv, groups, n_maps, lam_init):
    if n_maps == 2:
        lam_ref, subln_ref, q_ref, k_ref, vt_ref, o_ref, m_ref, acc_ref, st_ref, mx_ref = refs
    else:
        q_ref, k_ref, vt_ref, o_ref, m_ref, acc_ref, st_ref, mx_ref = refs
        lam_ref = subln_ref = None
    n_sub = groups * n_maps
    n_chain = tq // cw
    kv_per_q = tq // tk
    qi = pl.program_id(2)
    nt = (((1,), (1,)), ((), ()))

    m_ref[...] = jnp.full(m_ref.shape, -jnp.inf, F32)
    acc_ref[...] = jnp.zeros(acc_ref.shape, F32)

    def scores(item, k_blks, slot):
        u, n, k_lo, bi = item
        mi = u % n_maps
        q_n = q_ref[n * cw:(n + 1) * cw, u * dk:(u + 1) * dk]
        st = lax.dot_general(k_blks[bi][:, mi * dk:(mi + 1) * dk], q_n, nt, preferred_element_type=F32)
        if k_lo is not None:
            kc = (lax.broadcasted_iota(jnp.int32, st.shape, 0) + k_lo) // CHUNK
            qc = (lax.broadcasted_iota(jnp.int32, st.shape, 1) + n * cw) // CHUNK
            st = jnp.where(kc <= qc, st, -jnp.inf)
        st_ref[slot] = st
        mx_ref[slot] = jnp.max(st, axis=0, keepdims=True)

    def update(item, slot, vt_blks):
        u, n, _, bi = item
        st = st_ref[slot]
        idx = u * n_chain + n
        m_prev = m_ref[idx]
        m_new = jnp.maximum(m_prev, mx_ref[slot])
        alpha = jnp.exp2(m_prev - m_new)
        pt = jnp.exp2(st - m_new).astype(BF16)
        acc_ref[idx] = acc_ref[idx] * alpha + jnp.dot(vt_blks[bi], pt, preferred_element_type=F32)
        m_ref[idx] = m_new

    def run_chains(items, first_block):
        blocks = sorted({it[3] for it in items})
        k_blks = {bi: k_ref[pl.ds(pl.multiple_of((first_block + bi) * tk, tk), tk), :] for bi in blocks}
        vt_blks = {bi: vt_ref[first_block + bi] for bi in blocks}
        n_slots = ATTN_LOOKAHEAD + 1
        for t in range(min(ATTN_LOOKAHEAD, len(items))):
            scores(items[t], k_blks, t % n_slots)
        for t, item in enumerate(items):
            if t + ATTN_LOOKAHEAD < len(items):
                scores(items[t + ATTN_LOOKAHEAD], k_blks, (t + ATTN_LOOKAHEAD) % n_slots)
            update(item, t % n_slots, vt_blks)

    def body(j, carry):
        items = [(u, n, None, bi) for bi in range(kv_per_q) for n in range(n_chain) for u in range(n_sub)]
        run_chains(items, j * kv_per_q)
        return carry

    lax.fori_loop(0, qi, body, 0)

    items = [(u, n, bi * tk if (bi + 1) * tk > n * cw else None, bi)
             for bi in range(kv_per_q) for n in range(n_chain) if bi * tk < (n + 1) * cw
             for u in range(n_sub)]
    run_chains(items, qi * kv_per_q)

    for n in range(n_chain):
        outs = []
        for u in range(n_sub):
            a = acc_ref[u * n_chain + n]
            outs.append((a[:dv] / a[dv:dv + 1]).T)
        _combine_heads(outs, o_ref, slice(n * cw, (n + 1) * cw), lam_ref, subln_ref,
                       n_maps=n_maps, dv=dv, lam_init=lam_init)


def _flash_attention(q, k, vt, *, batch, seq, n_kv_heads, dk, dv, groups, n_maps,
                     lam=None, subln=None, lam_init=0.0):
    tk = vt.shape[2]
    dva = dv + ONES_ROWS
    n_sub = groups * n_maps
    tq = _tile(seq, max(ATTN_Q_TILE if n_sub == 1 else ATTN_Q_TILE // 2, tk))
    cw = _tile(tq, ATTN_CHAIN)
    assert tq % tk == 0 and tk % CHUNK == 0 and cw % CHUNK == 0
    nq, nk = seq // tq, seq // tk
    n_chain = tq // cw
    kern = functools.partial(_flash_kernel, tq=tq, tk=tk, cw=cw, dk=dk, dv=dv, groups=groups,
                             n_maps=n_maps, lam_init=lam_init)
    in_specs = [pl.BlockSpec((tq, n_sub * dk), lambda b, h, i: (b * nq + i, h)),
                pl.BlockSpec((seq, n_maps * dk), lambda b, h, i: (b, h)),
                pl.BlockSpec((nk, dva, tk), lambda b, h, i: (b, h, 0))]
    args = [q, k, vt]
    if n_maps == 2:
        in_specs = [pl.BlockSpec(lam.shape, lambda b, h, i: (0, 0)),
                    pl.BlockSpec(subln.shape, lambda b, h, i: (0, 0))] + in_specs
        args = [lam, subln] + args
    return pl.pallas_call(
        kern, grid=(batch, n_kv_heads, nq), in_specs=in_specs,
        out_specs=pl.BlockSpec((tq, groups * dv), lambda b, h, i: (b * nq + i, h)),
        out_shape=jax.ShapeDtypeStruct((batch * seq, n_kv_heads * groups * dv), BF16),
        scratch_shapes=[pltpu.VMEM((n_sub * n_chain, 1, cw), F32),
                        pltpu.VMEM((n_sub * n_chain, dva, cw), F32),
                        pltpu.VMEM((ATTN_LOOKAHEAD + 1, tk, cw), F32),
                        pltpu.VMEM((ATTN_LOOKAHEAD + 1, 1, cw), F32)],
        compiler_params=_cparams(3), name="flash_attention",
    )(*args)


def _decode_kernel(*refs, dk, dv, groups, n_maps, lam_init):
    if n_maps == 2:
        lam_ref, subln_ref, q_ref, kc_ref, vc_ref, kn_ref, vn_ref, o_ref = refs
    else:
        q_ref, kc_ref, vc_ref, kn_ref, vn_ref, o_ref = refs
        lam_ref = subln_ref = None
    nt = (((1,), (1,)), ((), ()))
    vc = vc_ref[...].astype(BF16)
    vn = vn_ref[...].astype(BF16)
    outs = []
    for u in range(groups * n_maps):
        mi = u % n_maps
        q = q_ref[:, u * dk:(u + 1) * dk]
        kc = kc_ref[:, mi * dk:(mi + 1) * dk].astype(BF16)
        kn = kn_ref[:, mi * dk:(mi + 1) * dk].astype(BF16)
        s1 = lax.dot_general(q, kc, nt, preferred_element_type=F32)
        s2 = lax.dot_general(q, kn, nt, preferred_element_type=F32)
        m = jnp.maximum(jnp.max(s1, axis=1, keepdims=True), jnp.max(s2, axis=1, keepdims=True))
        p1 = jnp.exp2(s1 - m)
        p2 = jnp.exp2(s2 - m)
        l = jnp.sum(p1, axis=1, keepdims=True) + jnp.sum(p2, axis=1, keepdims=True)
        o = (jnp.dot(p1.astype(BF16), vc, preferred_element_type=F32)
             + jnp.dot(p2.astype(BF16), vn, preferred_element_type=F32))
        outs.append(o / l)
    _combine_heads(outs, o_ref, slice(None), lam_ref, subln_ref, n_maps=n_maps, dv=dv, lam_init=lam_init)


def _decode_attention(q, kc, vc, kn, vn, *, batch, past, new, n_kv_heads, dk, dv, groups, n_maps,
                      lam=None, subln=None, lam_init=0.0):
    n_sub = groups * n_maps
    kern = functools.partial(_decode_kernel, dk=dk, dv=dv, groups=groups, n_maps=n_maps, lam_init=lam_init)
    bh = lambda b, h: (b, h)
    in_specs = [pl.BlockSpec((new, n_sub * dk), bh), pl.BlockSpec((past, n_maps * dk), bh),
                pl.BlockSpec((past, dv), bh), pl.BlockSpec((new, n_maps * dk), bh),
                pl.BlockSpec((new, dv), bh)]
    args = [q, kc, vc, kn, vn]
    if n_maps == 2:
        in_specs = [pl.BlockSpec(lam.shape, lambda b, h: (0, 0)),
                    pl.BlockSpec(subln.shape, lambda b, h: (0, 0))] + in_specs
        args = [lam, subln] + args
    return pl.pallas_call(
        kern, grid=(batch, n_kv_heads), in_specs=in_specs,
        out_specs=pl.BlockSpec((new, groups * dv), bh),
        out_shape=jax.ShapeDtypeStruct((batch * new, n_kv_heads * groups * dv), BF16),
        compiler_params=_cparams(2), name="decode_attention",
    )(*args)


def _mla_decode_kernel(q_ref, ckv_c_ref, kpe_c_ref, ckv_n_ref, kpe_n_ref, wk_ref, wv_ref, o_ref,
                       kcat_ref, qcat_ref, *, block):
    new = q_ref.shape[0]
    past, rank = ckv_c_ref.shape
    nt = (((1,), (1,)), ((), ()))
    kcat_ref[0:past, 0:rank] = ckv_c_ref[...].astype(BF16)
    kcat_ref[past:past + new, 0:rank] = ckv_n_ref[...].astype(BF16)
    kcat_ref[0:past, rank:rank + MLA_ROPE] = kpe_c_ref[...].astype(BF16)
    kcat_ref[past:past + new, rank:rank + MLA_ROPE] = kpe_n_ref[...].astype(BF16)
    kcat_ref[:, rank + MLA_ROPE:rank + LANES] = jnp.zeros((past + new, LANES - MLA_ROPE), BF16)
    for h in range(MLA_HEADS):
        c0 = h * MLA_QK_PAD
        q_lat = lax.dot_general(q_ref[:, c0:c0 + MLA_NOPE], wk_ref[:, h * MLA_NOPE:(h + 1) * MLA_NOPE], nt,
                                preferred_element_type=F32)
        qcat_ref[h * new:(h + 1) * new, 0:rank] = q_lat.astype(BF16)
        qcat_ref[h * new:(h + 1) * new, rank:rank + LANES] = q_ref[:, c0 + MLA_NOPE:c0 + MLA_QK_PAD]
    q = qcat_ref[...]
    rows = MLA_HEADS * new
    m = jnp.full((rows, 1), -jnp.inf, F32)
    l = jnp.zeros((rows, 1), F32)
    acc = jnp.zeros((rows, rank), F32)
    starts = list(range(0, past, block)) + [past]
    for k0 in starts:
        nk = min(block, past - k0) if k0 < past else new
        kb = kcat_ref[k0:k0 + nk, :]
        s = lax.dot_general(q, kb, nt, preferred_element_type=F32)
        m_new = jnp.maximum(m, jnp.max(s, axis=1, keepdims=True))
        alpha = jnp.exp2(m - m_new)
        p = jnp.exp2(s - m_new)
        l = alpha * l + jnp.sum(p, axis=1, keepdims=True)
        acc = alpha * acc + jnp.dot(p.astype(BF16), kb[:, 0:rank], preferred_element_type=F32)
        m = m_new
    o_lat = (acc / l).astype(BF16)
    for h in range(MLA_HEADS):
        o_ref[:, h * MLA_V:(h + 1) * MLA_V] = jnp.dot(
            o_lat[h * new:(h + 1) * new], wv_ref[:, h * MLA_V:(h + 1) * MLA_V],
            preferred_element_type=F32).astype(o_ref.dtype)


def _mla_decode_attention(q, ckv_cache, kpe_cache, ckv_new, kpe_new, wk, wv, *, layer, batch, past, new):
    rank = ckv_cache.shape[1]
    per = lambda b: (b, 0)
    cached = lambda b: (layer * batch + b, 0)
    return pl.pallas_call(
        functools.partial(_mla_decode_kernel, block=_tile(past, MLA_DECODE_KV_TILE)), grid=(batch,),
        in_specs=[pl.BlockSpec((new, q.shape[1]), per), pl.BlockSpec((past, rank), cached),
                  pl.BlockSpec((past, MLA_ROPE), cached), pl.BlockSpec((new, rank), per),
                  pl.BlockSpec((new, MLA_ROPE), per), _resident(wk.shape), _resident(wv.shape)],
        out_specs=pl.BlockSpec((new, MLA_HEADS * MLA_V), per),
        out_shape=jax.ShapeDtypeStruct((batch * new, MLA_HEADS * MLA_V), BF16),
        scratch_shapes=[pltpu.VMEM((past + new, rank + LANES), BF16),
                        pltpu.VMEM((MLA_HEADS * new, rank + LANES), BF16)],
        compiler_params=_cparams(1), name="mla_decode_attention",
    )(q, ckv_cache, kpe_cache, ckv_new, kpe_new, wk, wv)


def _swa_decode_kernel(sinks_ref, q_ref, ka_ref, kb_ref, va_ref, vb_ref, o_ref):
    nq = q_ref.shape[0]
    group = SWA_HEADS // SWA_KV_HEADS
    d = SWA_HEAD_DIM
    nt = (((1,), (1,)), ((), ()))
    for h in range(SWA_KV_HEADS):
        hs = slice(h * d, (h + 1) * d)
        k = jnp.concatenate([ka_ref[:, hs], kb_ref[:, hs]], axis=0).astype(BF16)
        v = jnp.concatenate([va_ref[:, hs], vb_ref[:, hs]], axis=0).astype(BF16)
        q = jnp.concatenate([q_ref[:, (h * group + g) * d:(h * group + g + 1) * d] for g in range(group)],
                            axis=0)
        sink = jnp.concatenate([jnp.full((nq, 1), sinks_ref[h * group + g] * LOG2E, F32)
                                for g in range(group)], axis=0)
        s = lax.dot_general(q, k, nt, preferred_element_type=F32)
        m = jnp.maximum(jnp.max(s, axis=1, keepdims=True), sink)
        p = jnp.exp2(s - m)
        l = jnp.sum(p, axis=1, keepdims=True) + jnp.exp2(sink - m)
        o = jnp.dot(p.astype(BF16), v, preferred_element_type=F32) / l
        for g in range(0, group, 2):
            pair = jnp.concatenate([o[g * nq:(g + 1) * nq], o[(g + 1) * nq:(g + 2) * nq]], axis=1)
            c0 = (h * group + g) * d
            o_ref[:, c0:c0 + 2 * d] = pair.astype(o_ref.dtype)


def _swa_decode_attention(q, ka, kb, va, vb, sinks, *, nq, na, nb):
    m, qw = q.shape
    kvw = ka.shape[1]
    cur = lambda t: (t, 0)
    return pl.pallas_call(
        _swa_decode_kernel, grid=(m // nq,),
        in_specs=[pl.BlockSpec(memory_space=pltpu.SMEM), pl.BlockSpec((nq, qw), cur),
                  pl.BlockSpec((na, kvw), cur), pl.BlockSpec((nb, kvw), cur),
                  pl.BlockSpec((na, kvw), cur), pl.BlockSpec((nb, kvw), cur)],
        out_specs=pl.BlockSpec((nq, qw), cur),
        out_shape=jax.ShapeDtypeStruct((m, qw), BF16),
        compiler_params=_cparams(1), name="swa_decode_attention",
    )(sinks, q, ka, kb, va, vb)


def _swa_kernel(sinks_ref, q_ref, ka_ref, kb_ref, vta_ref, vtb_ref, o_ref, st_ref, *, blocks_per_seq):
    nq, na = q_ref.shape[0], ka_ref.shape[0]
    nk = na + kb_ref.shape[0]
    group = SWA_HEADS // SWA_KV_HEADS
    d = SWA_HEAD_DIM
    vrows = vta_ref.shape[0] // SWA_KV_HEADS
    nt = (((1,), (1,)), ((), ()))
    kc = lax.broadcasted_iota(jnp.int32, (nk, group * nq), 0) // CHUNK
    qc = (lax.broadcasted_iota(jnp.int32, (nk, group * nq), 1) % nq) // CHUNK
    first = (pl.program_id(0) % blocks_per_seq) == 0
    lo = jnp.where(first, na // CHUNK, 0)
    valid = (kc >= qc) & (kc <= qc + na // CHUNK) & (kc >= lo)

    def scores(h):
        hs = slice(h * d, (h + 1) * d)
        k = jnp.concatenate([ka_ref[:, hs], kb_ref[:, hs]], axis=0)
        q = jnp.concatenate([q_ref[:, (h * group + g) * d:(h * group + g + 1) * d] for g in range(group)],
                            axis=0)
        st = lax.dot_general(k, q, nt, preferred_element_type=F32)
        st_ref[h % 2] = jnp.where(valid, st, -jnp.inf)

    scores(0)
    for h in range(SWA_KV_HEADS):
        if h + 1 < SWA_KV_HEADS:
            scores(h + 1)
        st = st_ref[h % 2]
        sink = jnp.concatenate([jnp.full((1, nq), sinks_ref[h * group + g] * LOG2E, F32)
                                for g in range(group)], axis=1)
        m = jnp.maximum(jnp.max(st, axis=0, keepdims=True), sink)
        pt = jnp.exp2(st - m).astype(BF16)
        vt = jnp.concatenate([vta_ref[h * vrows:(h + 1) * vrows, :], vtb_ref[h * vrows:(h + 1) * vrows, :]],
                             axis=1)
        ot = jnp.dot(vt, pt, preferred_element_type=F32)
        l = ot[d:d + 1] + jnp.exp2(sink - m)
        o = ot[:d] / l
        for g in range(0, group, 2):
            pair = jnp.concatenate([o[:, g * nq:(g + 1) * nq], o[:, (g + 1) * nq:(g + 2) * nq]], axis=0)
            c0 = (h * group + g) * d
            o_ref[:, c0:c0 + 2 * d] = pair.T.astype(o_ref.dtype)


def _swa_attention(q, k, vt, sinks, *, seq):
    m, qw = q.shape
    kvw = k.shape[1]
    nq = WINDOW
    cur = lambda t: (t, 0)
    prev = lambda t: (jnp.maximum(t - 1, 0), 0)
    cur_t = lambda t: (0, t)
    prev_t = lambda t: (0, jnp.maximum(t - 1, 0))
    return pl.pallas_call(
        functools.partial(_swa_kernel, blocks_per_seq=seq // nq),
        grid=(m // nq,),
        in_specs=[pl.BlockSpec(memory_space=pltpu.SMEM), pl.BlockSpec((nq, qw), cur),
                  pl.BlockSpec((nq, kvw), prev), pl.BlockSpec((nq, kvw), cur),
                  pl.BlockSpec((vt.shape[0], nq), prev_t), pl.BlockSpec((vt.shape[0], nq), cur_t)],
        out_specs=pl.BlockSpec((nq, qw), cur),
        out_shape=jax.ShapeDtypeStruct((m, qw), BF16),
        scratch_shapes=[pltpu.VMEM((2, 2 * nq, SWA_HEADS // SWA_KV_HEADS * nq), F32)],
        compiler_params=_cparams(1), name="swa_attention",
    )(sinks, q, k, k, vt, vt)


def _ffn_up_kernel(x_ref, g_ref, wg_ref, wu_ref, cw_ref, cb_ref, prev_ref, act_ref, tail_ref,
                   h_ref, halo_ref, carry_ref, *, rows, n_slab, tiles_per_seq, sub):
    i, j = pl.program_id(0), pl.program_id(1)
    halo = SUBLANES

    @pl.when(j == 0)
    def _():
        h_ref[...] = _rms(x_ref[...], g_ref[...], NORM_EPS).astype(BF16)

    w0, w1, w2 = cw_ref[0:1, :], cw_ref[1:2, :], cw_ref[2:3, :]
    bias = cb_ref[...]
    tm = h_ref.shape[0]

    def matmuls(r0):
        h = h_ref[r0:r0 + sub, :]
        return (jnp.dot(h, wg_ref[...], preferred_element_type=F32),
                jnp.dot(h, wu_ref[...], preferred_element_type=F32))

    def conv_act(gs, us, before):
        r8 = lax.broadcasted_iota(jnp.int32, (halo, gs.shape[1]), 0)
        g1 = pltpu.roll(gs, 1, axis=0)
        g2 = pltpu.roll(gs, 2, axis=0)
        head1 = jnp.where(r8 == 0, before[halo - 1:halo], g1[:halo])
        head2 = jnp.where(r8 == 0, before[halo - 2:halo - 1], jnp.where(r8 == 1, before[halo - 1:halo], g2[:halo]))
        g1 = jnp.concatenate([head1, g1[halo:]], axis=0)
        g2 = jnp.concatenate([head2, g2[halo:]], axis=0)
        conv = ((bias + g2 * w0) + g1 * w1) + gs * w2
        return (conv * jax.nn.sigmoid(conv)) * us

    if n_slab == 1:
        is_start = (i % tiles_per_seq) == 0

        @pl.when(is_start)
        def _():
            halo_ref[...] = prev_ref[0]

        @pl.when(jnp.logical_not(is_start))
        def _():
            halo_ref[...] = carry_ref[j]

    starts = list(range(0, tm, sub))
    pending = matmuls(starts[0])
    before = halo_ref[...] if n_slab == 1 else None
    for t, r0 in enumerate(starts):
        gate, up = pending
        if t + 1 < len(starts):
            pending = matmuls(starts[t + 1])
        for s0 in range(0, sub, rows):
            gs, us = gate[s0:s0 + rows], up[s0:s0 + rows]
            if n_slab > 1:
                before = prev_ref[(r0 + s0) // rows]
            act_ref[r0 + s0:r0 + s0 + rows, :] = conv_act(gs, us, before).astype(BF16)
            before = gs[rows - halo:rows]
            if n_slab > 1:
                tail_ref[(r0 + s0) // rows] = before
    if n_slab == 1:
        tail_ref[0] = before
        carry_ref[j] = before


def _ffn_up(x, gain, wg, wu, conv_w, conv_b, prev, *, seq):
    m, d_model = x.shape
    d_ff = wg.shape[1]
    tm = _tile(m, FFN_ROW_TILE)
    tf = _tile(d_ff, FFN_COL_TILE)
    if tm >= seq:
        rows, n_slab, tiles_per_seq = seq, tm // seq, 1
        prev_spec = pl.BlockSpec((n_slab, SUBLANES, tf), lambda i, j: (i, 0, j))
    else:
        rows, n_slab, tiles_per_seq = tm, 1, seq // tm
        prev_spec = pl.BlockSpec((1, SUBLANES, tf), lambda i, j: (i // tiles_per_seq, 0, j))
    sub = _tile(tm, FFN_SUB_TILE)
    if n_slab == 1:
        rows = sub
    assert sub % rows == 0
    nj = d_ff // tf
    return pl.pallas_call(
        functools.partial(_ffn_up_kernel, rows=rows, n_slab=n_slab, tiles_per_seq=tiles_per_seq, sub=sub),
        grid=(m // tm, nj),
        in_specs=[pl.BlockSpec((tm, d_model), lambda i, j: (i, 0)),
                  pl.BlockSpec((1, d_model), lambda i, j: (0, 0)),
                  pl.BlockSpec((d_model, tf), lambda i, j: (0, j)),
                  pl.BlockSpec((d_model, tf), lambda i, j: (0, j)),
                  pl.BlockSpec((CONV_W, tf), lambda i, j: (0, j)),
                  pl.BlockSpec((1, tf), lambda i, j: (0, j)),
                  prev_spec],
        out_specs=[pl.BlockSpec((tm, tf), lambda i, j: (i, j)),
                   pl.BlockSpec((n_slab, SUBLANES, tf), lambda i, j: (i, 0, j))],
        out_shape=[jax.ShapeDtypeStruct((m, d_ff), BF16),
                   jax.ShapeDtypeStruct((m // tm * n_slab, SUBLANES, d_ff), F32)],
        scratch_shapes=[pltpu.VMEM((tm, d_model), BF16), pltpu.VMEM((SUBLANES, tf), F32),
                        pltpu.VMEM((nj, SUBLANES, tf), F32)],
        compiler_params=_cparams(2), name="ffn_up",
    )(x, gain, wg, wu, conv_w, conv_b, prev)


def _final_norm_kernel(x_ref, g_ref, o_ref):
    o_ref[...] = _rms(x_ref[...], g_ref[...], NORM_EPS)


def _final_norm(x, gain):
    m, d = x.shape
    tm = _tile(m, ROW_TILE)
    return pl.pallas_call(
        _final_norm_kernel, grid=(m // tm,),
        in_specs=[pl.BlockSpec((tm, d), lambda i: (i, 0)), pl.BlockSpec((1, d), lambda i: (0, 0))],
        out_specs=pl.BlockSpec((tm, d), lambda i: (i, 0)),
        out_shape=jax.ShapeDtypeStruct((m, d), F32),
        compiler_params=_cparams(1), name="final_norm",
    )(x, gain)


def _conv_ffn(y, seq, prev_state, gain, wg, wu, conv_w, conv_b, wd):
    n_seq = y.shape[0] // seq
    d_ff = wg.shape[1]
    prev = jnp.concatenate([jnp.zeros((n_seq, SUBLANES - (CONV_W - 1), d_ff), F32), prev_state], axis=1)
    act, tails = _ffn_up(y, gain, wg, wu, conv_w, conv_b, prev, seq=seq)
    y = _mm_res(act, wd, y)
    state = tails.reshape(n_seq, -1, SUBLANES, d_ff)[:, -1, SUBLANES - (CONV_W - 1):, :]
    return y, state


def kernel(x_prompt, x_sample, cache_mla_ckv, cache_mla_kpe, cache_swa_k, cache_swa_v, cache_diff_k,
           cache_diff_v, state_ffn_conv, norm_mix, norm_ffn, final_norm, mla_w_dq, mla_q_norm, mla_w_uq,
           mla_w_dkv, mla_kv_norm, mla_w_ukv, mla_w_o, swa_w_qkv, swa_sinks, swa_w_o, diff_w_qkv,
           diff_lambda_q1, diff_lambda_k1, diff_lambda_q2, diff_lambda_k2, diff_subln, diff_w_o,
           ffn_w_gate, ffn_w_up, ffn_conv_w, ffn_conv_b, ffn_w_down):
    bp, sp, d_model = x_prompt.shape
    bs, ss, _ = x_sample.shape
    depth = norm_mix.shape[0]
    past = cache_mla_ckv.shape[2]
    d_ff = ffn_w_gate.shape[2]
    q_rank = mla_w_dq.shape[2]
    kv_rank = mla_w_ukv.shape[1]

    yp = x_prompt.reshape(bp * sp, d_model)
    ys = x_sample.reshape(bs * ss, d_model)
    row_p = _tile(bp * sp, ROW_TILE)
    row_s = _tile(bs * ss, ROW_TILE)

    def tables(head_dim):
        return (_rope_tables(sp, 0, head_dim, row_p), _rope_tables(ss, past, head_dim, row_s))

    tab64, tab128 = tables(64), tables(128)
    outs_p = {k: [] for k in ("ckv", "kpe", "swk", "swv", "dk", "dv", "conv")}
    outs_s = {k: [] for k in ("ckv", "kpe", "swk", "swv", "dk", "dv", "conv")}

    for i in range(depth):
        kind, j = i % N_MIXERS, i // N_MIXERS
        gain = norm_mix[i][None, :]
        if kind == 0:
            q_scale = (MLA_NOPE + MLA_ROPE) ** -0.5 * LOG2E
            pad = (-(q_rank + kv_rank + MLA_ROPE)) % LANES
            w_cat = jnp.concatenate([mla_w_dq[j], mla_w_dkv[j], jnp.zeros((d_model, pad), F32)],
                                    axis=1).astype(BF16)
            w_uq = mla_w_uq[j].reshape(q_rank, MLA_HEADS, MLA_NOPE + MLA_ROPE)
            w_uq = jnp.pad(w_uq, ((0, 0), (0, 0), (0, MLA_QK_PAD - MLA_NOPE - MLA_ROPE)))
            w_uq = w_uq.reshape(q_rank, MLA_HEADS * MLA_QK_PAD).astype(BF16)
            w_ukv = mla_w_ukv[j].reshape(kv_rank, MLA_HEADS, MLA_NOPE + MLA_V)
            wk = w_ukv[:, :, :MLA_NOPE].reshape(kv_rank, MLA_HEADS * MLA_NOPE).astype(BF16)
            wv = w_ukv[:, :, MLA_NOPE:].reshape(kv_rank, MLA_HEADS * MLA_V).astype(BF16)
            w_o = mla_w_o[j].astype(BF16)
            qn, kvn = mla_q_norm[j][None, :], mla_kv_norm[j][None, :]

            (cos_p, sin_p), (cos_s, sin_s) = tab64
            qa, ckv_p, kpe_p = _mla_down(yp, gain, w_cat, qn, kvn, cos_p, sin_p, q_rank, kv_rank)
            q = _mla_q_up(qa, w_uq, cos_p, sin_p, q_scale)
            k, vt = _mla_expand(ckv_p, kpe_p, wk, wv.T)
            o = _flash_attention(q, k, vt, batch=bp, seq=sp, n_kv_heads=MLA_HEADS, dk=MLA_QK_PAD,
                                 dv=MLA_V, groups=1, n_maps=1)
            yp = _mm_res(o, w_o, yp)

            qa, ckv_s, kpe_s = _mla_down(ys, gain, w_cat, qn, kvn, cos_s, sin_s, q_rank, kv_rank)
            q = _mla_q_up(qa, w_uq, cos_s, sin_s, q_scale)
            o = _mla_decode_attention(q, cache_mla_ckv.reshape(-1, kv_rank), cache_mla_kpe.reshape(-1, MLA_ROPE),
                                      ckv_s, kpe_s, wk, wv, layer=j, batch=bs, past=past, new=ss)
            ys = _mm_res(o, w_o, ys)
            outs_p["ckv"].append(ckv_p.reshape(bp, sp, kv_rank))
            outs_p["kpe"].append(kpe_p.reshape(bp, sp, MLA_ROPE))
            outs_s["ckv"].append(ckv_s.reshape(bs, ss, kv_rank))
            outs_s["kpe"].append(kpe_s.reshape(bs, ss, MLA_ROPE))
        elif kind == 1:
            qw, kw = SWA_HEADS * SWA_HEAD_DIM, SWA_KV_HEADS * SWA_HEAD_DIM
            w_qkv = swa_w_qkv[j].astype(BF16)
            w_o = swa_w_o[j].astype(BF16)
            sinks = swa_sinks[j]
            (cos_p, sin_p), (cos_s, sin_s) = tab64
            q_scale = SWA_HEAD_DIM ** -0.5 * LOG2E
            q, k, v, kb, vt = _qkv_proj(yp, gain, w_qkv, cos_p, sin_p, qw, kw, kw, SWA_HEAD_DIM,
                                        q_scale=q_scale, vt_head=(SWA_HEAD_DIM, LANES - SWA_HEAD_DIM),
                                        vt_blocked=False)
            o = _swa_attention(q, kb, vt, sinks, seq=sp)
            yp = _mm_res(o, w_o, yp)
            k3 = k.reshape(bp, sp, SWA_KV_HEADS, SWA_HEAD_DIM)
            v3 = v.reshape(bp, sp, SWA_KV_HEADS, SWA_HEAD_DIM)
            outs_p["swk"].append(k3[:, sp - WINDOW:])
            outs_p["swv"].append(v3[:, sp - WINDOW:])

            q, k, v, _, _ = _qkv_proj(ys, gain, w_qkv, cos_s, sin_s, qw, kw, kw, SWA_HEAD_DIM, q_scale=q_scale)
            kc = cache_swa_k[j].reshape(bs * WINDOW, kw)
            vc = cache_swa_v[j].reshape(bs * WINDOW, kw)
            o = _swa_decode_attention(q, kc, k, vc, v, sinks, nq=ss, na=WINDOW, nb=ss)
            ys = _mm_res(o, w_o, ys)
            k_all = jnp.concatenate([cache_swa_k[j], k.reshape(bs, ss, SWA_KV_HEADS, SWA_HEAD_DIM)], axis=1)
            v_all = jnp.concatenate([cache_swa_v[j], v.reshape(bs, ss, SWA_KV_HEADS, SWA_HEAD_DIM)], axis=1)
            outs_s["swk"].append(k_all[:, ss:])
            outs_s["swv"].append(v_all[:, ss:])
        else:
            lam_init = 0.8 - 0.6 * math.exp(-0.3 * i)
            q_scale = DIFF_HEAD_DIM ** -0.5 * LOG2E
            groups = DIFF_HEADS // DIFF_KV_HEADS
            qw = DIFF_HEADS * 2 * DIFF_HEAD_DIM
            kw = DIFF_KV_HEADS * 2 * DIFF_HEAD_DIM
            w_qkv = diff_w_qkv[j].astype(BF16)
            w_o = diff_w_o[j].astype(BF16)
            lam = jnp.stack([diff_lambda_q1[j], diff_lambda_k1[j], diff_lambda_q2[j], diff_lambda_k2[j]])
            subln = diff_subln[j][None, :]
            common = dict(n_kv_heads=DIFF_KV_HEADS, dk=DIFF_HEAD_DIM, dv=2 * DIFF_HEAD_DIM, groups=groups,
                          n_maps=2, lam=lam, subln=subln, lam_init=lam_init)
            (cos_p, sin_p), (cos_s, sin_s) = tab128
            q, k, v, kb, vt = _qkv_proj(yp, gain, w_qkv, cos_p, sin_p, qw, kw, kw, DIFF_HEAD_DIM,
                                        q_scale=q_scale, vt_head=(2 * DIFF_HEAD_DIM, ONES_ROWS))
            o = _flash_attention(q, kb, vt, batch=bp, seq=sp, **common)
            yp = _mm_res(o, w_o, yp)
            outs_p["dk"].append(k.reshape(bp, sp, DIFF_KV_HEADS, 2, DIFF_HEAD_DIM))
            outs_p["dv"].append(v.reshape(bp, sp, DIFF_KV_HEADS, 2 * DIFF_HEAD_DIM))

            q, k, v, _, _ = _qkv_proj(ys, gain, w_qkv, cos_s, sin_s, qw, kw, kw, DIFF_HEAD_DIM,
                                      q_scale=q_scale)
            o = _decode_attention(q, cache_diff_k[j].reshape(bs * past, kw),
                                  cache_diff_v[j].reshape(bs * past, kw), k, v,
                                  batch=bs, past=past, new=ss, **common)
            ys = _mm_res(o, w_o, ys)
            outs_s["dk"].append(k.reshape(bs, ss, DIFF_KV_HEADS, 2, DIFF_HEAD_DIM))
            outs_s["dv"].append(v.reshape(bs, ss, DIFF_KV_HEADS, 2 * DIFF_HEAD_DIM))

        ffn_gain = norm_ffn[i][None, :]
        wg, wu, wd = ffn_w_gate[i].astype(BF16), ffn_w_up[i].astype(BF16), ffn_w_down[i].astype(BF16)
        cb = ffn_conv_b[i][None, :]
        yp, conv_p = _conv_ffn(yp, sp, jnp.zeros((bp, CONV_W - 1, d_ff), F32), ffn_gain, wg, wu,
                               ffn_conv_w[i], cb, wd)
        ys, conv_s = _conv_ffn(ys, ss, state_ffn_conv[i], ffn_gain, wg, wu, ffn_conv_w[i], cb, wd)
        outs_p["conv"].append(conv_p)
        outs_s["conv"].append(conv_s)

    fg = final_norm[None, :]
    y_prompt = _final_norm(yp, fg).reshape(bp, sp, d_model)
    y_sample = _final_norm(ys, fg).reshape(bs, ss, d_model)
    order = ("ckv", "kpe", "swk", "swv", "dk", "dv", "conv")
    stack = lambda xs: xs[0][None] if len(xs) == 1 else jnp.stack(xs)
    return (y_prompt, y_sample) + tuple(stack(outs_p[k]) for k in order) + tuple(
        stack(outs_s[k]) for k in order)
```

```python
import functools
import math

import jax
import jax.numpy as jnp
from jax import lax
from jax.experimental import pallas as pl
from jax.experimental.pallas import tpu as pltpu

F32 = jnp.float32
BF16 = jnp.bfloat16

CHUNK = 64
ROPE_THETA = 10000.0
NORM_EPS = 1e-6
SUBLN_EPS = 1e-5
N_MIXERS = 3
MLA_HEADS = 16
MLA_NOPE = 128
MLA_ROPE = 64
MLA_V = 128
SWA_HEADS = 32
SWA_KV_HEADS = 4
SWA_HEAD_DIM = 64
WINDOW = 128
DIFF_HEADS = 8
DIFF_KV_HEADS = 4
DIFF_HEAD_DIM = 128
CONV_W = 3

LANES = 128
SUBLANES = 8
MXU_COLS = 256
MLA_QK_PAD = MXU_COLS
VMEM_LIMIT_BYTES = 56 * 1024 * 1024
LOG2E = 1.4426950408889634

ROW_TILE = 512
FFN_ROW_TILE = 1024
FFN_COL_TILE = 512
FFN_SUB_TILE = 256
ATTN_Q_TILE = 2048
ATTN_KV_TILE = 512
ATTN_CHAIN = 256
ATTN_LOOKAHEAD = 3
ONES_ROWS = 16
MLA_DECODE_KV_TILE = 1024
MM_TILE = 512
MM_WEIGHT_BLOCK_BYTES = 12 * 1024 * 1024
MM_SUB_TILE = 256
NORM_SUB_TILE = 128


def _cparams(n_axes):
    return pltpu.CompilerParams(dimension_semantics=("arbitrary",) * n_axes,
                                vmem_limit_bytes=VMEM_LIMIT_BYTES)


def _resident(shape):
    return pl.BlockSpec(shape, lambda *_: (0,) * len(shape), pipeline_mode=pl.Buffered(1))


def _tile(n, pref):
    t = min(n, pref)
    assert n % t == 0, (n, pref)
    return t


def _rms(x, g, eps):
    ms = jnp.mean(x * x, axis=-1, keepdims=True)
    return (x * lax.rsqrt(ms + eps)) * g


def _rope_slab(x, cos, sin, head_dim):
    if head_dim == LANES:
        swapped = pltpu.roll(x, LANES // 2, axis=1)
    else:
        half = head_dim // 2
        lane = lax.broadcasted_iota(jnp.int32, x.shape, 1)
        first = (lane & (head_dim - 1)) < half
        swapped = jnp.where(first, pltpu.roll(x, LANES - half, axis=1), pltpu.roll(x, half, axis=1))
    return x * cos + swapped * sin


def _rope_tables(seq_len, offset, head_dim, rows):
    half = head_dim // 2
    pos = jnp.arange(seq_len, dtype=F32) + offset
    inv = ROPE_THETA ** (-jnp.arange(0, head_dim, 2, dtype=F32) / head_dim)
    ang = pos[:, None] * inv[None, :]
    cos, sin = jnp.cos(ang), jnp.sin(ang)
    reps = LANES // head_dim
    cos_l = jnp.tile(jnp.concatenate([cos, cos], axis=1), (1, reps))
    sin_l = jnp.tile(jnp.concatenate([-sin, sin], axis=1), (1, reps))
    n = max(rows // seq_len, 1)
    return jnp.tile(cos_l, (n, 1)), jnp.tile(sin_l, (n, 1))


def _mla_down_kernel(x_ref, g_ref, w_ref, qn_ref, kvn_ref, cos_ref, sin_ref,
                     qa_ref, ckv_ref, kpe_ref, *, q_rank, kv_rank, sub):
    def project(r0):
        h = _rms(x_ref[r0:r0 + sub, :], g_ref[...], NORM_EPS).astype(BF16)
        return jnp.dot(h, w_ref[...], preferred_element_type=F32)

    starts = list(range(0, x_ref.shape[0], sub))
    pending = project(starts[0])
    for t, r0 in enumerate(starts):
        d = pending
        if t + 1 < len(starts):
            pending = project(starts[t + 1])
        rows = slice(r0, r0 + sub)
        qa_ref[rows, :] = _rms(d[:, :q_rank], qn_ref[...], NORM_EPS).astype(BF16)
        ckv_ref[rows, :] = _rms(d[:, q_rank:q_rank + kv_rank], kvn_ref[...], NORM_EPS)
        slab = d[:, q_rank + kv_rank:q_rank + kv_rank + LANES]
        kpe_ref[rows, :] = _rope_slab(slab, cos_ref[rows, :], sin_ref[rows, :], MLA_ROPE)[:, :MLA_ROPE]


def _mla_down(x, gain, w_cat, q_norm, kv_norm, cos, sin, q_rank, kv_rank):
    m, d_model = x.shape
    tm = _tile(m, ROW_TILE)
    n_tab = cos.shape[0] // tm
    wn = w_cat.shape[1]
    row = lambda i: (i, 0)
    fixed = lambda i: (0, 0)
    tab = lambda i: (i % n_tab, 0)
    return pl.pallas_call(
        functools.partial(_mla_down_kernel, q_rank=q_rank, kv_rank=kv_rank, sub=_tile(tm, NORM_SUB_TILE)),
        grid=(m // tm,),
        in_specs=[pl.BlockSpec((tm, d_model), row), pl.BlockSpec((1, d_model), fixed),
                  _resident((d_model, wn)), pl.BlockSpec((1, q_rank), fixed),
                  pl.BlockSpec((1, kv_rank), fixed), pl.BlockSpec((tm, LANES), tab),
                  pl.BlockSpec((tm, LANES), tab)],
        out_specs=[pl.BlockSpec((tm, q_rank), row), pl.BlockSpec((tm, kv_rank), row),
                   pl.BlockSpec((tm, MLA_ROPE), row)],
        out_shape=[jax.ShapeDtypeStruct((m, q_rank), BF16), jax.ShapeDtypeStruct((m, kv_rank), F32),
                   jax.ShapeDtypeStruct((m, MLA_ROPE), F32)],
        compiler_params=_cparams(1), name="mla_down",
    )(x, gain, w_cat, q_norm, kv_norm, cos, sin)


def _mla_q_up_kernel(qa_ref, w_ref, cos_ref, sin_ref, q_ref, *, q_scale):
    qa = qa_ref[...]
    cos, sin = cos_ref[...], sin_ref[...]
    for h in range(MLA_HEADS):
        c0 = h * MLA_QK_PAD
        d = jnp.dot(qa, w_ref[:, c0:c0 + MLA_QK_PAD], preferred_element_type=F32)
        q_ref[:, c0:c0 + LANES] = (d[:, :LANES] * q_scale).astype(BF16)
        q_ref[:, c0 + LANES:c0 + MLA_QK_PAD] = (_rope_slab(d[:, LANES:], cos, sin, MLA_ROPE) * q_scale).astype(BF16)


def _mla_q_up(qa, w_pad, cos, sin, q_scale):
    m, q_rank = qa.shape
    tm = _tile(m, ROW_TILE)
    n_tab = cos.shape[0] // tm
    n = w_pad.shape[1]
    row = lambda i: (i, 0)
    fixed = lambda i: (0, 0)
    tab = lambda i: (i % n_tab, 0)
    return pl.pallas_call(
        functools.partial(_mla_q_up_kernel, q_scale=q_scale), grid=(m // tm,),
        in_specs=[pl.BlockSpec((tm, q_rank), row), _resident((q_rank, n)),
                  pl.BlockSpec((tm, LANES), tab), pl.BlockSpec((tm, LANES), tab)],
        out_specs=pl.BlockSpec((tm, n), row),
        out_shape=jax.ShapeDtypeStruct((m, n), BF16),
        compiler_params=_cparams(1), name="mla_q_up",
    )(qa, w_pad, cos, sin)


def _mla_expand_kernel(ckv_ref, kpe_ref, wk_ref, wvt_ref, k_ref, vt_ref):
    c = ckv_ref[...].astype(BF16)
    kpe = kpe_ref[...].astype(BF16)
    tm = c.shape[0]
    dva = MLA_V + ONES_ROWS
    ones = jnp.ones((ONES_ROWS, tm), BF16)
    zeros = jnp.zeros((tm, MLA_QK_PAD - MLA_NOPE - MLA_ROPE), BF16)
    pair = MXU_COLS // MLA_NOPE

    def project(h0):
        kn = jnp.dot(c, wk_ref[:, h0 * MLA_NOPE:(h0 + pair) * MLA_NOPE], preferred_element_type=F32)
        vt = lax.dot_general(wvt_ref[h0 * MLA_V:(h0 + pair) * MLA_V, :], c, (((1,), (1,)), ((), ())),
                             preferred_element_type=F32)
        return kn, vt

    pending = project(0)
    for h0 in range(0, MLA_HEADS, pair):
        kn, vt = pending
        if h0 + pair < MLA_HEADS:
            pending = project(h0 + pair)
        for i in range(pair):
            h = h0 + i
            c0 = h * MLA_QK_PAD
            k_ref[:, c0:c0 + MLA_NOPE] = kn[:, i * MLA_NOPE:(i + 1) * MLA_NOPE].astype(BF16)
            k_ref[:, c0 + MLA_NOPE:c0 + MLA_NOPE + MLA_ROPE] = kpe
            k_ref[:, c0 + MLA_NOPE + MLA_ROPE:c0 + MLA_QK_PAD] = zeros
            vt_ref[0, h * dva:h * dva + MLA_V, :] = vt[i * MLA_V:(i + 1) * MLA_V].astype(BF16)
            vt_ref[0, h * dva + MLA_V:(h + 1) * dva, :] = ones


def _mla_expand(ckv, kpe, wk, wvt):
    m, kv_rank = ckv.shape
    tm = _tile(m, ATTN_KV_TILE)
    row = lambda i: (i, 0)
    nk, nva = MLA_HEADS * MLA_QK_PAD, MLA_HEADS * (MLA_V + ONES_ROWS)
    return pl.pallas_call(
        _mla_expand_kernel, grid=(m // tm,),
        in_specs=[pl.BlockSpec((tm, kv_rank), row), pl.BlockSpec((tm, MLA_ROPE), row),
                  _resident(wk.shape), _resident(wvt.shape)],
        out_specs=[pl.BlockSpec((tm, nk), row), pl.BlockSpec((1, nva, tm), lambda i: (i, 0, 0))],
        out_shape=[jax.ShapeDtypeStruct((m, nk), BF16), jax.ShapeDtypeStruct((m // tm, nva, tm), BF16)],
        compiler_params=_cparams(1), name="mla_expand",
    )(ckv, kpe, wk, wvt)


def _qkv_proj_kernel(x_ref, g_ref, w_ref, cos_ref, sin_ref, q_ref, k_ref, v_ref, kb_ref, vb_ref,
                     *, qw, kw, vw, head_dim, q_scale, vt_head):
    h = _rms(x_ref[...], g_ref[...], NORM_EPS).astype(BF16)
    cos, sin = cos_ref[...], sin_ref[...]
    vt_dst = None
    if vt_head:
        vt_dst = vb_ref.at[0] if len(vb_ref.shape) == 3 else vb_ref
    for n0 in range(0, qw + kw + vw, MXU_COLS):
        dd = jnp.dot(h, w_ref[:, n0:n0 + MXU_COLS], preferred_element_type=F32)
        for c0 in range(n0, n0 + MXU_COLS, LANES):
            d = dd[:, c0 - n0:c0 - n0 + LANES]
            if c0 < qw:
                q_ref[:, c0:c0 + LANES] = (_rope_slab(d, cos, sin, head_dim) * q_scale).astype(BF16)
            elif c0 < qw + kw:
                r = _rope_slab(d, cos, sin, head_dim)
                k_ref[:, c0 - qw:c0 - qw + LANES] = r
                kb_ref[:, c0 - qw:c0 - qw + LANES] = r.astype(BF16)
            else:
                c1 = c0 - qw - kw
                v_ref[:, c1:c1 + LANES] = d
                if vt_head:
                    vd, ones_rows = vt_head
                    dt = d.T.astype(BF16)
                    for hv in range(c1 // vd, (c1 + LANES - 1) // vd + 1):
                        lo, hi = max(c1, hv * vd), min(c1 + LANES, (hv + 1) * vd)
                        r0 = hv * (vd + ones_rows) + lo - hv * vd
                        vt_dst[r0:r0 + hi - lo, :] = dt[lo - c1:hi - c1]
                        if hi == (hv + 1) * vd:
                            vt_dst[r0 + hi - lo:r0 + hi - lo + ones_rows, :] = jnp.ones((ones_rows, d.shape[0]), BF16)
                else:
                    vb_ref[:, c1:c1 + LANES] = d.astype(BF16)


def _qkv_proj(x, gain, w, cos, sin, qw, kw, vw, head_dim, *, q_scale=1.0, vt_head=None, vt_blocked=True):
    m, d_model = x.shape
    tm = _tile(m, ATTN_KV_TILE if vt_head else ROW_TILE)
    n_tab = cos.shape[0] // tm
    row = lambda i: (i, 0)
    fixed = lambda i: (0, 0)
    tab = lambda i: (i % n_tab, 0)
    if vt_head:
        vwa = vw // vt_head[0] * (vt_head[0] + vt_head[1])
        if vt_blocked:
            vb_spec = pl.BlockSpec((1, vwa, tm), lambda i: (i, 0, 0))
            vb_shape = jax.ShapeDtypeStruct((m // tm, vwa, tm), BF16)
        else:
            vb_spec = pl.BlockSpec((vwa, tm), lambda i: (0, i))
            vb_shape = jax.ShapeDtypeStruct((vwa, m), BF16)
    else:
        vb_spec = pl.BlockSpec((tm, vw), row)
        vb_shape = jax.ShapeDtypeStruct((m, vw), BF16)
    return pl.pallas_call(
        functools.partial(_qkv_proj_kernel, qw=qw, kw=kw, vw=vw, head_dim=head_dim, q_scale=q_scale,
                          vt_head=vt_head),
        grid=(m // tm,),
        in_specs=[pl.BlockSpec((tm, d_model), row), pl.BlockSpec((1, d_model), fixed),
                  _resident(w.shape), pl.BlockSpec((tm, LANES), tab),
                  pl.BlockSpec((tm, LANES), tab)],
        out_specs=[pl.BlockSpec((tm, qw), row), pl.BlockSpec((tm, kw), row), pl.BlockSpec((tm, vw), row),
                   pl.BlockSpec((tm, kw), row), vb_spec],
        out_shape=[jax.ShapeDtypeStruct((m, qw), BF16), jax.ShapeDtypeStruct((m, kw), F32),
                   jax.ShapeDtypeStruct((m, vw), F32), jax.ShapeDtypeStruct((m, kw), BF16), vb_shape],
        compiler_params=_cparams(1), name="qkv_proj",
    )(x, gain, w, cos, sin)


def _mm_res_kernel(*refs, sub, with_norm):
    if with_norm:
        a_ref, w_ref, r_ref, g_ref, o_ref, h_ref = refs
    else:
        a_ref, w_ref, r_ref, o_ref = refs
    starts = list(range(0, a_ref.shape[0], sub))
    pending = jnp.dot(a_ref[0:sub, :], w_ref[...], preferred_element_type=F32)
    for t, r0 in enumerate(starts):
        d = pending
        if t + 1 < len(starts):
            r1 = starts[t + 1]
            pending = jnp.dot(a_ref[r1:r1 + sub, :], w_ref[...], preferred_element_type=F32)
        y = r_ref[r0:r0 + sub, :] + d
        o_ref[r0:r0 + sub, :] = y
        if with_norm:
            h_ref[r0:r0 + sub, :] = _rms(y, g_ref[...], NORM_EPS).astype(BF16)


def _mm_res(a, w, res, norm_gain=None):
    m, k = a.shape
    n = w.shape[1]
    tm = _tile(m, MM_TILE)
    tn = n
    while tn * k * w.dtype.itemsize > MM_WEIGHT_BLOCK_BYTES and tn % (2 * MXU_COLS) == 0:
        tn //= 2
    with_norm = norm_gain is not None
    assert not with_norm or tn == n
    tile = pl.BlockSpec((tm, tn), lambda j, i: (i, j))
    in_specs = [pl.BlockSpec((tm, k), lambda j, i: (i, 0)), pl.BlockSpec((k, tn), lambda j, i: (0, j)), tile]
    out_specs, out_shape, args = tile, jax.ShapeDtypeStruct((m, n), F32), [a, w, res]
    if with_norm:
        in_specs.append(pl.BlockSpec((1, n), lambda j, i: (0, 0)))
        out_specs, out_shape = [tile, tile], [out_shape, jax.ShapeDtypeStruct((m, n), BF16)]
        args.append(norm_gain)
    return pl.pallas_call(
        functools.partial(_mm_res_kernel, sub=_tile(tm, MM_SUB_TILE), with_norm=with_norm),
        grid=(n // tn, m // tm), in_specs=in_specs, out_specs=out_specs, out_shape=out_shape,
        compiler_params=_cparams(2), name="mm_res",
    )(*args)


def _diff_lambda(lam_ref, lam_init):
    lam = lam_ref[...]
    a = jnp.sum(lam[0:1] * lam[1:2], axis=-1, keepdims=True)
    b = jnp.sum(lam[2:3] * lam[3:4], axis=-1, keepdims=True)
    return jnp.exp(a) - jnp.exp(b) + lam_init


def _combine_heads(o_list, o_ref, rows, lam_ref, subln_ref, *, n_maps, dv, lam_init):
    groups = len(o_list) // n_maps
    for g in range(groups):
        if n_maps == 1:
            o = o_list[g]
        else:
            lam = _diff_lambda(lam_ref, lam_init)
            o = o_list[2 * g] - lam * o_list[2 * g + 1]
            o = _rms(o, subln_ref[...], SUBLN_EPS) * (1.0 - lam_init)
        o_ref[rows, g * dv:(g + 1) * dv] = o.astype(o_ref.dtype)


def _flash_kernel(*refs, tq, tk, cw, dk, dv, groups, n_maps, lam_init):
    if n_maps == 2:
        lam_ref, subln_ref, q_ref, k_ref, vt_ref, o_ref, m_ref, acc_ref, st_ref, mx_ref = refs
    else:
        q_ref, k_ref, vt_ref, o_ref, m_ref, acc_ref, st_ref, mx_ref = refs
        lam_ref = subln_ref = None
    n_sub = groups * n_maps
    n_chain = tq // cw
    kv_per_q = tq // tk
    qi = pl.program_id(2)
    nt = (((1,), (1,)), ((), ()))

    m_ref[...] = jnp.full(m_ref.shape, -jnp.inf, F32)
    acc_ref[...] = jnp.zeros(acc_ref.shape, F32)

    def scores(item, k_blks, slot):
        u, n, k_lo, bi = item
        mi = u % n_maps
        q_n = q_ref[n * cw:(n + 1) * cw, u * dk:(u + 1) * dk]
        st = lax.dot_general(k_blks[bi][:, mi * dk:(mi + 1) * dk], q_n, nt, preferred_element_type=F32)
        if k_lo is not None:
            kc = (lax.broadcasted_iota(jnp.int32, st.shape, 0) + k_lo) // CHUNK
            qc = (lax.broadcasted_iota(jnp.int32, st.shape, 1) + n * cw) // CHUNK
            st = jnp.where(kc <= qc, st, -jnp.inf)
        st_ref[slot] = st
        mx_ref[slot] = jnp.max(st, axis=0, keepdims=True)

    def update(item, slot, vt_blks):
        u, n, _, bi = item
        st = st_ref[slot]
        idx = u * n_chain + n
        m_prev = m_ref[idx]
        m_new = jnp.maximum(m_prev, mx_ref[slot])
        alpha = jnp.exp2(m_prev - m_new)
        pt = jnp.exp2(st - m_new).astype(BF16)
        acc_ref[idx] = acc_ref[idx] * alpha + jnp.dot(vt_blks[bi], pt, preferred_element_type=F32)
        m_ref[idx] = m_new

    def run_chains(items, first_block):
        blocks = sorted({it[3] for it in items})
        k_blks = {bi: k_ref[pl.ds(pl.multiple_of((first_block + bi) * tk, tk), tk), :] for bi in blocks}
        vt_blks = {bi: vt_ref[first_block + bi] for bi in blocks}
        n_slots = ATTN_LOOKAHEAD + 1
        for t in range(min(ATTN_LOOKAHEAD, len(items))):
            scores(items[t], k_blks, t % n_slots)
        for t, item in enumerate(items):
            if t + ATTN_LOOKAHEAD < len(items):
                scores(items[t + ATTN_LOOKAHEAD], k_blks, (t + ATTN_LOOKAHEAD) % n_slots)
            update(item, t % n_slots, vt_blks)

    def body(j, carry):
        items = [(u, n, None, bi) for bi in range(kv_per_q) for n in range(n_chain) for u in range(n_sub)]
        run_chains(items, j * kv_per_q)
        return carry

    lax.fori_loop(0, qi, body, 0)

    items = [(u, n, bi * tk if (bi + 1) * tk > n * cw else None, bi)
             for bi in range(kv_per_q) for n in range(n_chain) if bi * tk < (n + 1) * cw
             for u in range(n_sub)]
    run_chains(items, qi * kv_per_q)

    for n in range(n_chain):
        outs = []
        for u in range(n_sub):
            a = acc_ref[u * n_chain + n]
            outs.append((a[:dv] / a[dv:dv + 1]).T)
        _combine_heads(outs, o_ref, slice(n * cw, (n + 1) * cw), lam_ref, subln_ref,
                       n_maps=n_maps, dv=dv, lam_init=lam_init)


def _flash_attention(q, k, vt, *, batch, seq, n_kv_heads, dk, dv, groups, n_maps,
                     lam=None, subln=None, lam_init=0.0):
    tk = vt.shape[2]
    dva = dv + ONES_ROWS
    n_sub = groups * n_maps
    tq = _tile(seq, max(ATTN_Q_TILE if n_sub == 1 else ATTN_Q_TILE // 2, tk))
    cw = _tile(tq, ATTN_CHAIN)
    assert tq % tk == 0 and tk % CHUNK == 0 and cw % CHUNK == 0
    nq, nk = seq // tq, seq // tk
    n_chain = tq // cw
    kern = functools.partial(_flash_kernel, tq=tq, tk=tk, cw=cw, dk=dk, dv=dv, groups=groups,
                             n_maps=n_maps, lam_init=lam_init)
    in_specs = [pl.BlockSpec((tq, n_sub * dk), lambda b, h, i: (b * nq + i, h)),
                pl.BlockSpec((seq, n_maps * dk), lambda b, h, i: (b, h)),
                pl.BlockSpec((nk, dva, tk), lambda b, h, i: (b, h, 0))]
    args = [q, k, vt]
    if n_maps == 2:
        in_specs = [pl.BlockSpec(lam.shape, lambda b, h, i: (0, 0)),
                    pl.BlockSpec(subln.shape, lambda b, h, i: (0, 0))] + in_specs
        args = [lam, subln] + args
    return pl.pallas_call(
        kern, grid=(batch, n_kv_heads, nq), in_specs=in_specs,
        out_specs=pl.BlockSpec((tq, groups * dv), lambda b, h, i: (b * nq + i, h)),
        out_shape=jax.ShapeDtypeStruct((batch * seq, n_kv_heads * groups * dv), BF16),
        scratch_shapes=[pltpu.VMEM((n_sub * n_chain, 1, cw), F32),
                        pltpu.VMEM((n_sub * n_chain, dva, cw), F32),
                        pltpu.VMEM((ATTN_LOOKAHEAD + 1, tk, cw), F32),
                        pltpu.VMEM((ATTN_LOOKAHEAD + 1, 1, cw), F32)],
        compiler_params=_cparams(3), name="flash_attention",
    )(*args)


def _decode_kernel(*refs, dk, dv, groups, n_maps, lam_init):
    if n_maps == 2:
        lam_ref, subln_ref, q_ref, kc_ref, vc_ref, kn_ref, vn_ref, o_ref = refs
    else:
        q_ref, kc_ref, vc_ref, kn_ref, vn_ref, o_ref = refs
        lam_ref = subln_ref = None
    nt = (((1,), (1,)), ((), ()))
    vc = vc_ref[...].astype(BF16)
    vn = vn_ref[...].astype(BF16)
    outs = []
    for u in range(groups * n_maps):
        mi = u % n_maps
        q = q_ref[:, u * dk:(u + 1) * dk]
        kc = kc_ref[:, mi * dk:(mi + 1) * dk].astype(BF16)
        kn = kn_ref[:, mi * dk:(mi + 1) * dk].astype(BF16)
        s1 = lax.dot_general(q, kc, nt, preferred_element_type=F32)
        s2 = lax.dot_general(q, kn, nt, preferred_element_type=F32)
        m = jnp.maximum(jnp.max(s1, axis=1, keepdims=True), jnp.max(s2, axis=1, keepdims=True))
        p1 = jnp.exp2(s1 - m)
        p2 = jnp.exp2(s2 - m)
        l = jnp.sum(p1, axis=1, keepdims=True) + jnp.sum(p2, axis=1, keepdims=True)
        o = (jnp.dot(p1.astype(BF16), vc, preferred_element_type=F32)
             + jnp.dot(p2.astype(BF16), vn, preferred_element_type=F32))
        outs.append(o / l)
    _combine_heads(outs, o_ref, slice(None), lam_ref, subln_ref, n_maps=n_maps, dv=dv, lam_init=lam_init)


def _decode_attention(q, kc, vc, kn, vn, *, batch, past, new, n_kv_heads, dk, dv, groups, n_maps,
                      lam=None, subln=None, lam_init=0.0):
    n_sub = groups * n_maps
    kern = functools.partial(_decode_kernel, dk=dk, dv=dv, groups=groups, n_maps=n_maps, lam_init=lam_init)
    bh = lambda b, h: (b, h)
    in_specs = [pl.BlockSpec((new, n_sub * dk), bh), pl.BlockSpec((past, n_maps * dk), bh),
                pl.BlockSpec((past, dv), bh), pl.BlockSpec((new, n_maps * dk), bh),
                pl.BlockSpec((new, dv), bh)]
    args = [q, kc, vc, kn, vn]
    if n_maps == 2:
        in_specs = [pl.BlockSpec(lam.shape, lambda b, h: (0, 0)),
                    pl.BlockSpec(subln.shape, lambda b, h: (0, 0))] + in_specs
        args = [lam, subln] + args
    return pl.pallas_call(
        kern, grid=(batch, n_kv_heads), in_specs=in_specs,
        out_specs=pl.BlockSpec((new, groups * dv), bh),
        out_shape=jax.ShapeDtypeStruct((batch * new, n_kv_heads * groups * dv), BF16),
        compiler_params=_cparams(2), name="decode_attention",
    )(*args)


def _mla_decode_kernel(q_ref, ckv_c_ref, kpe_c_ref, ckv_n_ref, kpe_n_ref, wk_ref, wv_ref, o_ref,
                       kcat_ref, qcat_ref, *, block):
    new = q_ref.shape[0]
    past, rank = ckv_c_ref.shape
    nt = (((1,), (1,)), ((), ()))
    kcat_ref[0:past, 0:rank] = ckv_c_ref[...].astype(BF16)
    kcat_ref[past:past + new, 0:rank] = ckv_n_ref[...].astype(BF16)
    kcat_ref[0:past, rank:rank + MLA_ROPE] = kpe_c_ref[...].astype(BF16)
    kcat_ref[past:past + new, rank:rank + MLA_ROPE] = kpe_n_ref[...].astype(BF16)
    kcat_ref[:, rank + MLA_ROPE:rank + LANES] = jnp.zeros((past + new, LANES - MLA_ROPE), BF16)
    for h in range(MLA_HEADS):
        c0 = h * MLA_QK_PAD
        q_lat = lax.dot_general(q_ref[:, c0:c0 + MLA_NOPE], wk_ref[:, h * MLA_NOPE:(h + 1) * MLA_NOPE], nt,
                                preferred_element_type=F32)
        qcat_ref[h * new:(h + 1) * new, 0:rank] = q_lat.astype(BF16)
        qcat_ref[h * new:(h + 1) * new, rank:rank + LANES] = q_ref[:, c0 + MLA_NOPE:c0 + MLA_QK_PAD]
    q = qcat_ref[...]
    rows = MLA_HEADS * new
    m = jnp.full((rows, 1), -jnp.inf, F32)
    l = jnp.zeros((rows, 1), F32)
    acc = jnp.zeros((rows, rank), F32)
    starts = list(range(0, past, block)) + [past]
    for k0 in starts:
        nk = min(block, past - k0) if k0 < past else new
        kb = kcat_ref[k0:k0 + nk, :]
        s = lax.dot_general(q, kb, nt, preferred_element_type=F32)
        m_new = jnp.maximum(m, jnp.max(s, axis=1, keepdims=True))
        alpha = jnp.exp2(m - m_new)
        p = jnp.exp2(s - m_new)
        l = alpha * l + jnp.sum(p, axis=1, keepdims=True)
        acc = alpha * acc + jnp.dot(p.astype(BF16), kb[:, 0:rank], preferred_element_type=F32)
        m = m_new
    o_lat = (acc / l).astype(BF16)
    for h in range(MLA_HEADS):
        o_ref[:, h * MLA_V:(h + 1) * MLA_V] = jnp.dot(
            o_lat[h * new:(h + 1) * new], wv_ref[:, h * MLA_V:(h + 1) * MLA_V],
            preferred_element_type=F32).astype(o_ref.dtype)


def _mla_decode_attention(q, ckv_cache, kpe_cache, ckv_new, kpe_new, wk, wv, *, layer, batch, past, new):
    rank = ckv_cache.shape[1]
    per = lambda b: (b, 0)
    cached = lambda b: (layer * batch + b, 0)
    return pl.pallas_call(
        functools.partial(_mla_decode_kernel, block=_tile(past, MLA_DECODE_KV_TILE)), grid=(batch,),
        in_specs=[pl.BlockSpec((new, q.shape[1]), per), pl.BlockSpec((past, rank), cached),
                  pl.BlockSpec((past, MLA_ROPE), cached), pl.BlockSpec((new, rank), per),
                  pl.BlockSpec((new, MLA_ROPE), per), _resident(wk.shape), _resident(wv.shape)],
        out_specs=pl.BlockSpec((new, MLA_HEADS * MLA_V), per),
        out_shape=jax.ShapeDtypeStruct((batch * new, MLA_HEADS * MLA_V), BF16),
        scratch_shapes=[pltpu.VMEM((past + new, rank + LANES), BF16),
                        pltpu.VMEM((MLA_HEADS * new, rank + LANES), BF16)],
        compiler_params=_cparams(1), name="mla_decode_attention",
    )(q, ckv_cache, kpe_cache, ckv_new, kpe_new, wk, wv)


def _swa_decode_kernel(sinks_ref, q_ref, ka_ref, kb_ref, va_ref, vb_ref, o_ref):
    nq = q_ref.shape[0]
    group = SWA_HEADS // SWA_KV_HEADS
    d = SWA_HEAD_DIM
    nt = (((1,), (1,)), ((), ()))
    for h in range(SWA_KV_HEADS):
        hs = slice(h * d, (h + 1) * d)
        k = jnp.concatenate([ka_ref[:, hs], kb_ref[:, hs]], axis=0).astype(BF16)
        v = jnp.concatenate([va_ref[:, hs], vb_ref[:, hs]], axis=0).astype(BF16)
        q = jnp.concatenate([q_ref[:, (h * group + g) * d:(h * group + g + 1) * d] for g in range(group)],
                            axis=0)
        sink = jnp.concatenate([jnp.full((nq, 1), sinks_ref[h * group + g] * LOG2E, F32)
                                for g in range(group)], axis=0)
        s = lax.dot_general(q, k, nt, preferred_element_type=F32)
        m = jnp.maximum(jnp.max(s, axis=1, keepdims=True), sink)
        p = jnp.exp2(s - m)
        l = jnp.sum(p, axis=1, keepdims=True) + jnp.exp2(sink - m)
        o = jnp.dot(p.astype(BF16), v, preferred_element_type=F32) / l
        for g in range(0, group, 2):
            pair = jnp.concatenate([o[g * nq:(g + 1) * nq], o[(g + 1) * nq:(g + 2) * nq]], axis=1)
            c0 = (h * group + g) * d
            o_ref[:, c0:c0 + 2 * d] = pair.astype(o_ref.dtype)


def _swa_decode_attention(q, ka, kb, va, vb, sinks, *, nq, na, nb):
    m, qw = q.shape
    kvw = ka.shape[1]
    cur = lambda t: (t, 0)
    return pl.pallas_call(
        _swa_decode_kernel, grid=(m // nq,),
        in_specs=[pl.BlockSpec(memory_space=pltpu.SMEM), pl.BlockSpec((nq, qw), cur),
                  pl.BlockSpec((na, kvw), cur), pl.BlockSpec((nb, kvw), cur),
                  pl.BlockSpec((na, kvw), cur), pl.BlockSpec((nb, kvw), cur)],
        out_specs=pl.BlockSpec((nq, qw), cur),
        out_shape=jax.ShapeDtypeStruct((m, qw), BF16),
        compiler_params=_cparams(1), name="swa_decode_attention",
    )(sinks, q, ka, kb, va, vb)


def _swa_kernel(sinks_ref, q_ref, ka_ref, kb_ref, vta_ref, vtb_ref, o_ref, st_ref, *, blocks_per_seq):
    nq, na = q_ref.shape[0], ka_ref.shape[0]
    nk = na + kb_ref.shape[0]
    group = SWA_HEADS // SWA_KV_HEADS
    d = SWA_HEAD_DIM
    vrows = vta_ref.shape[0] // SWA_KV_HEADS
    nt = (((1,), (1,)), ((), ()))
    kc = lax.broadcasted_iota(jnp.int32, (nk, group * nq), 0) // CHUNK
    qc = (lax.broadcasted_iota(jnp.int32, (nk, group * nq), 1) % nq) // CHUNK
    first = (pl.program_id(0) % blocks_per_seq) == 0
    lo = jnp.where(first, na // CHUNK, 0)
    valid = (kc >= qc) & (kc <= qc + na // CHUNK) & (kc >= lo)

    def scores(h):
        hs = slice(h * d, (h + 1) * d)
        k = jnp.concatenate([ka_ref[:, hs], kb_ref[:, hs]], axis=0)
        q = jnp.concatenate([q_ref[:, (h * group + g) * d:(h * group + g + 1) * d] for g in range(group)],
                            axis=0)
        st = lax.dot_general(k, q, nt, preferred_element_type=F32)
        st_ref[h % 2] = jnp.where(valid, st, -jnp.inf)

    scores(0)
    for h in range(SWA_KV_HEADS):
        if h + 1 < SWA_KV_HEADS:
            scores(h + 1)
        st = st_ref[h % 2]
        sink = jnp.concatenate([jnp.full((1, nq), sinks_ref[h * group + g] * LOG2E, F32)
                                for g in range(group)], axis=1)
        m = jnp.maximum(jnp.max(st, axis=0, keepdims=True), sink)
        pt = jnp.exp2(st - m).astype(BF16)
        vt = jnp.concatenate([vta_ref[h * vrows:(h + 1) * vrows, :], vtb_ref[h * vrows:(h + 1) * vrows, :]],
                             axis=1)
        ot = jnp.dot(vt, pt, preferred_element_type=F32)
        l = ot[d:d + 1] + jnp.exp2(sink - m)
        o = ot[:d] / l
        for g in range(0, group, 2):
            pair = jnp.concatenate([o[:, g * nq:(g + 1) * nq], o[:, (g + 1) * nq:(g + 2) * nq]], axis=0)
            c0 = (h * group + g) * d
            o_ref[:, c0:c0 + 2 * d] = pair.T.astype(o_ref.dtype)


def _swa_attention(q, k, vt, sinks, *, seq):
    m, qw = q.shape
    kvw = k.shape[1]
    nq = WINDOW
    cur = lambda t: (t, 0)
    prev = lambda t: (jnp.maximum(t - 1, 0), 0)
    cur_t = lambda t: (0, t)
    prev_t = lambda t: (0, jnp.maximum(t - 1, 0))
    return pl.pallas_call(
        functools.partial(_swa_kernel, blocks_per_seq=seq // nq),
        grid=(m // nq,),
        in_specs=[pl.BlockSpec(memory_space=pltpu.SMEM), pl.BlockSpec((nq, qw), cur),
                  pl.BlockSpec((nq, kvw), prev), pl.BlockSpec((nq, kvw), cur),
                  pl.BlockSpec((vt.shape[0], nq), prev_t), pl.BlockSpec((vt.shape[0], nq), cur_t)],
        out_specs=pl.BlockSpec((nq, qw), cur),
        out_shape=jax.ShapeDtypeStruct((m, qw), BF16),
        scratch_shapes=[pltpu.VMEM((2, 2 * nq, SWA_HEADS // SWA_KV_HEADS * nq), F32)],
        compiler_params=_cparams(1), name="swa_attention",
    )(sinks, q, k, k, vt, vt)


def _ffn_up_kernel(h_ref, wg_ref, wu_ref, cw_ref, cb_ref, prev_ref, act_ref, tail_ref,
                   halo_ref, carry_ref, *, rows, n_slab, tiles_per_seq, sub):
    i, j = pl.program_id(0), pl.program_id(1)
    halo = SUBLANES

    w0, w1, w2 = cw_ref[0:1, :], cw_ref[1:2, :], cw_ref[2:3, :]
    bias = cb_ref[...]
    tm = h_ref.shape[0]

    def matmuls(r0):
        h = h_ref[r0:r0 + sub, :]
        return (jnp.dot(h, wg_ref[...], preferred_element_type=F32),
                jnp.dot(h, wu_ref[...], preferred_element_type=F32))

    def conv_act(gs, us, before):
        r8 = lax.broadcasted_iota(jnp.int32, (halo, gs.shape[1]), 0)
        g1 = pltpu.roll(gs, 1, axis=0)
        g2 = pltpu.roll(gs, 2, axis=0)
        head1 = jnp.where(r8 == 0, before[halo - 1:halo], g1[:halo])
        head2 = jnp.where(r8 == 0, before[halo - 2:halo - 1], jnp.where(r8 == 1, before[halo - 1:halo], g2[:halo]))
        g1 = jnp.concatenate([head1, g1[halo:]], axis=0)
        g2 = jnp.concatenate([head2, g2[halo:]], axis=0)
        conv = ((bias + g2 * w0) + g1 * w1) + gs * w2
        return (conv * jax.nn.sigmoid(conv)) * us

    if n_slab == 1:
        is_start = (i % tiles_per_seq) == 0

        @pl.when(is_start)
        def _():
            halo_ref[...] = prev_ref[0]

        @pl.when(jnp.logical_not(is_start))
        def _():
            halo_ref[...] = carry_ref[j]

    starts = list(range(0, tm, sub))
    pending = matmuls(starts[0])
    before = halo_ref[...] if n_slab == 1 else None
    for t, r0 in enumerate(starts):
        gate, up = pending
        if t + 1 < len(starts):
            pending = matmuls(starts[t + 1])
        for s0 in range(0, sub, rows):
            gs, us = gate[s0:s0 + rows], up[s0:s0 + rows]
            if n_slab > 1:
                before = prev_ref[(r0 + s0) // rows]
            act_ref[r0 + s0:r0 + s0 + rows, :] = conv_act(gs, us, before).astype(BF16)
            before = gs[rows - halo:rows]
            if n_slab > 1:
                tail_ref[(r0 + s0) // rows] = before
    if n_slab == 1:
        tail_ref[0] = before
        carry_ref[j] = before


def _ffn_up(h, wg, wu, conv_w, conv_b, prev, *, seq):
    m, d_model = h.shape
    d_ff = wg.shape[1]
    tm = _tile(m, FFN_ROW_TILE)
    tf = _tile(d_ff, FFN_COL_TILE)
    if tm >= seq:
        rows, n_slab, tiles_per_seq = seq, tm // seq, 1
        prev_spec = pl.BlockSpec((n_slab, SUBLANES, tf), lambda i, j: (i, 0, j))
    else:
        rows, n_slab, tiles_per_seq = tm, 1, seq // tm
        prev_spec = pl.BlockSpec((1, SUBLANES, tf), lambda i, j: (i // tiles_per_seq, 0, j))
    sub = _tile(tm, FFN_SUB_TILE)
    if n_slab == 1:
        rows = sub
    assert sub % rows == 0
    nj = d_ff // tf
    return pl.pallas_call(
        functools.partial(_ffn_up_kernel, rows=rows, n_slab=n_slab, tiles_per_seq=tiles_per_seq, sub=sub),
        grid=(m // tm, nj),
        in_specs=[pl.BlockSpec((tm, d_model), lambda i, j: (i, 0)),
                  pl.BlockSpec((d_model, tf), lambda i, j: (0, j)),
                  pl.BlockSpec((d_model, tf), lambda i, j: (0, j)),
                  pl.BlockSpec((CONV_W, tf), lambda i, j: (0, j)),
                  pl.BlockSpec((1, tf), lambda i, j: (0, j)),
                  prev_spec],
        out_specs=[pl.BlockSpec((tm, tf), lambda i, j: (i, j)),
                   pl.BlockSpec((n_slab, SUBLANES, tf), lambda i, j: (i, 0, j))],
        out_shape=[jax.ShapeDtypeStruct((m, d_ff), BF16),
                   jax.ShapeDtypeStruct((m // tm * n_slab, SUBLANES, d_ff), F32)],
        scratch_shapes=[pltpu.VMEM((SUBLANES, tf), F32),
                        pltpu.VMEM((nj, SUBLANES, tf), F32)],
        compiler_params=_cparams(2), name="ffn_up",
    )(h, wg, wu, conv_w, conv_b, prev)


def _final_norm_kernel(x_ref, g_ref, o_ref):
    o_ref[...] = _rms(x_ref[...], g_ref[...], NORM_EPS)


def _final_norm(x, gain):
    m, d = x.shape
    tm = _tile(m, ROW_TILE)
    return pl.pallas_call(
        _final_norm_kernel, grid=(m // tm,),
        in_specs=[pl.BlockSpec((tm, d), lambda i: (i, 0)), pl.BlockSpec((1, d), lambda i: (0, 0))],
        out_specs=pl.BlockSpec((tm, d), lambda i: (i, 0)),
        out_shape=jax.ShapeDtypeStruct((m, d), F32),
        compiler_params=_cparams(1), name="final_norm",
    )(x, gain)


def _conv_ffn(y, h, seq, prev_state, wg, wu, conv_w, conv_b, wd):
    n_seq = y.shape[0] // seq
    d_ff = wg.shape[1]
    prev = jnp.concatenate([jnp.zeros((n_seq, SUBLANES - (CONV_W - 1), d_ff), F32), prev_state], axis=1)
    act, tails = _ffn_up(h, wg, wu, conv_w, conv_b, prev, seq=seq)
    y = _mm_res(act, wd, y)
    state = tails.reshape(n_seq, -1, SUBLANES, d_ff)[:, -1, SUBLANES - (CONV_W - 1):, :]
    return y, state


def kernel(x_prompt, x_sample, cache_mla_ckv, cache_mla_kpe, cache_swa_k, cache_swa_v, cache_diff_k,
           cache_diff_v, state_ffn_conv, norm_mix, norm_ffn, final_norm, mla_w_dq, mla_q_norm, mla_w_uq,
           mla_w_dkv, mla_kv_norm, mla_w_ukv, mla_w_o, swa_w_qkv, swa_sinks, swa_w_o, diff_w_qkv,
           diff_lambda_q1, diff_lambda_k1, diff_lambda_q2, diff_lambda_k2, diff_subln, diff_w_o,
           ffn_w_gate, ffn_w_up, ffn_conv_w, ffn_conv_b, ffn_w_down):
    bp, sp, d_model = x_prompt.shape
    bs, ss, _ = x_sample.shape
    depth = norm_mix.shape[0]
    past = cache_mla_ckv.shape[2]
    d_ff = ffn_w_gate.shape[2]
    q_rank = mla_w_dq.shape[2]
    kv_rank = mla_w_ukv.shape[1]

    yp = x_prompt.reshape(bp * sp, d_model)
    ys = x_sample.reshape(bs * ss, d_model)
    row_p = _tile(bp * sp, ROW_TILE)
    row_s = _tile(bs * ss, ROW_TILE)

    def tables(head_dim):
        return (_rope_tables(sp, 0, head_dim, row_p), _rope_tables(ss, past, head_dim, row_s))

    tab64, tab128 = tables(64), tables(128)
    outs_p = {k: [] for k in ("ckv", "kpe", "swk", "swv", "dk", "dv", "conv")}
    outs_s = {k: [] for k in ("ckv", "kpe", "swk", "swv", "dk", "dv", "conv")}

    for i in range(depth):
        kind, j = i % N_MIXERS, i // N_MIXERS
        gain = norm_mix[i][None, :]
        ffn_gain = norm_ffn[i][None, :]
        if kind == 0:
            q_scale = (MLA_NOPE + MLA_ROPE) ** -0.5 * LOG2E
            pad = (-(q_rank + kv_rank + MLA_ROPE)) % LANES
            w_cat = jnp.concatenate([mla_w_dq[j], mla_w_dkv[j], jnp.zeros((d_model, pad), F32)],
                                    axis=1).astype(BF16)
            w_uq = mla_w_uq[j].reshape(q_rank, MLA_HEADS, MLA_NOPE + MLA_ROPE)
            w_uq = jnp.pad(w_uq, ((0, 0), (0, 0), (0, MLA_QK_PAD - MLA_NOPE - MLA_ROPE)))
            w_uq = w_uq.reshape(q_rank, MLA_HEADS * MLA_QK_PAD).astype(BF16)
            w_ukv = mla_w_ukv[j].reshape(kv_rank, MLA_HEADS, MLA_NOPE + MLA_V)
            wk = w_ukv[:, :, :MLA_NOPE].reshape(kv_rank, MLA_HEADS * MLA_NOPE).astype(BF16)
            wv = w_ukv[:, :, MLA_NOPE:].reshape(kv_rank, MLA_HEADS * MLA_V).astype(BF16)
            w_o = mla_w_o[j].astype(BF16)
            qn, kvn = mla_q_norm[j][None, :], mla_kv_norm[j][None, :]

            (cos_p, sin_p), (cos_s, sin_s) = tab64
            qa, ckv_p, kpe_p = _mla_down(yp, gain, w_cat, qn, kvn, cos_p, sin_p, q_rank, kv_rank)
            q = _mla_q_up(qa, w_uq, cos_p, sin_p, q_scale)
            k, vt = _mla_expand(ckv_p, kpe_p, wk, wv.T)
            o = _flash_attention(q, k, vt, batch=bp, seq=sp, n_kv_heads=MLA_HEADS, dk=MLA_QK_PAD,
                                 dv=MLA_V, groups=1, n_maps=1)
            yp, hp = _mm_res(o, w_o, yp, ffn_gain)

            qa, ckv_s, kpe_s = _mla_down(ys, gain, w_cat, qn, kvn, cos_s, sin_s, q_rank, kv_rank)
            q = _mla_q_up(qa, w_uq, cos_s, sin_s, q_scale)
            o = _mla_decode_attention(q, cache_mla_ckv.reshape(-1, kv_rank), cache_mla_kpe.reshape(-1, MLA_ROPE),
                                      ckv_s, kpe_s, wk, wv, layer=j, batch=bs, past=past, new=ss)
            ys, hs = _mm_res(o, w_o, ys, ffn_gain)
            outs_p["ckv"].append(ckv_p.reshape(bp, sp, kv_rank))
            outs_p["kpe"].append(kpe_p.reshape(bp, sp, MLA_ROPE))
            outs_s["ckv"].append(ckv_s.reshape(bs, ss, kv_rank))
            outs_s["kpe"].append(kpe_s.reshape(bs, ss, MLA_ROPE))
        elif kind == 1:
            qw, kw = SWA_HEADS * SWA_HEAD_DIM, SWA_KV_HEADS * SWA_HEAD_DIM
            w_qkv = swa_w_qkv[j].astype(BF16)
            w_o = swa_w_o[j].astype(BF16)
            sinks = swa_sinks[j]
            (cos_p, sin_p), (cos_s, sin_s) = tab64
            q_scale = SWA_HEAD_DIM ** -0.5 * LOG2E
            q, k, v, kb, vt = _qkv_proj(yp, gain, w_qkv, cos_p, sin_p, qw, kw, kw, SWA_HEAD_DIM,
                                        q_scale=q_scale, vt_head=(SWA_HEAD_DIM, LANES - SWA_HEAD_DIM),
                                        vt_blocked=False)
            o = _swa_attention(q, kb, vt, sinks, seq=sp)
            yp, hp = _mm_res(o, w_o, yp, ffn_gain)
            k3 = k.reshape(bp, sp, SWA_KV_HEADS, SWA_HEAD_DIM)
            v3 = v.reshape(bp, sp, SWA_KV_HEADS, SWA_HEAD_DIM)
            outs_p["swk"].append(k3[:, sp - WINDOW:])
            outs_p["swv"].append(v3[:, sp - WINDOW:])

            q, k, v, _, _ = _qkv_proj(ys, gain, w_qkv, cos_s, sin_s, qw, kw, kw, SWA_HEAD_DIM, q_scale=q_scale)
            kc = cache_swa_k[j].reshape(bs * WINDOW, kw)
            vc = cache_swa_v[j].reshape(bs * WINDOW, kw)
            o = _swa_decode_attention(q, kc, k, vc, v, sinks, nq=ss, na=WINDOW, nb=ss)
            ys, hs = _mm_res(o, w_o, ys, ffn_gain)
            k_all = jnp.concatenate([cache_swa_k[j], k.reshape(bs, ss, SWA_KV_HEADS, SWA_HEAD_DIM)], axis=1)
            v_all = jnp.concatenate([cache_swa_v[j], v.reshape(bs, ss, SWA_KV_HEADS, SWA_HEAD_DIM)], axis=1)
            outs_s["swk"].append(k_all[:, ss:])
            outs_s["swv"].append(v_all[:, ss:])
        else:
            lam_init = 0.8 - 0.6 * math.exp(-0.3 * i)
            q_scale = DIFF_HEAD_DIM ** -0.5 * LOG2E
            groups = DIFF_HEADS // DIFF_KV_HEADS
            qw = DIFF_HEADS * 2 * DIFF_HEAD_DIM
            kw = DIFF_KV_HEADS * 2 * DIFF_HEAD_DIM
            w_qkv = diff_w_qkv[j].astype(BF16)
            w_o = diff_w_o[j].astype(BF16)
            lam = jnp.stack([diff_lambda_q1[j], diff_lambda_k1[j], diff_lambda_q2[j], diff_lambda_k2[j]])
            subln = diff_subln[j][None, :]
            common = dict(n_kv_heads=DIFF_KV_HEADS, dk=DIFF_HEAD_DIM, dv=2 * DIFF_HEAD_DIM, groups=groups,
                          n_maps=2, lam=lam, subln=subln, lam_init=lam_init)
            (cos_p, sin_p), (cos_s, sin_s) = tab128
            q, k, v, kb, vt = _qkv_proj(yp, gain, w_qkv, cos_p, sin_p, qw, kw, kw, DIFF_HEAD_DIM,
                                        q_scale=q_scale, vt_head=(2 * DIFF_HEAD_DIM, ONES_ROWS))
            o = _flash_attention(q, kb, vt, batch=bp, seq=sp, **common)
            yp, hp = _mm_res(o, w_o, yp, ffn_gain)
            outs_p["dk"].append(k.reshape(bp, sp, DIFF_KV_HEADS, 2, DIFF_HEAD_DIM))
            outs_p["dv"].append(v.reshape(bp, sp, DIFF_KV_HEADS, 2 * DIFF_HEAD_DIM))

            q, k, v, _, _ = _qkv_proj(ys, gain, w_qkv, cos_s, sin_s, qw, kw, kw, DIFF_HEAD_DIM,
                                      q_scale=q_scale)
            o = _decode_attention(q, cache_diff_k[j].reshape(bs * past, kw),
                                  cache_diff_v[j].reshape(bs * past, kw), k, v,
                                  batch=bs, past=past, new=ss, **common)
            ys, hs = _mm_res(o, w_o, ys, ffn_gain)
            outs_s["dk"].append(k.reshape(bs, ss, DIFF_KV_HEADS, 2, DIFF_HEAD_DIM))
            outs_s["dv"].append(v.reshape(bs, ss, DIFF_KV_HEADS, 2 * DIFF_HEAD_DIM))

        wg, wu, wd = ffn_w_gate[i].astype(BF16), ffn_w_up[i].astype(BF16), ffn_w_down[i].astype(BF16)
        cb = ffn_conv_b[i][None, :]
        yp, conv_p = _conv_ffn(yp, hp, sp, jnp.zeros((bp, CONV_W - 1, d_ff), F32), wg, wu, ffn_conv_w[i], cb, wd)
        ys, conv_s = _conv_ffn(ys, hs, ss, state_ffn_conv[i], wg, wu, ffn_conv_w[i], cb, wd)
        outs_p["conv"].append(conv_p)
        outs_s["conv"].append(conv_s)

    fg = final_norm[None, :]
    y_prompt = _final_norm(yp, fg).reshape(bp, sp, d_model)
    y_sample = _final_norm(ys, fg).reshape(bs, ss, d_model)
    order = ("ckv", "kpe", "swk", "swv", "dk", "dv", "conv")
    stack = lambda xs: xs[0][None] if len(xs) == 1 else jnp.stack(xs)
    return (y_prompt, y_sample) + tuple(stack(outs_p[k]) for k in order) + tuple(
        stack(outs_s[k]) for k in order)
```

```python
import functools
import math

import jax
import jax.numpy as jnp
from jax import lax
from jax.experimental import pallas as pl
from jax.experimental.pallas import tpu as pltpu

F32 = jnp.float32
BF16 = jnp.bfloat16

CHUNK = 64
ROPE_THETA = 10000.0
NORM_EPS = 1e-6
SUBLN_EPS = 1e-5
N_MIXERS = 3
MLA_HEADS = 16
MLA_NOPE = 128
MLA_ROPE = 64
MLA_V = 128
SWA_HEADS = 32
SWA_KV_HEADS = 4
SWA_HEAD_DIM = 64
WINDOW = 128
DIFF_HEADS = 8
DIFF_KV_HEADS = 4
DIFF_HEAD_DIM = 128
CONV_W = 3

LANES = 128
SUBLANES = 8
MXU_COLS = 256
MLA_QK_PAD = MXU_COLS
VMEM_LIMIT_BYTES = 56 * 1024 * 1024
LOG2E = 1.4426950408889634

ROW_TILE = 512
FFN_ROW_TILE = 1024
FFN_COL_TILE = 512
FFN_SUB_TILE = 256
ATTN_Q_TILE = 2048
ATTN_KV_TILE = 512
ATTN_CHAIN = 256
ATTN_LOOKAHEAD = 3
ONES_ROWS = 16
MLA_DECODE_KV_TILE = 1024
MM_TILE = 512
MM_WEIGHT_BLOCK_BYTES = 12 * 1024 * 1024
MM_SUB_TILE = 256
NORM_SUB_TILE = 128


def _cparams(n_axes):
    return pltpu.CompilerParams(dimension_semantics=("arbitrary",) * n_axes,
                                vmem_limit_bytes=VMEM_LIMIT_BYTES)


def _resident(shape):
    return pl.BlockSpec(shape, lambda *_: (0,) * len(shape), pipeline_mode=pl.Buffered(1))


def _tile(n, pref):
    t = min(n, pref)
    assert n % t == 0, (n, pref)
    return t


def _rms(x, g, eps):
    ms = jnp.mean(x * x, axis=-1, keepdims=True)
    return (x * lax.rsqrt(ms + eps)) * g


def _rope_slab(x, cos, sin, head_dim):
    if head_dim == LANES:
        swapped = pltpu.roll(x, LANES // 2, axis=1)
    else:
        half = head_dim // 2
        lane = lax.broadcasted_iota(jnp.int32, x.shape, 1)
        first = (lane & (head_dim - 1)) < half
        swapped = jnp.where(first, pltpu.roll(x, LANES - half, axis=1), pltpu.roll(x, half, axis=1))
    return x * cos + swapped * sin


def _rope_tables(seq_len, offset, head_dim, rows):
    half = head_dim // 2
    pos = jnp.arange(seq_len, dtype=F32) + offset
    inv = ROPE_THETA ** (-jnp.arange(0, head_dim, 2, dtype=F32) / head_dim)
    ang = pos[:, None] * inv[None, :]
    cos, sin = jnp.cos(ang), jnp.sin(ang)
    reps = LANES // head_dim
    cos_l = jnp.tile(jnp.concatenate([cos, cos], axis=1), (1, reps))
    sin_l = jnp.tile(jnp.concatenate([-sin, sin], axis=1), (1, reps))
    n = max(rows // seq_len, 1)
    return jnp.tile(cos_l, (n, 1)), jnp.tile(sin_l, (n, 1))


def _mla_down_kernel(x_ref, g_ref, w_ref, qn_ref, kvn_ref, cos_ref, sin_ref,
                     qa_ref, ckv_ref, kpe_ref, *, q_rank, kv_rank, sub):
    def project(r0):
        h = _rms(x_ref[r0:r0 + sub, :], g_ref[...], NORM_EPS).astype(BF16)
        return jnp.dot(h, w_ref[...], preferred_element_type=F32)

    starts = list(range(0, x_ref.shape[0], sub))
    pending = project(starts[0])
    for t, r0 in enumerate(starts):
        d = pending
        if t + 1 < len(starts):
            pending = project(starts[t + 1])
        rows = slice(r0, r0 + sub)
        qa_ref[rows, :] = _rms(d[:, :q_rank], qn_ref[...], NORM_EPS).astype(BF16)
        ckv_ref[rows, :] = _rms(d[:, q_rank:q_rank + kv_rank], kvn_ref[...], NORM_EPS)
        slab = d[:, q_rank + kv_rank:q_rank + kv_rank + LANES]
        kpe_ref[rows, :] = _rope_slab(slab, cos_ref[rows, :], sin_ref[rows, :], MLA_ROPE)[:, :MLA_ROPE]


def _mla_down(x, gain, w_cat, q_norm, kv_norm, cos, sin, q_rank, kv_rank):
    m, d_model = x.shape
    tm = _tile(m, ROW_TILE)
    n_tab = cos.shape[0] // tm
    wn = w_cat.shape[1]
    row = lambda i: (i, 0)
    fixed = lambda i: (0, 0)
    tab = lambda i: (i % n_tab, 0)
    return pl.pallas_call(
        functools.partial(_mla_down_kernel, q_rank=q_rank, kv_rank=kv_rank, sub=_tile(tm, NORM_SUB_TILE)),
        grid=(m // tm,),
        in_specs=[pl.BlockSpec((tm, d_model), row), pl.BlockSpec((1, d_model), fixed),
                  _resident((d_model, wn)), pl.BlockSpec((1, q_rank), fixed),
                  pl.BlockSpec((1, kv_rank), fixed), pl.BlockSpec((tm, LANES), tab),
                  pl.BlockSpec((tm, LANES), tab)],
        out_specs=[pl.BlockSpec((tm, q_rank), row), pl.BlockSpec((tm, kv_rank), row),
                   pl.BlockSpec((tm, MLA_ROPE), row)],
        out_shape=[jax.ShapeDtypeStruct((m, q_rank), BF16), jax.ShapeDtypeStruct((m, kv_rank), F32),
                   jax.ShapeDtypeStruct((m, MLA_ROPE), F32)],
        compiler_params=_cparams(1), name="mla_down",
    )(x, gain, w_cat, q_norm, kv_norm, cos, sin)


def _mla_q_up_kernel(qa_ref, w_ref, cos_ref, sin_ref, q_ref, *, q_scale):
    qa = qa_ref[...]
    cos, sin = cos_ref[...], sin_ref[...]
    for h in range(MLA_HEADS):
        c0 = h * MLA_QK_PAD
        d = jnp.dot(qa, w_ref[:, c0:c0 + MLA_QK_PAD], preferred_element_type=F32)
        q_ref[:, c0:c0 + LANES] = (d[:, :LANES] * q_scale).astype(BF16)
        q_ref[:, c0 + LANES:c0 + MLA_QK_PAD] = (_rope_slab(d[:, LANES:], cos, sin, MLA_ROPE) * q_scale).astype(BF16)


def _mla_q_up(qa, w_pad, cos, sin, q_scale):
    m, q_rank = qa.shape
    tm = _tile(m, ROW_TILE)
    n_tab = cos.shape[0] // tm
    n = w_pad.shape[1]
    row = lambda i: (i, 0)
    fixed = lambda i: (0, 0)
    tab = lambda i: (i % n_tab, 0)
    return pl.pallas_call(
        functools.partial(_mla_q_up_kernel, q_scale=q_scale), grid=(m // tm,),
        in_specs=[pl.BlockSpec((tm, q_rank), row), _resident((q_rank, n)),
                  pl.BlockSpec((tm, LANES), tab), pl.BlockSpec((tm, LANES), tab)],
        out_specs=pl.BlockSpec((tm, n), row),
        out_shape=jax.ShapeDtypeStruct((m, n), BF16),
        compiler_params=_cparams(1), name="mla_q_up",
    )(qa, w_pad, cos, sin)


def _mla_expand_kernel(ckv_ref, kpe_ref, wk_ref, wvt_ref, k_ref, vt_ref):
    c = ckv_ref[...].astype(BF16)
    kpe = kpe_ref[...].astype(BF16)
    tm = c.shape[0]
    dva = MLA_V + ONES_ROWS
    ones = jnp.ones((ONES_ROWS, tm), BF16)
    zeros = jnp.zeros((tm, MLA_QK_PAD - MLA_NOPE - MLA_ROPE), BF16)
    pair = MXU_COLS // MLA_NOPE

    def project(h0):
        kn = jnp.dot(c, wk_ref[:, h0 * MLA_NOPE:(h0 + pair) * MLA_NOPE], preferred_element_type=F32)
        vt = lax.dot_general(wvt_ref[h0 * MLA_V:(h0 + pair) * MLA_V, :], c, (((1,), (1,)), ((), ())),
                             preferred_element_type=F32)
        return kn, vt

    pending = project(0)
    for h0 in range(0, MLA_HEADS, pair):
        kn, vt = pending
        if h0 + pair < MLA_HEADS:
            pending = project(h0 + pair)
        for i in range(pair):
            h = h0 + i
            c0 = h * MLA_QK_PAD
            k_ref[:, c0:c0 + MLA_NOPE] = kn[:, i * MLA_NOPE:(i + 1) * MLA_NOPE].astype(BF16)
            k_ref[:, c0 + MLA_NOPE:c0 + MLA_NOPE + MLA_ROPE] = kpe
            k_ref[:, c0 + MLA_NOPE + MLA_ROPE:c0 + MLA_QK_PAD] = zeros
            vt_ref[0, h * dva:h * dva + MLA_V, :] = vt[i * MLA_V:(i + 1) * MLA_V].astype(BF16)
            vt_ref[0, h * dva + MLA_V:(h + 1) * dva, :] = ones


def _mla_expand(ckv, kpe, wk, wvt):
    m, kv_rank = ckv.shape
    tm = _tile(m, ATTN_KV_TILE)
    row = lambda i: (i, 0)
    nk, nva = MLA_HEADS * MLA_QK_PAD, MLA_HEADS * (MLA_V + ONES_ROWS)
    return pl.pallas_call(
        _mla_expand_kernel, grid=(m // tm,),
        in_specs=[pl.BlockSpec((tm, kv_rank), row), pl.BlockSpec((tm, MLA_ROPE), row),
                  _resident(wk.shape), _resident(wvt.shape)],
        out_specs=[pl.BlockSpec((tm, nk), row), pl.BlockSpec((1, nva, tm), lambda i: (i, 0, 0))],
        out_shape=[jax.ShapeDtypeStruct((m, nk), BF16), jax.ShapeDtypeStruct((m // tm, nva, tm), BF16)],
        compiler_params=_cparams(1), name="mla_expand",
    )(ckv, kpe, wk, wvt)


def _qkv_proj_kernel(x_ref, g_ref, w_ref, cos_ref, sin_ref, q_ref, k_ref, v_ref, kb_ref, vb_ref,
                     *, qw, kw, vw, head_dim, q_scale, vt_head):
    h = _rms(x_ref[...], g_ref[...], NORM_EPS).astype(BF16)
    cos, sin = cos_ref[...], sin_ref[...]
    vt_dst = None
    if vt_head:
        vt_dst = vb_ref.at[0] if len(vb_ref.shape) == 3 else vb_ref
    for n0 in range(0, qw + kw + vw, MXU_COLS):
        dd = jnp.dot(h, w_ref[:, n0:n0 + MXU_COLS], preferred_element_type=F32)
        for c0 in range(n0, n0 + MXU_COLS, LANES):
            d = dd[:, c0 - n0:c0 - n0 + LANES]
            if c0 < qw:
                q_ref[:, c0:c0 + LANES] = (_rope_slab(d, cos, sin, head_dim) * q_scale).astype(BF16)
            elif c0 < qw + kw:
                r = _rope_slab(d, cos, sin, head_dim)
                k_ref[:, c0 - qw:c0 - qw + LANES] = r
                kb_ref[:, c0 - qw:c0 - qw + LANES] = r.astype(BF16)
            else:
                c1 = c0 - qw - kw
                v_ref[:, c1:c1 + LANES] = d
                if vt_head:
                    vd, ones_rows = vt_head
                    dt = d.T.astype(BF16)
                    for hv in range(c1 // vd, (c1 + LANES - 1) // vd + 1):
                        lo, hi = max(c1, hv * vd), min(c1 + LANES, (hv + 1) * vd)
                        r0 = hv * (vd + ones_rows) + lo - hv * vd
                        vt_dst[r0:r0 + hi - lo, :] = dt[lo - c1:hi - c1]
                        if hi == (hv + 1) * vd:
                            vt_dst[r0 + hi - lo:r0 + hi - lo + ones_rows, :] = jnp.ones((ones_rows, d.shape[0]), BF16)
                else:
                    vb_ref[:, c1:c1 + LANES] = d.astype(BF16)


def _qkv_proj(x, gain, w, cos, sin, qw, kw, vw, head_dim, *, q_scale=1.0, vt_head=None, vt_blocked=True):
    m, d_model = x.shape
    tm = _tile(m, ATTN_KV_TILE if vt_head else ROW_TILE)
    n_tab = cos.shape[0] // tm
    row = lambda i: (i, 0)
    fixed = lambda i: (0, 0)
    tab = lambda i: (i % n_tab, 0)
    if vt_head:
        vwa = vw // vt_head[0] * (vt_head[0] + vt_head[1])
        if vt_blocked:
            vb_spec = pl.BlockSpec((1, vwa, tm), lambda i: (i, 0, 0))
            vb_shape = jax.ShapeDtypeStruct((m // tm, vwa, tm), BF16)
        else:
            vb_spec = pl.BlockSpec((vwa, tm), lambda i: (0, i))
            vb_shape = jax.ShapeDtypeStruct((vwa, m), BF16)
    else:
        vb_spec = pl.BlockSpec((tm, vw), row)
        vb_shape = jax.ShapeDtypeStruct((m, vw), BF16)
    return pl.pallas_call(
        functools.partial(_qkv_proj_kernel, qw=qw, kw=kw, vw=vw, head_dim=head_dim, q_scale=q_scale,
                          vt_head=vt_head),
        grid=(m // tm,),
        in_specs=[pl.BlockSpec((tm, d_model), row), pl.BlockSpec((1, d_model), fixed),
                  _resident(w.shape), pl.BlockSpec((tm, LANES), tab),
                  pl.BlockSpec((tm, LANES), tab)],
        out_specs=[pl.BlockSpec((tm, qw), row), pl.BlockSpec((tm, kw), row), pl.BlockSpec((tm, vw), row),
                   pl.BlockSpec((tm, kw), row), vb_spec],
        out_shape=[jax.ShapeDtypeStruct((m, qw), BF16), jax.ShapeDtypeStruct((m, kw), F32),
                   jax.ShapeDtypeStruct((m, vw), F32), jax.ShapeDtypeStruct((m, kw), BF16), vb_shape],
        compiler_params=_cparams(1), name="qkv_proj",
    )(x, gain, w, cos, sin)


def _mm_res_kernel(*refs, sub, with_norm):
    if with_norm:
        a_ref, w_ref, r_ref, g_ref, o_ref, h_ref = refs
    else:
        a_ref, w_ref, r_ref, o_ref = refs
    starts = list(range(0, a_ref.shape[0], sub))
    pending = jnp.dot(a_ref[0:sub, :], w_ref[...], preferred_element_type=F32)
    for t, r0 in enumerate(starts):
        d = pending
        if t + 1 < len(starts):
            r1 = starts[t + 1]
            pending = jnp.dot(a_ref[r1:r1 + sub, :], w_ref[...], preferred_element_type=F32)
        y = r_ref[r0:r0 + sub, :] + d
        o_ref[r0:r0 + sub, :] = y
        if with_norm:
            h_ref[r0:r0 + sub, :] = _rms(y, g_ref[...], NORM_EPS).astype(BF16)


def _mm_res(a, w, res, norm_gain=None, layer=None):
    m, k = a.shape
    n = w.shape[-1]
    tm = _tile(m, MM_TILE)
    tn = n
    while tn * k * w.dtype.itemsize > MM_WEIGHT_BLOCK_BYTES and tn % (2 * MXU_COLS) == 0:
        tn //= 2
    with_norm = norm_gain is not None
    assert not with_norm or tn == n
    tile = pl.BlockSpec((tm, tn), lambda j, i: (i, j))
    if layer is None:
        w_spec = pl.BlockSpec((k, tn), lambda j, i: (0, j))
    else:
        w_spec = pl.BlockSpec((None, k, tn), lambda j, i: (layer, 0, j))
    in_specs = [pl.BlockSpec((tm, k), lambda j, i: (i, 0)), w_spec, tile]
    out_specs, out_shape, args = tile, jax.ShapeDtypeStruct((m, n), F32), [a, w, res]
    if with_norm:
        in_specs.append(pl.BlockSpec((1, n), lambda j, i: (0, 0)))
        out_specs, out_shape = [tile, tile], [out_shape, jax.ShapeDtypeStruct((m, n), BF16)]
        args.append(norm_gain)
    return pl.pallas_call(
        functools.partial(_mm_res_kernel, sub=_tile(tm, MM_SUB_TILE), with_norm=with_norm),
        grid=(n // tn, m // tm), in_specs=in_specs, out_specs=out_specs, out_shape=out_shape,
        compiler_params=_cparams(2), name="mm_res",
    )(*args)


def _diff_lambda(lam_ref, lam_init):
    lam = lam_ref[...]
    a = jnp.sum(lam[0:1] * lam[1:2], axis=-1, keepdims=True)
    b = jnp.sum(lam[2:3] * lam[3:4], axis=-1, keepdims=True)
    return jnp.exp(a) - jnp.exp(b) + lam_init


def _combine_heads(o_list, o_ref, rows, lam_ref, subln_ref, *, n_maps, dv, lam_init):
    groups = len(o_list) // n_maps
    for g in range(groups):
        if n_maps == 1:
            o = o_list[g]
        else:
            lam = _diff_lambda(lam_ref, lam_init)
            o = o_list[2 * g] - lam * o_list[2 * g + 1]
            o = _rms(o, subln_ref[...], SUBLN_EPS) * (1.0 - lam_init)
        o_ref[rows, g * dv:(g + 1) * dv] = o.astype(o_ref.dtype)


def _flash_kernel(*refs, tq, tk, cw, dk, dv, groups, n_maps, lam_init):
    if n_maps == 2:
        lam_ref, subln_ref, q_ref, k_ref, vt_ref, o_ref, m_ref, acc_ref, st_ref, mx_ref = refs
    else:
        q_ref, k_ref, vt_ref, o_ref, m_ref, acc_ref, st_ref, mx_ref = refs
        lam_ref = subln_ref = None
    n_sub = groups * n_maps
    n_chain = tq // cw
    kv_per_q = tq // tk
    qi = pl.program_id(2)
    nt = (((1,), (1,)), ((), ()))

    m_ref[...] = jnp.full(m_ref.shape, -jnp.inf, F32)
    acc_ref[...] = jnp.zeros(acc_ref.shape, F32)

    def scores(item, k_blks, slot):
        u, n, k_lo, bi, nk = item
        mi = u % n_maps
        q_n = q_ref[n * cw:(n + 1) * cw, u * dk:(u + 1) * dk]
        st = lax.dot_general(k_blks[bi][:nk, mi * dk:(mi + 1) * dk], q_n, nt, preferred_element_type=F32)
        if k_lo is not None:
            kc = (lax.broadcasted_iota(jnp.int32, st.shape, 0) + k_lo) // CHUNK
            qc = (lax.broadcasted_iota(jnp.int32, st.shape, 1) + n * cw) // CHUNK
            st = jnp.where(kc <= qc, st, -jnp.inf)
        st_ref[slot, 0:nk, :] = st
        mx_ref[slot] = jnp.max(st, axis=0, keepdims=True)

    def update(item, slot, vt_blks):
        u, n, _, bi, nk = item
        st = st_ref[slot, 0:nk, :]
        idx = u * n_chain + n
        m_prev = m_ref[idx]
        m_new = jnp.maximum(m_prev, mx_ref[slot])
        alpha = jnp.exp2(m_prev - m_new)
        pt = jnp.exp2(st - m_new).astype(BF16)
        acc_ref[idx] = acc_ref[idx] * alpha + jnp.dot(vt_blks[bi][:, :nk], pt, preferred_element_type=F32)
        m_ref[idx] = m_new

    def run_chains(items, first_block):
        blocks = sorted({it[3] for it in items})
        k_blks = {bi: k_ref[pl.ds(pl.multiple_of((first_block + bi) * tk, tk), tk), :] for bi in blocks}
        vt_blks = {bi: vt_ref[first_block + bi] for bi in blocks}
        n_slots = ATTN_LOOKAHEAD + 1
        for t in range(min(ATTN_LOOKAHEAD, len(items))):
            scores(items[t], k_blks, t % n_slots)
        for t, item in enumerate(items):
            if t + ATTN_LOOKAHEAD < len(items):
                scores(items[t + ATTN_LOOKAHEAD], k_blks, (t + ATTN_LOOKAHEAD) % n_slots)
            update(item, t % n_slots, vt_blks)

    def body(j, carry):
        items = [(u, n, None, bi, tk) for bi in range(kv_per_q) for n in range(n_chain) for u in range(n_sub)]
        run_chains(items, j * kv_per_q)
        return carry

    lax.fori_loop(0, qi, body, 0)

    items = [(u, n, bi * tk if (bi + 1) * tk > n * cw else None, bi, min(tk, (n + 1) * cw - bi * tk))
             for bi in range(kv_per_q) for n in range(n_chain) if bi * tk < (n + 1) * cw
             for u in range(n_sub)]
    run_chains(items, qi * kv_per_q)

    for n in range(n_chain):
        outs = []
        for u in range(n_sub):
            a = acc_ref[u * n_chain + n]
            outs.append((a[:dv] / a[dv:dv + 1]).T)
        _combine_heads(outs, o_ref, slice(n * cw, (n + 1) * cw), lam_ref, subln_ref,
                       n_maps=n_maps, dv=dv, lam_init=lam_init)


def _flash_attention(q, k, vt, *, batch, seq, n_kv_heads, dk, dv, groups, n_maps,
                     lam=None, subln=None, lam_init=0.0):
    tk = vt.shape[2]
    dva = dv + ONES_ROWS
    n_sub = groups * n_maps
    tq = _tile(seq, max(ATTN_Q_TILE if n_sub == 1 else ATTN_Q_TILE // 2, tk))
    cw = _tile(tq, ATTN_CHAIN)
    assert tq % tk == 0 and tk % CHUNK == 0 and cw % CHUNK == 0
    nq, nk = seq // tq, seq // tk
    n_chain = tq // cw
    kern = functools.partial(_flash_kernel, tq=tq, tk=tk, cw=cw, dk=dk, dv=dv, groups=groups,
                             n_maps=n_maps, lam_init=lam_init)
    in_specs = [pl.BlockSpec((tq, n_sub * dk), lambda b, h, i: (b * nq + i, h)),
                pl.BlockSpec((seq, n_maps * dk), lambda b, h, i: (b, h)),
                pl.BlockSpec((nk, dva, tk), lambda b, h, i: (b, h, 0))]
    args = [q, k, vt]
    if n_maps == 2:
        in_specs = [pl.BlockSpec(lam.shape, lambda b, h, i: (0, 0)),
                    pl.BlockSpec(subln.shape, lambda b, h, i: (0, 0))] + in_specs
        args = [lam, subln] + args
    return pl.pallas_call(
        kern, grid=(batch, n_kv_heads, nq), in_specs=in_specs,
        out_specs=pl.BlockSpec((tq, groups * dv), lambda b, h, i: (b * nq + i, h)),
        out_shape=jax.ShapeDtypeStruct((batch * seq, n_kv_heads * groups * dv), BF16),
        scratch_shapes=[pltpu.VMEM((n_sub * n_chain, 1, cw), F32),
                        pltpu.VMEM((n_sub * n_chain, dva, cw), F32),
                        pltpu.VMEM((ATTN_LOOKAHEAD + 1, tk, cw), F32),
                        pltpu.VMEM((ATTN_LOOKAHEAD + 1, 1, cw), F32)],
        compiler_params=_cparams(3), name="flash_attention",
    )(*args)


def _decode_kernel(*refs, dk, dv, groups, n_maps, lam_init):
    if n_maps == 2:
        lam_ref, subln_ref, q_ref, kc_ref, vc_ref, kn_ref, vn_ref, o_ref = refs
    else:
        q_ref, kc_ref, vc_ref, kn_ref, vn_ref, o_ref = refs
        lam_ref = subln_ref = None
    nt = (((1,), (1,)), ((), ()))
    vc = vc_ref[...].astype(BF16)
    vn = vn_ref[...].astype(BF16)
    outs = []
    for u in range(groups * n_maps):
        mi = u % n_maps
        q = q_ref[:, u * dk:(u + 1) * dk]
        kc = kc_ref[:, mi * dk:(mi + 1) * dk].astype(BF16)
        kn = kn_ref[:, mi * dk:(mi + 1) * dk].astype(BF16)
        s1 = lax.dot_general(q, kc, nt, preferred_element_type=F32)
        s2 = lax.dot_general(q, kn, nt, preferred_element_type=F32)
        m = jnp.maximum(jnp.max(s1, axis=1, keepdims=True), jnp.max(s2, axis=1, keepdims=True))
        p1 = jnp.exp2(s1 - m)
        p2 = jnp.exp2(s2 - m)
        l = jnp.sum(p1, axis=1, keepdims=True) + jnp.sum(p2, axis=1, keepdims=True)
        o = (jnp.dot(p1.astype(BF16), vc, preferred_element_type=F32)
             + jnp.dot(p2.astype(BF16), vn, preferred_element_type=F32))
        outs.append(o / l)
    _combine_heads(outs, o_ref, slice(None), lam_ref, subln_ref, n_maps=n_maps, dv=dv, lam_init=lam_init)


def _decode_attention(q, kc, vc, kn, vn, *, batch, past, new, n_kv_heads, dk, dv, groups, n_maps,
                      lam=None, subln=None, lam_init=0.0):
    n_sub = groups * n_maps
    kern = functools.partial(_decode_kernel, dk=dk, dv=dv, groups=groups, n_maps=n_maps, lam_init=lam_init)
    bh = lambda b, h: (b, h)
    in_specs = [pl.BlockSpec((new, n_sub * dk), bh), pl.BlockSpec((past, n_maps * dk), bh),
                pl.BlockSpec((past, dv), bh), pl.BlockSpec((new, n_maps * dk), bh),
                pl.BlockSpec((new, dv), bh)]
    args = [q, kc, vc, kn, vn]
    if n_maps == 2:
        in_specs = [pl.BlockSpec(lam.shape, lambda b, h: (0, 0)),
                    pl.BlockSpec(subln.shape, lambda b, h: (0, 0))] + in_specs
        args = [lam, subln] + args
    return pl.pallas_call(
        kern, grid=(batch, n_kv_heads), in_specs=in_specs,
        out_specs=pl.BlockSpec((new, groups * dv), bh),
        out_shape=jax.ShapeDtypeStruct((batch * new, n_kv_heads * groups * dv), BF16),
        compiler_params=_cparams(2), name="decode_attention",
    )(*args)


def _mla_decode_kernel(q_ref, ckv_c_ref, kpe_c_ref, ckv_n_ref, kpe_n_ref, wk_ref, wv_ref, o_ref,
                       kcat_ref, qcat_ref, *, block):
    new = q_ref.shape[0]
    past, rank = ckv_c_ref.shape
    nt = (((1,), (1,)), ((), ()))
    kcat_ref[0:past, 0:rank] = ckv_c_ref[...].astype(BF16)
    kcat_ref[past:past + new, 0:rank] = ckv_n_ref[...].astype(BF16)
    kcat_ref[0:past, rank:rank + MLA_ROPE] = kpe_c_ref[...].astype(BF16)
    kcat_ref[past:past + new, rank:rank + MLA_ROPE] = kpe_n_ref[...].astype(BF16)
    kcat_ref[:, rank + MLA_ROPE:rank + LANES] = jnp.zeros((past + new, LANES - MLA_ROPE), BF16)
    for h in range(MLA_HEADS):
        c0 = h * MLA_QK_PAD
        q_lat = lax.dot_general(q_ref[:, c0:c0 + MLA_NOPE], wk_ref[:, h * MLA_NOPE:(h + 1) * MLA_NOPE], nt,
                                preferred_element_type=F32)
        qcat_ref[h * new:(h + 1) * new, 0:rank] = q_lat.astype(BF16)
        qcat_ref[h * new:(h + 1) * new, rank:rank + LANES] = q_ref[:, c0 + MLA_NOPE:c0 + MLA_QK_PAD]
    q = qcat_ref[...]
    rows = MLA_HEADS * new
    m = jnp.full((rows, 1), -jnp.inf, F32)
    l = jnp.zeros((rows, 1), F32)
    acc = jnp.zeros((rows, rank), F32)
    starts = list(range(0, past, block)) + [past]
    for k0 in starts:
        nk = min(block, past - k0) if k0 < past else new
        kb = kcat_ref[k0:k0 + nk, :]
        s = lax.dot_general(q, kb, nt, preferred_element_type=F32)
        m_new = jnp.maximum(m, jnp.max(s, axis=1, keepdims=True))
        alpha = jnp.exp2(m - m_new)
        p = jnp.exp2(s - m_new)
        l = alpha * l + jnp.sum(p, axis=1, keepdims=True)
        acc = alpha * acc + jnp.dot(p.astype(BF16), kb[:, 0:rank], preferred_element_type=F32)
        m = m_new
    o_lat = (acc / l).astype(BF16)
    for h in range(MLA_HEADS):
        o_ref[:, h * MLA_V:(h + 1) * MLA_V] = jnp.dot(
            o_lat[h * new:(h + 1) * new], wv_ref[:, h * MLA_V:(h + 1) * MLA_V],
            preferred_element_type=F32).astype(o_ref.dtype)


def _mla_decode_attention(q, ckv_cache, kpe_cache, ckv_new, kpe_new, wk, wv, *, layer, batch, past, new):
    rank = ckv_cache.shape[1]
    per = lambda b: (b, 0)
    cached = lambda b: (layer * batch + b, 0)
    return pl.pallas_call(
        functools.partial(_mla_decode_kernel, block=_tile(past, MLA_DECODE_KV_TILE)), grid=(batch,),
        in_specs=[pl.BlockSpec((new, q.shape[1]), per), pl.BlockSpec((past, rank), cached),
                  pl.BlockSpec((past, MLA_ROPE), cached), pl.BlockSpec((new, rank), per),
                  pl.BlockSpec((new, MLA_ROPE), per), _resident(wk.shape), _resident(wv.shape)],
        out_specs=pl.BlockSpec((new, MLA_HEADS * MLA_V), per),
        out_shape=jax.ShapeDtypeStruct((batch * new, MLA_HEADS * MLA_V), BF16),
        scratch_shapes=[pltpu.VMEM((past + new, rank + LANES), BF16),
                        pltpu.VMEM((MLA_HEADS * new, rank + LANES), BF16)],
        compiler_params=_cparams(1), name="mla_decode_attention",
    )(q, ckv_cache, kpe_cache, ckv_new, kpe_new, wk, wv)


def _swa_decode_kernel(sinks_ref, q_ref, ka_ref, kb_ref, va_ref, vb_ref, o_ref):
    nq = q_ref.shape[0]
    group = SWA_HEADS // SWA_KV_HEADS
    d = SWA_HEAD_DIM
    nt = (((1,), (1,)), ((), ()))
    for h in range(SWA_KV_HEADS):
        hs = slice(h * d, (h + 1) * d)
        k = jnp.concatenate([ka_ref[:, hs], kb_ref[:, hs]], axis=0).astype(BF16)
        v = jnp.concatenate([va_ref[:, hs], vb_ref[:, hs]], axis=0).astype(BF16)
        q = jnp.concatenate([q_ref[:, (h * group + g) * d:(h * group + g + 1) * d] for g in range(group)],
                            axis=0)
        sink = jnp.concatenate([jnp.full((nq, 1), sinks_ref[h * group + g] * LOG2E, F32)
                                for g in range(group)], axis=0)
        s = lax.dot_general(q, k, nt, preferred_element_type=F32)
        m = jnp.maximum(jnp.max(s, axis=1, keepdims=True), sink)
        p = jnp.exp2(s - m)
        l = jnp.sum(p, axis=1, keepdims=True) + jnp.exp2(sink - m)
        o = jnp.dot(p.astype(BF16), v, preferred_element_type=F32) / l
        for g in range(0, group, 2):
            pair = jnp.concatenate([o[g * nq:(g + 1) * nq], o[(g + 1) * nq:(g + 2) * nq]], axis=1)
            c0 = (h * group + g) * d
            o_ref[:, c0:c0 + 2 * d] = pair.astype(o_ref.dtype)


def _swa_decode_attention(q, ka, kb, va, vb, sinks, *, nq, na, nb):
    m, qw = q.shape
    kvw = ka.shape[1]
    cur = lambda t: (t, 0)
    return pl.pallas_call(
        _swa_decode_kernel, grid=(m // nq,),
        in_specs=[pl.BlockSpec(memory_space=pltpu.SMEM), pl.BlockSpec((nq, qw), cur),
                  pl.BlockSpec((na, kvw), cur), pl.BlockSpec((nb, kvw), cur),
                  pl.BlockSpec((na, kvw), cur), pl.BlockSpec((nb, kvw), cur)],
        out_specs=pl.BlockSpec((nq, qw), cur),
        out_shape=jax.ShapeDtypeStruct((m, qw), BF16),
        compiler_params=_cparams(1), name="swa_decode_attention",
    )(sinks, q, ka, kb, va, vb)


def _swa_kernel(sinks_ref, q_ref, ka_ref, kb_ref, vta_ref, vtb_ref, o_ref, st_ref, *, blocks_per_seq):
    nq, na = q_ref.shape[0], ka_ref.shape[0]
    nk = na + kb_ref.shape[0]
    group = SWA_HEADS // SWA_KV_HEADS
    d = SWA_HEAD_DIM
    vrows = vta_ref.shape[0] // SWA_KV_HEADS
    nt = (((1,), (1,)), ((), ()))
    kc = lax.broadcasted_iota(jnp.int32, (nk, group * nq), 0) // CHUNK
    qc = (lax.broadcasted_iota(jnp.int32, (nk, group * nq), 1) % nq) // CHUNK
    first = (pl.program_id(0) % blocks_per_seq) == 0
    lo = jnp.where(first, na // CHUNK, 0)
    valid = (kc >= qc) & (kc <= qc + na // CHUNK) & (kc >= lo)

    def scores(h):
        hs = slice(h * d, (h + 1) * d)
        k = jnp.concatenate([ka_ref[:, hs], kb_ref[:, hs]], axis=0)
        q = jnp.concatenate([q_ref[:, (h * group + g) * d:(h * group + g + 1) * d] for g in range(group)],
                            axis=0)
        st = lax.dot_general(k, q, nt, preferred_element_type=F32)
        st_ref[h % 2] = jnp.where(valid, st, -jnp.inf)

    scores(0)
    for h in range(SWA_KV_HEADS):
        if h + 1 < SWA_KV_HEADS:
            scores(h + 1)
        st = st_ref[h % 2]
        sink = jnp.concatenate([jnp.full((1, nq), sinks_ref[h * group + g] * LOG2E, F32)
                                for g in range(group)], axis=1)
        m = jnp.maximum(jnp.max(st, axis=0, keepdims=True), sink)
        pt = jnp.exp2(st - m).astype(BF16)
        vt = jnp.concatenate([vta_ref[h * vrows:(h + 1) * vrows, :], vtb_ref[h * vrows:(h + 1) * vrows, :]],
                             axis=1)
        ot = jnp.dot(vt, pt, preferred_element_type=F32)
        l = ot[d:d + 1] + jnp.exp2(sink - m)
        o = ot[:d] / l
        for g in range(0, group, 2):
            pair = jnp.concatenate([o[:, g * nq:(g + 1) * nq], o[:, (g + 1) * nq:(g + 2) * nq]], axis=0)
            c0 = (h * group + g) * d
            o_ref[:, c0:c0 + 2 * d] = pair.T.astype(o_ref.dtype)


def _swa_attention(q, k, vt, sinks, *, seq):
    m, qw = q.shape
    kvw = k.shape[1]
    nq = WINDOW
    cur = lambda t: (t, 0)
    prev = lambda t: (jnp.maximum(t - 1, 0), 0)
    cur_t = lambda t: (0, t)
    prev_t = lambda t: (0, jnp.maximum(t - 1, 0))
    return pl.pallas_call(
        functools.partial(_swa_kernel, blocks_per_seq=seq // nq),
        grid=(m // nq,),
        in_specs=[pl.BlockSpec(memory_space=pltpu.SMEM), pl.BlockSpec((nq, qw), cur),
                  pl.BlockSpec((nq, kvw), prev), pl.BlockSpec((nq, kvw), cur),
                  pl.BlockSpec((vt.shape[0], nq), prev_t), pl.BlockSpec((vt.shape[0], nq), cur_t)],
        out_specs=pl.BlockSpec((nq, qw), cur),
        out_shape=jax.ShapeDtypeStruct((m, qw), BF16),
        scratch_shapes=[pltpu.VMEM((2, 2 * nq, SWA_HEADS // SWA_KV_HEADS * nq), F32)],
        compiler_params=_cparams(1), name="swa_attention",
    )(sinks, q, k, k, vt, vt)


def _ffn_up_kernel(h_ref, wg_ref, wu_ref, cw_ref, cb_ref, prev_ref, act_ref, tail_ref,
                   halo_ref, carry_ref, *, rows, n_slab, tiles_per_seq, sub):
    i, j = pl.program_id(0), pl.program_id(1)
    halo = SUBLANES

    w0, w1, w2 = cw_ref[0:1, :], cw_ref[1:2, :], cw_ref[2:3, :]
    bias = cb_ref[...]
    tm = h_ref.shape[0]

    def matmuls(r0):
        h = h_ref[r0:r0 + sub, :]
        return (jnp.dot(h, wg_ref[...], preferred_element_type=F32),
                jnp.dot(h, wu_ref[...], preferred_element_type=F32))

    def conv_act(gs, us, before):
        r8 = lax.broadcasted_iota(jnp.int32, (halo, gs.shape[1]), 0)
        g1 = pltpu.roll(gs, 1, axis=0)
        g2 = pltpu.roll(gs, 2, axis=0)
        head1 = jnp.where(r8 == 0, before[halo - 1:halo], g1[:halo])
        head2 = jnp.where(r8 == 0, before[halo - 2:halo - 1], jnp.where(r8 == 1, before[halo - 1:halo], g2[:halo]))
        g1 = jnp.concatenate([head1, g1[halo:]], axis=0)
        g2 = jnp.concatenate([head2, g2[halo:]], axis=0)
        conv = ((bias + g2 * w0) + g1 * w1) + gs * w2
        return (conv * jax.nn.sigmoid(conv)) * us

    if n_slab == 1:
        is_start = (i % tiles_per_seq) == 0

        @pl.when(is_start)
        def _():
            halo_ref[...] = prev_ref[0]

        @pl.when(jnp.logical_not(is_start))
        def _():
            halo_ref[...] = carry_ref[j]

    starts = list(range(0, tm, sub))
    pending = matmuls(starts[0])
    before = halo_ref[...] if n_slab == 1 else None
    for t, r0 in enumerate(starts):
        gate, up = pending
        if t + 1 < len(starts):
            pending = matmuls(starts[t + 1])
        for s0 in range(0, sub, rows):
            gs, us = gate[s0:s0 + rows], up[s0:s0 + rows]
            if n_slab > 1:
                before = prev_ref[(r0 + s0) // rows]
            act_ref[r0 + s0:r0 + s0 + rows, :] = conv_act(gs, us, before).astype(BF16)
            before = gs[rows - halo:rows]
            if n_slab > 1:
                tail_ref[(r0 + s0) // rows] = before
    if n_slab == 1:
        tail_ref[0] = before
        carry_ref[j] = before


def _ffn_up(h, wg, wu, conv_w, conv_b, prev, *, layer, seq):
    m, d_model = h.shape
    d_ff = wg.shape[-1]
    tm = _tile(m, FFN_ROW_TILE)
    tf = _tile(d_ff, FFN_COL_TILE)
    if tm >= seq:
        rows, n_slab, tiles_per_seq = seq, tm // seq, 1
        prev_spec = pl.BlockSpec((n_slab, SUBLANES, tf), lambda i, j: (i, 0, j))
    else:
        rows, n_slab, tiles_per_seq = tm, 1, seq // tm
        prev_spec = pl.BlockSpec((1, SUBLANES, tf), lambda i, j: (i // tiles_per_seq, 0, j))
    sub = _tile(tm, FFN_SUB_TILE)
    if n_slab == 1:
        rows = sub
    assert sub % rows == 0
    nj = d_ff // tf
    return pl.pallas_call(
        functools.partial(_ffn_up_kernel, rows=rows, n_slab=n_slab, tiles_per_seq=tiles_per_seq, sub=sub),
        grid=(m // tm, nj),
        in_specs=[pl.BlockSpec((tm, d_model), lambda i, j: (i, 0)),
                  pl.BlockSpec((None, d_model, tf), lambda i, j: (layer, 0, j)),
                  pl.BlockSpec((None, d_model, tf), lambda i, j: (layer, 0, j)),
                  pl.BlockSpec((None, CONV_W, tf), lambda i, j: (layer, 0, j)),
                  pl.BlockSpec((None, 1, tf), lambda i, j: (layer, 0, j)),
                  prev_spec],
        out_specs=[pl.BlockSpec((tm, tf), lambda i, j: (i, j)),
                   pl.BlockSpec((n_slab, SUBLANES, tf), lambda i, j: (i, 0, j))],
        out_shape=[jax.ShapeDtypeStruct((m, d_ff), BF16),
                   jax.ShapeDtypeStruct((m // tm * n_slab, SUBLANES, d_ff), F32)],
        scratch_shapes=[pltpu.VMEM((SUBLANES, tf), F32),
                        pltpu.VMEM((nj, SUBLANES, tf), F32)],
        compiler_params=_cparams(2), name="ffn_up",
    )(h, wg, wu, conv_w, conv_b, prev)


def _final_norm_kernel(x_ref, g_ref, o_ref):
    o_ref[...] = _rms(x_ref[...], g_ref[...], NORM_EPS)


def _final_norm(x, gain):
    m, d = x.shape
    tm = _tile(m, ROW_TILE)
    return pl.pallas_call(
        _final_norm_kernel, grid=(m // tm,),
        in_specs=[pl.BlockSpec((tm, d), lambda i: (i, 0)), pl.BlockSpec((1, d), lambda i: (0, 0))],
        out_specs=pl.BlockSpec((tm, d), lambda i: (i, 0)),
        out_shape=jax.ShapeDtypeStruct((m, d), F32),
        compiler_params=_cparams(1), name="final_norm",
    )(x, gain)


def _conv_ffn(y, h, seq, prev_state, layer, wg, wu, conv_w, conv_b, wd):
    n_seq = y.shape[0] // seq
    d_ff = wg.shape[-1]
    prev = jnp.concatenate([jnp.zeros((n_seq, SUBLANES - (CONV_W - 1), d_ff), F32), prev_state], axis=1)
    act, tails = _ffn_up(h, wg, wu, conv_w, conv_b, prev, layer=layer, seq=seq)
    y = _mm_res(act, wd, y, layer=layer)
    state = tails.reshape(n_seq, -1, SUBLANES, d_ff)[:, -1, SUBLANES - (CONV_W - 1):, :]
    return y, state


def kernel(x_prompt, x_sample, cache_mla_ckv, cache_mla_kpe, cache_swa_k, cache_swa_v, cache_diff_k,
           cache_diff_v, state_ffn_conv, norm_mix, norm_ffn, final_norm, mla_w_dq, mla_q_norm, mla_w_uq,
           mla_w_dkv, mla_kv_norm, mla_w_ukv, mla_w_o, swa_w_qkv, swa_sinks, swa_w_o, diff_w_qkv,
           diff_lambda_q1, diff_lambda_k1, diff_lambda_q2, diff_lambda_k2, diff_subln, diff_w_o,
           ffn_w_gate, ffn_w_up, ffn_conv_w, ffn_conv_b, ffn_w_down):
    bp, sp, d_model = x_prompt.shape
    bs, ss, _ = x_sample.shape
    depth = norm_mix.shape[0]
    past = cache_mla_ckv.shape[2]
    d_ff = ffn_w_gate.shape[2]
    q_rank = mla_w_dq.shape[2]
    kv_rank = mla_w_ukv.shape[1]

    yp = x_prompt.reshape(bp * sp, d_model)
    ys = x_sample.reshape(bs * ss, d_model)
    row_p = _tile(bp * sp, ROW_TILE)
    row_s = _tile(bs * ss, ROW_TILE)

    def tables(head_dim):
        return (_rope_tables(sp, 0, head_dim, row_p), _rope_tables(ss, past, head_dim, row_s))

    tab64, tab128 = tables(64), tables(128)
    outs_p = {k: [] for k in ("ckv", "kpe", "swk", "swv", "dk", "dv", "conv")}
    outs_s = {k: [] for k in ("ckv", "kpe", "swk", "swv", "dk", "dv", "conv")}

    ffn_w = (ffn_w_gate.astype(BF16), ffn_w_up.astype(BF16), ffn_conv_w, ffn_conv_b[:, None, :],
             ffn_w_down.astype(BF16))

    for i in range(depth):
        kind, j = i % N_MIXERS, i // N_MIXERS
        gain = norm_mix[i][None, :]
        ffn_gain = norm_ffn[i][None, :]
        if kind == 0:
            q_scale = (MLA_NOPE + MLA_ROPE) ** -0.5 * LOG2E
            pad = (-(q_rank + kv_rank + MLA_ROPE)) % LANES
            w_cat = jnp.concatenate([mla_w_dq[j], mla_w_dkv[j], jnp.zeros((d_model, pad), F32)],
                                    axis=1).astype(BF16)
            w_uq = mla_w_uq[j].reshape(q_rank, MLA_HEADS, MLA_NOPE + MLA_ROPE)
            w_uq = jnp.pad(w_uq, ((0, 0), (0, 0), (0, MLA_QK_PAD - MLA_NOPE - MLA_ROPE)))
            w_uq = w_uq.reshape(q_rank, MLA_HEADS * MLA_QK_PAD).astype(BF16)
            w_ukv = mla_w_ukv[j].reshape(kv_rank, MLA_HEADS, MLA_NOPE + MLA_V)
            wk = w_ukv[:, :, :MLA_NOPE].reshape(kv_rank, MLA_HEADS * MLA_NOPE).astype(BF16)
            wv = w_ukv[:, :, MLA_NOPE:].reshape(kv_rank, MLA_HEADS * MLA_V).astype(BF16)
            w_o = mla_w_o[j].astype(BF16)
            qn, kvn = mla_q_norm[j][None, :], mla_kv_norm[j][None, :]

            (cos_p, sin_p), (cos_s, sin_s) = tab64
            qa, ckv_p, kpe_p = _mla_down(yp, gain, w_cat, qn, kvn, cos_p, sin_p, q_rank, kv_rank)
            q = _mla_q_up(qa, w_uq, cos_p, sin_p, q_scale)
            k, vt = _mla_expand(ckv_p, kpe_p, wk, wv.T)
            o = _flash_attention(q, k, vt, batch=bp, seq=sp, n_kv_heads=MLA_HEADS, dk=MLA_QK_PAD,
                                 dv=MLA_V, groups=1, n_maps=1)
            yp, hp = _mm_res(o, w_o, yp, ffn_gain)

            qa, ckv_s, kpe_s = _mla_down(ys, gain, w_cat, qn, kvn, cos_s, sin_s, q_rank, kv_rank)
            q = _mla_q_up(qa, w_uq, cos_s, sin_s, q_scale)
            o = _mla_decode_attention(q, cache_mla_ckv.reshape(-1, kv_rank), cache_mla_kpe.reshape(-1, MLA_ROPE),
                                      ckv_s, kpe_s, wk, wv, layer=j, batch=bs, past=past, new=ss)
            ys, hs = _mm_res(o, w_o, ys, ffn_gain)
            outs_p["ckv"].append(ckv_p.reshape(bp, sp, kv_rank))
            outs_p["kpe"].append(kpe_p.reshape(bp, sp, MLA_ROPE))
            outs_s["ckv"].append(ckv_s.reshape(bs, ss, kv_rank))
            outs_s["kpe"].append(kpe_s.reshape(bs, ss, MLA_ROPE))
        elif kind == 1:
            qw, kw = SWA_HEADS * SWA_HEAD_DIM, SWA_KV_HEADS * SWA_HEAD_DIM
            w_qkv = swa_w_qkv[j].astype(BF16)
            w_o = swa_w_o[j].astype(BF16)
            sinks = swa_sinks[j]
            (cos_p, sin_p), (cos_s, sin_s) = tab64
            q_scale = SWA_HEAD_DIM ** -0.5 * LOG2E
            q, k, v, kb, vt = _qkv_proj(yp, gain, w_qkv, cos_p, sin_p, qw, kw, kw, SWA_HEAD_DIM,
                                        q_scale=q_scale, vt_head=(SWA_HEAD_DIM, LANES - SWA_HEAD_DIM),
                                        vt_blocked=False)
            o = _swa_attention(q, kb, vt, sinks, seq=sp)
            yp, hp = _mm_res(o, w_o, yp, ffn_gain)
            k3 = k.reshape(bp, sp, SWA_KV_HEADS, SWA_HEAD_DIM)
            v3 = v.reshape(bp, sp, SWA_KV_HEADS, SWA_HEAD_DIM)
            outs_p["swk"].append(k3[:, sp - WINDOW:])
            outs_p["swv"].append(v3[:, sp - WINDOW:])

            q, k, v, _, _ = _qkv_proj(ys, gain, w_qkv, cos_s, sin_s, qw, kw, kw, SWA_HEAD_DIM, q_scale=q_scale)
            kc = cache_swa_k[j].reshape(bs * WINDOW, kw)
            vc = cache_swa_v[j].reshape(bs * WINDOW, kw)
            o = _swa_decode_attention(q, kc, k, vc, v, sinks, nq=ss, na=WINDOW, nb=ss)
            ys, hs = _mm_res(o, w_o, ys, ffn_gain)
            k_all = jnp.concatenate([cache_swa_k[j], k.reshape(bs, ss, SWA_KV_HEADS, SWA_HEAD_DIM)], axis=1)
            v_all = jnp.concatenate([cache_swa_v[j], v.reshape(bs, ss, SWA_KV_HEADS, SWA_HEAD_DIM)], axis=1)
            outs_s["swk"].append(k_all[:, ss:])
            outs_s["swv"].append(v_all[:, ss:])
        else:
            lam_init = 0.8 - 0.6 * math.exp(-0.3 * i)
            q_scale = DIFF_HEAD_DIM ** -0.5 * LOG2E
            groups = DIFF_HEADS // DIFF_KV_HEADS
            qw = DIFF_HEADS * 2 * DIFF_HEAD_DIM
            kw = DIFF_KV_HEADS * 2 * DIFF_HEAD_DIM
            w_qkv = diff_w_qkv[j].astype(BF16)
            w_o = diff_w_o[j].astype(BF16)
            lam = jnp.stack([diff_lambda_q1[j], diff_lambda_k1[j], diff_lambda_q2[j], diff_lambda_k2[j]])
            subln = diff_subln[j][None, :]
            common = dict(n_kv_heads=DIFF_KV_HEADS, dk=DIFF_HEAD_DIM, dv=2 * DIFF_HEAD_DIM, groups=groups,
                          n_maps=2, lam=lam, subln=subln, lam_init=lam_init)
            (cos_p, sin_p), (cos_s, sin_s) = tab128
            q, k, v, kb, vt = _qkv_proj(yp, gain, w_qkv, cos_p, sin_p, qw, kw, kw, DIFF_HEAD_DIM,
                                        q_scale=q_scale, vt_head=(2 * DIFF_HEAD_DIM, ONES_ROWS))
            o = _flash_attention(q, kb, vt, batch=bp, seq=sp, **common)
            yp, hp = _mm_res(o, w_o, yp, ffn_gain)
            outs_p["dk"].append(k.reshape(bp, sp, DIFF_KV_HEADS, 2, DIFF_HEAD_DIM))
            outs_p["dv"].append(v.reshape(bp, sp, DIFF_KV_HEADS, 2 * DIFF_HEAD_DIM))

            q, k, v, _, _ = _qkv_proj(ys, gain, w_qkv, cos_s, sin_s, qw, kw, kw, DIFF_HEAD_DIM,
                                      q_scale=q_scale)
            o = _decode_attention(q, cache_diff_k[j].reshape(bs * past, kw),
                                  cache_diff_v[j].reshape(bs * past, kw), k, v,
                                  batch=bs, past=past, new=ss, **common)
            ys, hs = _mm_res(o, w_o, ys, ffn_gain)
            outs_s["dk"].append(k.reshape(bs, ss, DIFF_KV_HEADS, 2, DIFF_HEAD_DIM))
            outs_s["dv"].append(v.reshape(bs, ss, DIFF_KV_HEADS, 2 * DIFF_HEAD_DIM))

        yp, conv_p = _conv_ffn(yp, hp, sp, jnp.zeros((bp, CONV_W - 1, d_ff), F32), i, *ffn_w)
        ys, conv_s = _conv_ffn(ys, hs, ss, state_ffn_conv[i], i, *ffn_w)
        outs_p["conv"].append(conv_p)
        outs_s["conv"].append(conv_s)

    fg = final_norm[None, :]
    y_prompt = _final_norm(yp, fg).reshape(bp, sp, d_model)
    y_sample = _final_norm(ys, fg).reshape(bs, ss, d_model)
    order = ("ckv", "kpe", "swk", "swv", "dk", "dv", "conv")
    stack = lambda xs: xs[0][None] if len(xs) == 1 else jnp.stack(xs)
    return (y_prompt, y_sample) + tuple(stack(outs_p[k]) for k in order) + tuple(
        stack(outs_s[k]) for k in order)
```

```python
import functools
import math

import jax
import jax.numpy as jnp
from jax import lax
from jax.experimental import pallas as pl
from jax.experimental.pallas import tpu as pltpu

F32 = jnp.float32
BF16 = jnp.bfloat16

CHUNK = 64
ROPE_THETA = 10000.0
NORM_EPS = 1e-6
SUBLN_EPS = 1e-5
N_MIXERS = 3
MLA_HEADS = 16
MLA_NOPE = 128
MLA_ROPE = 64
MLA_V = 128
SWA_HEADS = 32
SWA_KV_HEADS = 4
SWA_HEAD_DIM = 64
WINDOW = 128
DIFF_HEADS = 8
DIFF_KV_HEADS = 4
DIFF_HEAD_DIM = 128
CONV_W = 3

LANES = 128
SUBLANES = 8
MXU_COLS = 256
MLA_QK_PAD = MXU_COLS
VMEM_LIMIT_BYTES = 56 * 1024 * 1024
LOG2E = 1.4426950408889634

ROW_TILE = 512
FFN_ROW_TILE = 1024
FFN_COL_TILE = 512
FFN_SUB_TILE = 256
ATTN_Q_TILE = 2048
ATTN_KV_TILE = 512
ATTN_CHAIN = 256
ATTN_LOOKAHEAD = 3
ONES_ROWS = 16
MLA_DECODE_KV_TILE = 1024
MM_TILE = 512
MM_WEIGHT_BLOCK_BYTES = 12 * 1024 * 1024
MM_SUB_TILE = 256
NORM_SUB_TILE = 128


def _cparams(n_axes):
    return pltpu.CompilerParams(dimension_semantics=("arbitrary",) * n_axes,
                                vmem_limit_bytes=VMEM_LIMIT_BYTES)


def _resident(shape):
    return pl.BlockSpec(shape, lambda *_: (0,) * len(shape), pipeline_mode=pl.Buffered(1))


def _tile(n, pref):
    t = min(n, pref)
    assert n % t == 0, (n, pref)
    return t


def _rms(x, g, eps):
    ms = jnp.mean(x * x, axis=-1, keepdims=True)
    return (x * lax.rsqrt(ms + eps)) * g


def _rope_slab(x, cos, sin, head_dim):
    if head_dim == LANES:
        swapped = pltpu.roll(x, LANES // 2, axis=1)
    else:
        half = head_dim // 2
        lane = lax.broadcasted_iota(jnp.int32, x.shape, 1)
        first = (lane & (head_dim - 1)) < half
        swapped = jnp.where(first, pltpu.roll(x, LANES - half, axis=1), pltpu.roll(x, half, axis=1))
    return x * cos + swapped * sin


def _rope_tables(seq_len, offset, head_dim, rows):
    half = head_dim // 2
    pos = jnp.arange(seq_len, dtype=F32) + offset
    inv = ROPE_THETA ** (-jnp.arange(0, head_dim, 2, dtype=F32) / head_dim)
    ang = pos[:, None] * inv[None, :]
    cos, sin = jnp.cos(ang), jnp.sin(ang)
    reps = LANES // head_dim
    cos_l = jnp.tile(jnp.concatenate([cos, cos], axis=1), (1, reps))
    sin_l = jnp.tile(jnp.concatenate([-sin, sin], axis=1), (1, reps))
    n = max(rows // seq_len, 1)
    return jnp.tile(cos_l, (n, 1)), jnp.tile(sin_l, (n, 1))


def _mla_down_kernel(x_ref, g_ref, w_ref, qn_ref, kvn_ref, cos_ref, sin_ref,
                     qa_ref, ckv_ref, kpe_ref, *, q_rank, kv_rank, sub):
    def project(r0):
        h = _rms(x_ref[r0:r0 + sub, :], g_ref[...], NORM_EPS).astype(BF16)
        return jnp.dot(h, w_ref[...], preferred_element_type=F32)

    starts = list(range(0, x_ref.shape[0], sub))
    pending = project(starts[0])
    for t, r0 in enumerate(starts):
        d = pending
        if t + 1 < len(starts):
            pending = project(starts[t + 1])
        rows = slice(r0, r0 + sub)
        qa_ref[rows, :] = _rms(d[:, :q_rank], qn_ref[...], NORM_EPS).astype(BF16)
        ckv_ref[rows, :] = _rms(d[:, q_rank:q_rank + kv_rank], kvn_ref[...], NORM_EPS)
        slab = d[:, q_rank + kv_rank:q_rank + kv_rank + LANES]
        kpe_ref[rows, :] = _rope_slab(slab, cos_ref[rows, :], sin_ref[rows, :], MLA_ROPE)[:, :MLA_ROPE]


def _mla_down(x, gain, w_cat, q_norm, kv_norm, cos, sin, q_rank, kv_rank):
    m, d_model = x.shape
    tm = _tile(m, ROW_TILE)
    n_tab = cos.shape[0] // tm
    wn = w_cat.shape[1]
    row = lambda i: (i, 0)
    fixed = lambda i: (0, 0)
    tab = lambda i: (i % n_tab, 0)
    return pl.pallas_call(
        functools.partial(_mla_down_kernel, q_rank=q_rank, kv_rank=kv_rank, sub=_tile(tm, NORM_SUB_TILE)),
        grid=(m // tm,),
        in_specs=[pl.BlockSpec((tm, d_model), row), pl.BlockSpec((1, d_model), fixed),
                  _resident((d_model, wn)), pl.BlockSpec((1, q_rank), fixed),
                  pl.BlockSpec((1, kv_rank), fixed), pl.BlockSpec((tm, LANES), tab),
                  pl.BlockSpec((tm, LANES), tab)],
        out_specs=[pl.BlockSpec((tm, q_rank), row), pl.BlockSpec((tm, kv_rank), row),
                   pl.BlockSpec((tm, MLA_ROPE), row)],
        out_shape=[jax.ShapeDtypeStruct((m, q_rank), BF16), jax.ShapeDtypeStruct((m, kv_rank), F32),
                   jax.ShapeDtypeStruct((m, MLA_ROPE), F32)],
        compiler_params=_cparams(1), name="mla_down",
    )(x, gain, w_cat, q_norm, kv_norm, cos, sin)


def _mla_q_up_kernel(qa_ref, w_ref, cos_ref, sin_ref, q_ref, *, q_scale):
    qa = qa_ref[...]
    cos, sin = cos_ref[...], sin_ref[...]
    for h in range(MLA_HEADS):
        c0 = h * MLA_QK_PAD
        d = jnp.dot(qa, w_ref[:, c0:c0 + MLA_QK_PAD], preferred_element_type=F32)
        q_ref[:, c0:c0 + LANES] = (d[:, :LANES] * q_scale).astype(BF16)
        q_ref[:, c0 + LANES:c0 + MLA_QK_PAD] = (_rope_slab(d[:, LANES:], cos, sin, MLA_ROPE) * q_scale).astype(BF16)


def _mla_q_up(qa, w_pad, cos, sin, q_scale):
    m, q_rank = qa.shape
    tm = _tile(m, ROW_TILE)
    n_tab = cos.shape[0] // tm
    n = w_pad.shape[1]
    row = lambda i: (i, 0)
    fixed = lambda i: (0, 0)
    tab = lambda i: (i % n_tab, 0)
    return pl.pallas_call(
        functools.partial(_mla_q_up_kernel, q_scale=q_scale), grid=(m // tm,),
        in_specs=[pl.BlockSpec((tm, q_rank), row), _resident((q_rank, n)),
                  pl.BlockSpec((tm, LANES), tab), pl.BlockSpec((tm, LANES), tab)],
        out_specs=pl.BlockSpec((tm, n), row),
        out_shape=jax.ShapeDtypeStruct((m, n), BF16),
        compiler_params=_cparams(1), name="mla_q_up",
    )(qa, w_pad, cos, sin)


def _mla_expand_kernel(ckv_ref, kpe_ref, wk_ref, wvt_ref, k_ref, vt_ref):
    c = ckv_ref[...].astype(BF16)
    kpe = kpe_ref[...].astype(BF16)
    tm = c.shape[0]
    dva = MLA_V + ONES_ROWS
    ones = jnp.ones((ONES_ROWS, tm), BF16)
    kpe_pad = jnp.concatenate([kpe, jnp.zeros((tm, MLA_QK_PAD - MLA_NOPE - MLA_ROPE), BF16)], axis=1)
    pair = MXU_COLS // MLA_NOPE

    def project(h0):
        kn = jnp.dot(c, wk_ref[:, h0 * MLA_NOPE:(h0 + pair) * MLA_NOPE], preferred_element_type=F32)
        vt = lax.dot_general(wvt_ref[h0 * MLA_V:(h0 + pair) * MLA_V, :], c, (((1,), (1,)), ((), ())),
                             preferred_element_type=F32)
        return kn, vt

    pending = project(0)
    for h0 in range(0, MLA_HEADS, pair):
        kn, vt = pending
        if h0 + pair < MLA_HEADS:
            pending = project(h0 + pair)
        for i in range(pair):
            h = h0 + i
            c0 = h * MLA_QK_PAD
            k_ref[:, c0:c0 + MLA_NOPE] = kn[:, i * MLA_NOPE:(i + 1) * MLA_NOPE].astype(BF16)
            k_ref[:, c0 + MLA_NOPE:c0 + MLA_QK_PAD] = kpe_pad
            vt_ref[0, h * dva:h * dva + MLA_V, :] = vt[i * MLA_V:(i + 1) * MLA_V].astype(BF16)
            vt_ref[0, h * dva + MLA_V:(h + 1) * dva, :] = ones


def _mla_expand(ckv, kpe, wk, wvt):
    m, kv_rank = ckv.shape
    tm = _tile(m, ATTN_KV_TILE)
    row = lambda i: (i, 0)
    nk, nva = MLA_HEADS * MLA_QK_PAD, MLA_HEADS * (MLA_V + ONES_ROWS)
    return pl.pallas_call(
        _mla_expand_kernel, grid=(m // tm,),
        in_specs=[pl.BlockSpec((tm, kv_rank), row), pl.BlockSpec((tm, MLA_ROPE), row),
                  _resident(wk.shape), _resident(wvt.shape)],
        out_specs=[pl.BlockSpec((tm, nk), row), pl.BlockSpec((1, nva, tm), lambda i: (i, 0, 0))],
        out_shape=[jax.ShapeDtypeStruct((m, nk), BF16), jax.ShapeDtypeStruct((m // tm, nva, tm), BF16)],
        compiler_params=_cparams(1), name="mla_expand",
    )(ckv, kpe, wk, wvt)


def _qkv_proj_kernel(x_ref, g_ref, w_ref, cos_ref, sin_ref, q_ref, k_ref, v_ref, kb_ref, vb_ref,
                     *, qw, kw, vw, head_dim, q_scale, vt_head, sub):
    vt_dst = None
    if vt_head:
        vt_dst = vb_ref.at[0] if len(vb_ref.shape) == 3 else vb_ref
    for s0 in range(0, x_ref.shape[0], sub):
        rows = slice(s0, s0 + sub)
        h = _rms(x_ref[rows, :], g_ref[...], NORM_EPS).astype(BF16)
        cos, sin = cos_ref[rows, :], sin_ref[rows, :]
        for n0 in range(0, qw + kw + vw, MXU_COLS):
            dd = jnp.dot(h, w_ref[:, n0:n0 + MXU_COLS], preferred_element_type=F32)
            for c0 in range(n0, n0 + MXU_COLS, LANES):
                d = dd[:, c0 - n0:c0 - n0 + LANES]
                if c0 < qw:
                    q_ref[rows, c0:c0 + LANES] = (_rope_slab(d, cos, sin, head_dim) * q_scale).astype(BF16)
                elif c0 < qw + kw:
                    r = _rope_slab(d, cos, sin, head_dim)
                    k_ref[rows, c0 - qw:c0 - qw + LANES] = r
                    kb_ref[rows, c0 - qw:c0 - qw + LANES] = r.astype(BF16)
                else:
                    c1 = c0 - qw - kw
                    v_ref[rows, c1:c1 + LANES] = d
                    if vt_head:
                        vd, ones_rows = vt_head
                        dt = d.T.astype(BF16)
                        for hv in range(c1 // vd, (c1 + LANES - 1) // vd + 1):
                            lo, hi = max(c1, hv * vd), min(c1 + LANES, (hv + 1) * vd)
                            r0 = hv * (vd + ones_rows) + lo - hv * vd
                            vt_dst[r0:r0 + hi - lo, rows] = dt[lo - c1:hi - c1]
                            if hi == (hv + 1) * vd:
                                vt_dst[r0 + hi - lo:r0 + hi - lo + ones_rows, rows] = jnp.ones((ones_rows, sub), BF16)
                    else:
                        vb_ref[rows, c1:c1 + LANES] = d.astype(BF16)


def _qkv_proj(x, gain, w, cos, sin, qw, kw, vw, head_dim, *, q_scale=1.0, vt_head=None, vt_blocked=True):
    m, d_model = x.shape
    tm = _tile(m, ATTN_KV_TILE if vt_head else ROW_TILE)
    n_tab = cos.shape[0] // tm
    row = lambda i: (i, 0)
    fixed = lambda i: (0, 0)
    tab = lambda i: (i % n_tab, 0)
    if vt_head:
        vwa = vw // vt_head[0] * (vt_head[0] + vt_head[1])
        if vt_blocked:
            vb_spec = pl.BlockSpec((1, vwa, tm), lambda i: (i, 0, 0))
            vb_shape = jax.ShapeDtypeStruct((m // tm, vwa, tm), BF16)
        else:
            vb_spec = pl.BlockSpec((vwa, tm), lambda i: (0, i))
            vb_shape = jax.ShapeDtypeStruct((vwa, m), BF16)
    else:
        vb_spec = pl.BlockSpec((tm, vw), row)
        vb_shape = jax.ShapeDtypeStruct((m, vw), BF16)
    return pl.pallas_call(
        functools.partial(_qkv_proj_kernel, qw=qw, kw=kw, vw=vw, head_dim=head_dim, q_scale=q_scale,
                          vt_head=vt_head, sub=_tile(tm, MM_SUB_TILE)),
        grid=(m // tm,),
        in_specs=[pl.BlockSpec((tm, d_model), row), pl.BlockSpec((1, d_model), fixed),
                  _resident(w.shape), pl.BlockSpec((tm, LANES), tab),
                  pl.BlockSpec((tm, LANES), tab)],
        out_specs=[pl.BlockSpec((tm, qw), row), pl.BlockSpec((tm, kw), row), pl.BlockSpec((tm, vw), row),
                   pl.BlockSpec((tm, kw), row), vb_spec],
        out_shape=[jax.ShapeDtypeStruct((m, qw), BF16), jax.ShapeDtypeStruct((m, kw), F32),
                   jax.ShapeDtypeStruct((m, vw), F32), jax.ShapeDtypeStruct((m, kw), BF16), vb_shape],
        compiler_params=_cparams(1), name="qkv_proj",
    )(x, gain, w, cos, sin)


def _mm_res_kernel(*refs, sub, with_norm):
    if with_norm:
        a_ref, w_ref, r_ref, g_ref, o_ref, h_ref = refs
    else:
        a_ref, w_ref, r_ref, o_ref = refs
    starts = list(range(0, a_ref.shape[0], sub))
    pending = jnp.dot(a_ref[0:sub, :], w_ref[...], preferred_element_type=F32)
    for t, r0 in enumerate(starts):
        d = pending
        if t + 1 < len(starts):
            r1 = starts[t + 1]
            pending = jnp.dot(a_ref[r1:r1 + sub, :], w_ref[...], preferred_element_type=F32)
        y = r_ref[r0:r0 + sub, :] + d
        o_ref[r0:r0 + sub, :] = y
        if with_norm:
            h_ref[r0:r0 + sub, :] = _rms(y, g_ref[...], NORM_EPS).astype(BF16)


def _mm_res(a, w, res, norm_gain=None, layer=None):
    m, k = a.shape
    n = w.shape[-1]
    tm = _tile(m, MM_TILE)
    tn = n
    while tn * k * w.dtype.itemsize > MM_WEIGHT_BLOCK_BYTES and tn % (2 * MXU_COLS) == 0:
        tn //= 2
    with_norm = norm_gain is not None
    assert not with_norm or tn == n
    tile = pl.BlockSpec((tm, tn), lambda j, i: (i, j))
    if layer is None:
        w_spec = pl.BlockSpec((k, tn), lambda j, i: (0, j))
    else:
        w_spec = pl.BlockSpec((None, k, tn), lambda j, i: (layer, 0, j))
    in_specs = [pl.BlockSpec((tm, k), lambda j, i: (i, 0)), w_spec, tile]
    out_specs, out_shape, args = tile, jax.ShapeDtypeStruct((m, n), F32), [a, w, res]
    if with_norm:
        in_specs.append(pl.BlockSpec((1, n), lambda j, i: (0, 0)))
        out_specs, out_shape = [tile, tile], [out_shape, jax.ShapeDtypeStruct((m, n), BF16)]
        args.append(norm_gain)
    return pl.pallas_call(
        functools.partial(_mm_res_kernel, sub=_tile(tm, MM_SUB_TILE), with_norm=with_norm),
        grid=(n // tn, m // tm), in_specs=in_specs, out_specs=out_specs, out_shape=out_shape,
        compiler_params=_cparams(2), name="mm_res",
    )(*args)


def _diff_lambda(lam_ref, lam_init):
    lam = lam_ref[...]
    a = jnp.sum(lam[0:1] * lam[1:2], axis=-1, keepdims=True)
    b = jnp.sum(lam[2:3] * lam[3:4], axis=-1, keepdims=True)
    return jnp.exp(a) - jnp.exp(b) + lam_init


def _combine_heads(o_list, o_ref, rows, lam_ref, subln_ref, *, n_maps, dv, lam_init):
    groups = len(o_list) // n_maps
    for g in range(groups):
        if n_maps == 1:
            o = o_list[g]
        else:
            lam = _diff_lambda(lam_ref, lam_init)
            o = o_list[2 * g] - lam * o_list[2 * g + 1]
            o = _rms(o, subln_ref[...], SUBLN_EPS) * (1.0 - lam_init)
        o_ref[rows, g * dv:(g + 1) * dv] = o.astype(o_ref.dtype)


def _flash_kernel(*refs, tq, tk, cw, dk, dv, groups, n_maps, lam_init):
    if n_maps == 2:
        lam_ref, subln_ref, q_ref, k_ref, vt_ref, o_ref, m_ref, acc_ref, st_ref, mx_ref = refs
    else:
        q_ref, k_ref, vt_ref, o_ref, m_ref, acc_ref, st_ref, mx_ref = refs
        lam_ref = subln_ref = None
    n_sub = groups * n_maps
    n_chain = tq // cw
    kv_per_q = tq // tk
    qi = pl.program_id(2)
    nt = (((1,), (1,)), ((), ()))

    m_ref[...] = jnp.full(m_ref.shape, -jnp.inf, F32)
    acc_ref[...] = jnp.zeros(acc_ref.shape, F32)

    def scores(item, k_blks, slot):
        u, n, k_lo, bi, nk = item
        mi = u % n_maps
        q_n = q_ref[n * cw:(n + 1) * cw, u * dk:(u + 1) * dk]
        st = lax.dot_general(k_blks[bi][:nk, mi * dk:(mi + 1) * dk], q_n, nt, preferred_element_type=F32)
        if k_lo is not None:
            kc = (lax.broadcasted_iota(jnp.int32, st.shape, 0) + k_lo) // CHUNK
            qc = (lax.broadcasted_iota(jnp.int32, st.shape, 1) + n * cw) // CHUNK
            st = jnp.where(kc <= qc, st, -jnp.inf)
        st_ref[slot, 0:nk, :] = st
        mx_ref[slot] = jnp.max(st, axis=0, keepdims=True)

    def update(item, slot, vt_blks):
        u, n, _, bi, nk = item
        st = st_ref[slot, 0:nk, :]
        idx = u * n_chain + n
        m_prev = m_ref[idx]
        m_new = jnp.maximum(m_prev, mx_ref[slot])
        alpha = jnp.exp2(m_prev - m_new)
        pt = jnp.exp2(st - m_new).astype(BF16)
        acc_ref[idx] = acc_ref[idx] * alpha + jnp.dot(vt_blks[bi][:, :nk], pt, preferred_element_type=F32)
        m_ref[idx] = m_new

    def run_chains(items, first_block):
        blocks = sorted({it[3] for it in items})
        k_blks = {bi: k_ref[pl.ds(pl.multiple_of((first_block + bi) * tk, tk), tk), :] for bi in blocks}
        vt_blks = {bi: vt_ref[first_block + bi] for bi in blocks}
        n_slots = ATTN_LOOKAHEAD + 1
        for t in range(min(ATTN_LOOKAHEAD, len(items))):
            scores(items[t], k_blks, t % n_slots)
        for t, item in enumerate(items):
            if t + ATTN_LOOKAHEAD < len(items):
                scores(items[t + ATTN_LOOKAHEAD], k_blks, (t + ATTN_LOOKAHEAD) % n_slots)
            update(item, t % n_slots, vt_blks)

    def body(j, carry):
        items = [(u, n, None, bi, tk) for bi in range(kv_per_q) for n in range(n_chain) for u in range(n_sub)]
        run_chains(items, j * kv_per_q)
        return carry

    lax.fori_loop(0, qi, body, 0)

    items = [(u, n, bi * tk if (bi + 1) * tk > n * cw else None, bi, min(tk, (n + 1) * cw - bi * tk))
             for bi in range(kv_per_q) for n in range(n_chain) if bi * tk < (n + 1) * cw
             for u in range(n_sub)]
    run_chains(items, qi * kv_per_q)

    for n in range(n_chain):
        outs = []
        for u in range(n_sub):
            a = acc_ref[u * n_chain + n]
            outs.append((a[:dv] / a[dv:dv + 1]).T)
        _combine_heads(outs, o_ref, slice(n * cw, (n + 1) * cw), lam_ref, subln_ref,
                       n_maps=n_maps, dv=dv, lam_init=lam_init)


def _flash_attention(q, k, vt, *, batch, seq, n_kv_heads, dk, dv, groups, n_maps,
                     lam=None, subln=None, lam_init=0.0):
    tk = vt.shape[2]
    dva = dv + ONES_ROWS
    n_sub = groups * n_maps
    tq = _tile(seq, max(ATTN_Q_TILE if n_sub == 1 else ATTN_Q_TILE // 2, tk))
    cw = _tile(tq, ATTN_CHAIN)
    assert tq % tk == 0 and tk % CHUNK == 0 and cw % CHUNK == 0
    nq, nk = seq // tq, seq // tk
    n_chain = tq // cw
    kern = functools.partial(_flash_kernel, tq=tq, tk=tk, cw=cw, dk=dk, dv=dv, groups=groups,
                             n_maps=n_maps, lam_init=lam_init)
    in_specs = [pl.BlockSpec((tq, n_sub * dk), lambda b, h, i: (b * nq + i, h)),
                pl.BlockSpec((seq, n_maps * dk), lambda b, h, i: (b, h)),
                pl.BlockSpec((nk, dva, tk), lambda b, h, i: (b, h, 0))]
    args = [q, k, vt]
    if n_maps == 2:
        in_specs = [pl.BlockSpec(lam.shape, lambda b, h, i: (0, 0)),
                    pl.BlockSpec(subln.shape, lambda b, h, i: (0, 0))] + in_specs
        args = [lam, subln] + args
    return pl.pallas_call(
        kern, grid=(batch, n_kv_heads, nq), in_specs=in_specs,
        out_specs=pl.BlockSpec((tq, groups * dv), lambda b, h, i: (b * nq + i, h)),
        out_shape=jax.ShapeDtypeStruct((batch * seq, n_kv_heads * groups * dv), BF16),
        scratch_shapes=[pltpu.VMEM((n_sub * n_chain, 1, cw), F32),
                        pltpu.VMEM((n_sub * n_chain, dva, cw), F32),
                        pltpu.VMEM((ATTN_LOOKAHEAD + 1, tk, cw), F32),
                        pltpu.VMEM((ATTN_LOOKAHEAD + 1, 1, cw), F32)],
        compiler_params=_cparams(3), name="flash_attention",
    )(*args)


def _decode_kernel(*refs, dk, dv, groups, n_maps, lam_init):
    if n_maps == 2:
        lam_ref, subln_ref, q_ref, kc_ref, vc_ref, kn_ref, vn_ref, o_ref = refs
    else:
        q_ref, kc_ref, vc_ref, kn_ref, vn_ref, o_ref = refs
        lam_ref = subln_ref = None
    nt = (((1,), (1,)), ((), ()))
    vc = vc_ref[...].astype(BF16)
    vn = vn_ref[...].astype(BF16)
    outs = []
    for u in range(groups * n_maps):
        mi = u % n_maps
        q = q_ref[:, u * dk:(u + 1) * dk]
        kc = kc_ref[:, mi * dk:(mi + 1) * dk].astype(BF16)
        kn = kn_ref[:, mi * dk:(mi + 1) * dk].astype(BF16)
        s1 = lax.dot_general(q, kc, nt, preferred_element_type=F32)
        s2 = lax.dot_general(q, kn, nt, preferred_element_type=F32)
        m = jnp.maximum(jnp.max(s1, axis=1, keepdims=True), jnp.max(s2, axis=1, keepdims=True))
        p1 = jnp.exp2(s1 - m)
        p2 = jnp.exp2(s2 - m)
        l = jnp.sum(p1, axis=1, keepdims=True) + jnp.sum(p2, axis=1, keepdims=True)
        o = (jnp.dot(p1.astype(BF16), vc, preferred_element_type=F32)
             + jnp.dot(p2.astype(BF16), vn, preferred_element_type=F32))
        outs.append(o / l)
    _combine_heads(outs, o_ref, slice(None), lam_ref, subln_ref, n_maps=n_maps, dv=dv, lam_init=lam_init)


def _decode_attention(q, kc, vc, kn, vn, *, batch, past, new, n_kv_heads, dk, dv, groups, n_maps,
                      lam=None, subln=None, lam_init=0.0):
    n_sub = groups * n_maps
    kern = functools.partial(_decode_kernel, dk=dk, dv=dv, groups=groups, n_maps=n_maps, lam_init=lam_init)
    bh = lambda b, h: (b, h)
    in_specs = [pl.BlockSpec((new, n_sub * dk), bh), pl.BlockSpec((past, n_maps * dk), bh),
                pl.BlockSpec((past, dv), bh), pl.BlockSpec((new, n_maps * dk), bh),
                pl.BlockSpec((new, dv), bh)]
    args = [q, kc, vc, kn, vn]
    if n_maps == 2:
        in_specs = [pl.BlockSpec(lam.shape, lambda b, h: (0, 0)),
                    pl.BlockSpec(subln.shape, lambda b, h: (0, 0))] + in_specs
        args = [lam, subln] + args
    return pl.pallas_call(
        kern, grid=(batch, n_kv_heads), in_specs=in_specs,
        out_specs=pl.BlockSpec((new, groups * dv), bh),
        out_shape=jax.ShapeDtypeStruct((batch * new, n_kv_heads * groups * dv), BF16),
        compiler_params=_cparams(2), name="decode_attention",
    )(*args)


def _mla_decode_kernel(q_ref, ckv_c_ref, kpe_c_ref, ckv_n_ref, kpe_n_ref, wk_ref, wv_ref, o_ref,
                       kcat_ref, qcat_ref, *, block):
    new = q_ref.shape[0]
    past, rank = ckv_c_ref.shape
    nt = (((1,), (1,)), ((), ()))
    kcat_ref[0:past, 0:rank] = ckv_c_ref[...].astype(BF16)
    kcat_ref[past:past + new, 0:rank] = ckv_n_ref[...].astype(BF16)
    kcat_ref[0:past, rank:rank + MLA_ROPE] = kpe_c_ref[...].astype(BF16)
    kcat_ref[past:past + new, rank:rank + MLA_ROPE] = kpe_n_ref[...].astype(BF16)
    kcat_ref[:, rank + MLA_ROPE:rank + LANES] = jnp.zeros((past + new, LANES - MLA_ROPE), BF16)
    for h in range(MLA_HEADS):
        c0 = h * MLA_QK_PAD
        q_lat = lax.dot_general(q_ref[:, c0:c0 + MLA_NOPE], wk_ref[:, h * MLA_NOPE:(h + 1) * MLA_NOPE], nt,
                                preferred_element_type=F32)
        qcat_ref[h * new:(h + 1) * new, 0:rank] = q_lat.astype(BF16)
        qcat_ref[h * new:(h + 1) * new, rank:rank + LANES] = q_ref[:, c0 + MLA_NOPE:c0 + MLA_QK_PAD]
    q = qcat_ref[...]
    rows = MLA_HEADS * new
    m = jnp.full((rows, 1), -jnp.inf, F32)
    l = jnp.zeros((rows, 1), F32)
    acc = jnp.zeros((rows, rank), F32)
    starts = list(range(0, past, block)) + [past]
    for k0 in starts:
        nk = min(block, past - k0) if k0 < past else new
        kb = kcat_ref[k0:k0 + nk, :]
        s = lax.dot_general(q, kb, nt, preferred_element_type=F32)
        m_new = jnp.maximum(m, jnp.max(s, axis=1, keepdims=True))
        alpha = jnp.exp2(m - m_new)
        p = jnp.exp2(s - m_new)
        l = alpha * l + jnp.sum(p, axis=1, keepdims=True)
        acc = alpha * acc + jnp.dot(p.astype(BF16), kb[:, 0:rank], preferred_element_type=F32)
        m = m_new
    o_lat = (acc / l).astype(BF16)
    for h in range(MLA_HEADS):
        o_ref[:, h * MLA_V:(h + 1) * MLA_V] = jnp.dot(
            o_lat[h * new:(h + 1) * new], wv_ref[:, h * MLA_V:(h + 1) * MLA_V],
            preferred_element_type=F32).astype(o_ref.dtype)


def _mla_decode_attention(q, ckv_cache, kpe_cache, ckv_new, kpe_new, wk, wv, *, layer, batch, past, new):
    rank = ckv_cache.shape[1]
    per = lambda b: (b, 0)
    cached = lambda b: (layer * batch + b, 0)
    return pl.pallas_call(
        functools.partial(_mla_decode_kernel, block=_tile(past, MLA_DECODE_KV_TILE)), grid=(batch,),
        in_specs=[pl.BlockSpec((new, q.shape[1]), per), pl.BlockSpec((past, rank), cached),
                  pl.BlockSpec((past, MLA_ROPE), cached), pl.BlockSpec((new, rank), per),
                  pl.BlockSpec((new, MLA_ROPE), per), _resident(wk.shape), _resident(wv.shape)],
        out_specs=pl.BlockSpec((new, MLA_HEADS * MLA_V), per),
        out_shape=jax.ShapeDtypeStruct((batch * new, MLA_HEADS * MLA_V), BF16),
        scratch_shapes=[pltpu.VMEM((past + new, rank + LANES), BF16),
                        pltpu.VMEM((MLA_HEADS * new, rank + LANES), BF16)],
        compiler_params=_cparams(1), name="mla_decode_attention",
    )(q, ckv_cache, kpe_cache, ckv_new, kpe_new, wk, wv)


def _swa_decode_kernel(sinks_ref, q_ref, ka_ref, kb_ref, va_ref, vb_ref, o_ref):
    nq = q_ref.shape[0]
    group = SWA_HEADS // SWA_KV_HEADS
    d = SWA_HEAD_DIM
    nt = (((1,), (1,)), ((), ()))
    for h in range(SWA_KV_HEADS):
        hs = slice(h * d, (h + 1) * d)
        k = jnp.concatenate([ka_ref[:, hs], kb_ref[:, hs]], axis=0).astype(BF16)
        v = jnp.concatenate([va_ref[:, hs], vb_ref[:, hs]], axis=0).astype(BF16)
        q = jnp.concatenate([q_ref[:, (h * group + g) * d:(h * group + g + 1) * d] for g in range(group)],
                            axis=0)
        sink = jnp.concatenate([jnp.full((nq, 1), sinks_ref[h * group + g] * LOG2E, F32)
                                for g in range(group)], axis=0)
        s = lax.dot_general(q, k, nt, preferred_element_type=F32)
        m = jnp.maximum(jnp.max(s, axis=1, keepdims=True), sink)
        p = jnp.exp2(s - m)
        l = jnp.sum(p, axis=1, keepdims=True) + jnp.exp2(sink - m)
        o = jnp.dot(p.astype(BF16), v, preferred_element_type=F32) / l
        for g in range(0, group, 2):
            pair = jnp.concatenate([o[g * nq:(g + 1) * nq], o[(g + 1) * nq:(g + 2) * nq]], axis=1)
            c0 = (h * group + g) * d
            o_ref[:, c0:c0 + 2 * d] = pair.astype(o_ref.dtype)


def _swa_decode_attention(q, ka, kb, va, vb, sinks, *, nq, na, nb):
    m, qw = q.shape
    kvw = ka.shape[1]
    cur = lambda t: (t, 0)
    return pl.pallas_call(
        _swa_decode_kernel, grid=(m // nq,),
        in_specs=[pl.BlockSpec(memory_space=pltpu.SMEM), pl.BlockSpec((nq, qw), cur),
                  pl.BlockSpec((na, kvw), cur), pl.BlockSpec((nb, kvw), cur),
                  pl.BlockSpec((na, kvw), cur), pl.BlockSpec((nb, kvw), cur)],
        out_specs=pl.BlockSpec((nq, qw), cur),
        out_shape=jax.ShapeDtypeStruct((m, qw), BF16),
        compiler_params=_cparams(1), name="swa_decode_attention",
    )(sinks, q, ka, kb, va, vb)


def _swa_kernel(sinks_ref, q_ref, ka_ref, kb_ref, vta_ref, vtb_ref, o_ref, st_ref, *, blocks_per_seq):
    nq, na = q_ref.shape[0], ka_ref.shape[0]
    nk = na + kb_ref.shape[0]
    group = SWA_HEADS // SWA_KV_HEADS
    d = SWA_HEAD_DIM
    vrows = vta_ref.shape[0] // SWA_KV_HEADS
    nt = (((1,), (1,)), ((), ()))
    kc = lax.broadcasted_iota(jnp.int32, (nk, group * nq), 0) // CHUNK
    qc = (lax.broadcasted_iota(jnp.int32, (nk, group * nq), 1) % nq) // CHUNK
    first = (pl.program_id(0) % blocks_per_seq) == 0
    lo = jnp.where(first, na // CHUNK, 0)
    valid = (kc >= qc) & (kc <= qc + na // CHUNK) & (kc >= lo)

    def scores(h):
        hs = slice(h * d, (h + 1) * d)
        k = jnp.concatenate([ka_ref[:, hs], kb_ref[:, hs]], axis=0)
        q = jnp.concatenate([q_ref[:, (h * group + g) * d:(h * group + g + 1) * d] for g in range(group)],
                            axis=0)
        st = lax.dot_general(k, q, nt, preferred_element_type=F32)
        st_ref[h % 2] = jnp.where(valid, st, -jnp.inf)

    scores(0)
    for h in range(SWA_KV_HEADS):
        if h + 1 < SWA_KV_HEADS:
            scores(h + 1)
        st = st_ref[h % 2]
        sink = jnp.concatenate([jnp.full((1, nq), sinks_ref[h * group + g] * LOG2E, F32)
                                for g in range(group)], axis=1)
        m = jnp.maximum(jnp.max(st, axis=0, keepdims=True), sink)
        pt = jnp.exp2(st - m).astype(BF16)
        vt = jnp.concatenate([vta_ref[h * vrows:(h + 1) * vrows, :], vtb_ref[h * vrows:(h + 1) * vrows, :]],
                             axis=1)
        ot = jnp.dot(vt, pt, preferred_element_type=F32)
        l = ot[d:d + 1] + jnp.exp2(sink - m)
        o = ot[:d] / l
        for g in range(0, group, 2):
            pair = jnp.concatenate([o[:, g * nq:(g + 1) * nq], o[:, (g + 1) * nq:(g + 2) * nq]], axis=0)
            c0 = (h * group + g) * d
            o_ref[:, c0:c0 + 2 * d] = pair.T.astype(o_ref.dtype)


def _swa_attention(q, k, vt, sinks, *, seq):
    m, qw = q.shape
    kvw = k.shape[1]
    nq = WINDOW
    cur = lambda t: (t, 0)
    prev = lambda t: (jnp.maximum(t - 1, 0), 0)
    cur_t = lambda t: (0, t)
    prev_t = lambda t: (0, jnp.maximum(t - 1, 0))
    return pl.pallas_call(
        functools.partial(_swa_kernel, blocks_per_seq=seq // nq),
        grid=(m // nq,),
        in_specs=[pl.BlockSpec(memory_space=pltpu.SMEM), pl.BlockSpec((nq, qw), cur),
                  pl.BlockSpec((nq, kvw), prev), pl.BlockSpec((nq, kvw), cur),
                  pl.BlockSpec((vt.shape[0], nq), prev_t), pl.BlockSpec((vt.shape[0], nq), cur_t)],
        out_specs=pl.BlockSpec((nq, qw), cur),
        out_shape=jax.ShapeDtypeStruct((m, qw), BF16),
        scratch_shapes=[pltpu.VMEM((2, 2 * nq, SWA_HEADS // SWA_KV_HEADS * nq), F32)],
        compiler_params=_cparams(1), name="swa_attention",
    )(sinks, q, k, k, vt, vt)


def _ffn_up_kernel(h_ref, wg_ref, wu_ref, cw_ref, cb_ref, prev_ref, act_ref, tail_ref,
                   halo_ref, carry_ref, *, rows, n_slab, tiles_per_seq, sub):
    i, j = pl.program_id(0), pl.program_id(1)
    halo = SUBLANES

    w0, w1, w2 = cw_ref[0:1, :], cw_ref[1:2, :], cw_ref[2:3, :]
    bias = cb_ref[...]
    tm = h_ref.shape[0]

    def matmuls(r0):
        h = h_ref[r0:r0 + sub, :]
        return (jnp.dot(h, wg_ref[...], preferred_element_type=F32),
                jnp.dot(h, wu_ref[...], preferred_element_type=F32))

    def conv_act(gs, us, before):
        r8 = lax.broadcasted_iota(jnp.int32, (halo, gs.shape[1]), 0)
        g1 = pltpu.roll(gs, 1, axis=0)
        g2 = pltpu.roll(gs, 2, axis=0)
        head1 = jnp.where(r8 == 0, before[halo - 1:halo], g1[:halo])
        head2 = jnp.where(r8 == 0, before[halo - 2:halo - 1], jnp.where(r8 == 1, before[halo - 1:halo], g2[:halo]))
        g1 = jnp.concatenate([head1, g1[halo:]], axis=0)
        g2 = jnp.concatenate([head2, g2[halo:]], axis=0)
        conv = ((bias + g2 * w0) + g1 * w1) + gs * w2
        return (conv * jax.nn.sigmoid(conv)) * us

    if n_slab == 1:
        is_start = (i % tiles_per_seq) == 0

        @pl.when(is_start)
        def _():
            halo_ref[...] = prev_ref[0]

        @pl.when(jnp.logical_not(is_start))
        def _():
            halo_ref[...] = carry_ref[j]

    starts = list(range(0, tm, sub))
    pending = matmuls(starts[0])
    before = halo_ref[...] if n_slab == 1 else None
    for t, r0 in enumerate(starts):
        gate, up = pending
        if t + 1 < len(starts):
            pending = matmuls(starts[t + 1])
        for s0 in range(0, sub, rows):
            gs, us = gate[s0:s0 + rows], up[s0:s0 + rows]
            if n_slab > 1:
                before = prev_ref[(r0 + s0) // rows]
            act_ref[r0 + s0:r0 + s0 + rows, :] = conv_act(gs, us, before).astype(BF16)
            before = gs[rows - halo:rows]
            if n_slab > 1:
                tail_ref[(r0 + s0) // rows] = before
    if n_slab == 1:
        tail_ref[0] = before
        carry_ref[j] = before


def _ffn_up(h, wg, wu, conv_w, conv_b, prev, *, layer, seq):
    m, d_model = h.shape
    d_ff = wg.shape[-1]
    tm = _tile(m, FFN_ROW_TILE)
    tf = _tile(d_ff, FFN_COL_TILE)
    if tm >= seq:
        rows, n_slab, tiles_per_seq = seq, tm // seq, 1
        prev_spec = pl.BlockSpec((n_slab, SUBLANES, tf), lambda i, j: (i, 0, j))
    else:
        rows, n_slab, tiles_per_seq = tm, 1, seq // tm
        prev_spec = pl.BlockSpec((1, SUBLANES, tf), lambda i, j: (i // tiles_per_seq, 0, j))
    sub = _tile(tm, FFN_SUB_TILE)
    if n_slab == 1:
        rows = sub
    assert sub % rows == 0
    nj = d_ff // tf
    return pl.pallas_call(
        functools.partial(_ffn_up_kernel, rows=rows, n_slab=n_slab, tiles_per_seq=tiles_per_seq, sub=sub),
        grid=(m // tm, nj),
        in_specs=[pl.BlockSpec((tm, d_model), lambda i, j: (i, 0)),
                  pl.BlockSpec((None, d_model, tf), lambda i, j: (layer, 0, j)),
                  pl.BlockSpec((None, d_model, tf), lambda i, j: (layer, 0, j)),
                  pl.BlockSpec((None, CONV_W, tf), lambda i, j: (layer, 0, j)),
                  pl.BlockSpec((None, 1, tf), lambda i, j: (layer, 0, j)),
                  prev_spec],
        out_specs=[pl.BlockSpec((tm, tf), lambda i, j: (i, j)),
                   pl.BlockSpec((n_slab, SUBLANES, tf), lambda i, j: (i, 0, j))],
        out_shape=[jax.ShapeDtypeStruct((m, d_ff), BF16),
                   jax.ShapeDtypeStruct((m // tm * n_slab, SUBLANES, d_ff), F32)],
        scratch_shapes=[pltpu.VMEM((SUBLANES, tf), F32),
                        pltpu.VMEM((nj, SUBLANES, tf), F32)],
        compiler_params=_cparams(2), name="ffn_up",
    )(h, wg, wu, conv_w, conv_b, prev)


def _final_norm_kernel(x_ref, g_ref, o_ref):
    o_ref[...] = _rms(x_ref[...], g_ref[...], NORM_EPS)


def _final_norm(x, gain):
    m, d = x.shape
    tm = _tile(m, ROW_TILE)
    return pl.pallas_call(
        _final_norm_kernel, grid=(m // tm,),
        in_specs=[pl.BlockSpec((tm, d), lambda i: (i, 0)), pl.BlockSpec((1, d), lambda i: (0, 0))],
        out_specs=pl.BlockSpec((tm, d), lambda i: (i, 0)),
        out_shape=jax.ShapeDtypeStruct((m, d), F32),
        compiler_params=_cparams(1), name="final_norm",
    )(x, gain)


def _conv_ffn(y, h, seq, prev_state, layer, wg, wu, conv_w, conv_b, wd):
    n_seq = y.shape[0] // seq
    d_ff = wg.shape[-1]
    prev = jnp.concatenate([jnp.zeros((n_seq, SUBLANES - (CONV_W - 1), d_ff), F32), prev_state], axis=1)
    act, tails = _ffn_up(h, wg, wu, conv_w, conv_b, prev, layer=layer, seq=seq)
    y = _mm_res(act, wd, y, layer=layer)
    state = tails.reshape(n_seq, -1, SUBLANES, d_ff)[:, -1, SUBLANES - (CONV_W - 1):, :]
    return y, state


def kernel(x_prompt, x_sample, cache_mla_ckv, cache_mla_kpe, cache_swa_k, cache_swa_v, cache_diff_k,
           cache_diff_v, state_ffn_conv, norm_mix, norm_ffn, final_norm, mla_w_dq, mla_q_norm, mla_w_uq,
           mla_w_dkv, mla_kv_norm, mla_w_ukv, mla_w_o, swa_w_qkv, swa_sinks, swa_w_o, diff_w_qkv,
           diff_lambda_q1, diff_lambda_k1, diff_lambda_q2, diff_lambda_k2, diff_subln, diff_w_o,
           ffn_w_gate, ffn_w_up, ffn_conv_w, ffn_conv_b, ffn_w_down):
    bp, sp, d_model = x_prompt.shape
    bs, ss, _ = x_sample.shape
    depth = norm_mix.shape[0]
    past = cache_mla_ckv.shape[2]
    d_ff = ffn_w_gate.shape[2]
    q_rank = mla_w_dq.shape[2]
    kv_rank = mla_w_ukv.shape[1]

    yp = x_prompt.reshape(bp * sp, d_model)
    ys = x_sample.reshape(bs * ss, d_model)
    row_p = _tile(bp * sp, ROW_TILE)
    row_s = _tile(bs * ss, ROW_TILE)

    def tables(head_dim):
        return (_rope_tables(sp, 0, head_dim, row_p), _rope_tables(ss, past, head_dim, row_s))

    tab64, tab128 = tables(64), tables(128)
    outs_p = {k: [] for k in ("ckv", "kpe", "swk", "swv", "dk", "dv", "conv")}
    outs_s = {k: [] for k in ("ckv", "kpe", "swk", "swv", "dk", "dv", "conv")}

    ffn_w = (ffn_w_gate.astype(BF16), ffn_w_up.astype(BF16), ffn_conv_w, ffn_conv_b[:, None, :],
             ffn_w_down.astype(BF16))

    for i in range(depth):
        kind, j = i % N_MIXERS, i // N_MIXERS
        gain = norm_mix[i][None, :]
        ffn_gain = norm_ffn[i][None, :]
        if kind == 0:
            q_scale = (MLA_NOPE + MLA_ROPE) ** -0.5 * LOG2E
            pad = (-(q_rank + kv_rank + MLA_ROPE)) % LANES
            w_cat = jnp.concatenate([mla_w_dq[j], mla_w_dkv[j], jnp.zeros((d_model, pad), F32)],
                                    axis=1).astype(BF16)
            w_uq = mla_w_uq[j].reshape(q_rank, MLA_HEADS, MLA_NOPE + MLA_ROPE)
            w_uq = jnp.pad(w_uq, ((0, 0), (0, 0), (0, MLA_QK_PAD - MLA_NOPE - MLA_ROPE)))
            w_uq = w_uq.reshape(q_rank, MLA_HEADS * MLA_QK_PAD).astype(BF16)
            w_ukv = mla_w_ukv[j].reshape(kv_rank, MLA_HEADS, MLA_NOPE + MLA_V)
            wk = w_ukv[:, :, :MLA_NOPE].reshape(kv_rank, MLA_HEADS * MLA_NOPE).astype(BF16)
            wv = w_ukv[:, :, MLA_NOPE:].reshape(kv_rank, MLA_HEADS * MLA_V).astype(BF16)
            w_o = mla_w_o[j].astype(BF16)
            qn, kvn = mla_q_norm[j][None, :], mla_kv_norm[j][None, :]

            (cos_p, sin_p), (cos_s, sin_s) = tab64
            qa, ckv_p, kpe_p = _mla_down(yp, gain, w_cat, qn, kvn, cos_p, sin_p, q_rank, kv_rank)
            q = _mla_q_up(qa, w_uq, cos_p, sin_p, q_scale)
            k, vt = _mla_expand(ckv_p, kpe_p, wk, wv.T)
            o = _flash_attention(q, k, vt, batch=bp, seq=sp, n_kv_heads=MLA_HEADS, dk=MLA_QK_PAD,
                                 dv=MLA_V, groups=1, n_maps=1)
            yp, hp = _mm_res(o, w_o, yp, ffn_gain)

            qa, ckv_s, kpe_s = _mla_down(ys, gain, w_cat, qn, kvn, cos_s, sin_s, q_rank, kv_rank)
            q = _mla_q_up(qa, w_uq, cos_s, sin_s, q_scale)
            o = _mla_decode_attention(q, cache_mla_ckv.reshape(-1, kv_rank), cache_mla_kpe.reshape(-1, MLA_ROPE),
                                      ckv_s, kpe_s, wk, wv, layer=j, batch=bs, past=past, new=ss)
            ys, hs = _mm_res(o, w_o, ys, ffn_gain)
            outs_p["ckv"].append(ckv_p.reshape(bp, sp, kv_rank))
            outs_p["kpe"].append(kpe_p.reshape(bp, sp, MLA_ROPE))
            outs_s["ckv"].append(ckv_s.reshape(bs, ss, kv_rank))
            outs_s["kpe"].append(kpe_s.reshape(bs, ss, MLA_ROPE))
        elif kind == 1:
            qw, kw = SWA_HEADS * SWA_HEAD_DIM, SWA_KV_HEADS * SWA_HEAD_DIM
            w_qkv = swa_w_qkv[j].astype(BF16)
            w_o = swa_w_o[j].astype(BF16)
            sinks = swa_sinks[j]
            (cos_p, sin_p), (cos_s, sin_s) = tab64
            q_scale = SWA_HEAD_DIM ** -0.5 * LOG2E
            q, k, v, kb, vt = _qkv_proj(yp, gain, w_qkv, cos_p, sin_p, qw, kw, kw, SWA_HEAD_DIM,
                                        q_scale=q_scale, vt_head=(SWA_HEAD_DIM, LANES - SWA_HEAD_DIM),
                                        vt_blocked=False)
            o = _swa_attention(q, kb, vt, sinks, seq=sp)
            yp, hp = _mm_res(o, w_o, yp, ffn_gain)
            k3 = k.reshape(bp, sp, SWA_KV_HEADS, SWA_HEAD_DIM)
            v3 = v.reshape(bp, sp, SWA_KV_HEADS, SWA_HEAD_DIM)
            outs_p["swk"].append(k3[:, sp - WINDOW:])
            outs_p["swv"].append(v3[:, sp - WINDOW:])

            q, k, v, _, _ = _qkv_proj(ys, gain, w_qkv, cos_s, sin_s, qw, kw, kw, SWA_HEAD_DIM, q_scale=q_scale)
            kc = cache_swa_k[j].reshape(bs * WINDOW, kw)
            vc = cache_swa_v[j].reshape(bs * WINDOW, kw)
            o = _swa_decode_attention(q, kc, k, vc, v, sinks, nq=ss, na=WINDOW, nb=ss)
            ys, hs = _mm_res(o, w_o, ys, ffn_gain)
            k_all = jnp.concatenate([cache_swa_k[j], k.reshape(bs, ss, SWA_KV_HEADS, SWA_HEAD_DIM)], axis=1)
            v_all = jnp.concatenate([cache_swa_v[j], v.reshape(bs, ss, SWA_KV_HEADS, SWA_HEAD_DIM)], axis=1)
            outs_s["swk"].append(k_all[:, ss:])
            outs_s["swv"].append(v_all[:, ss:])
        else:
            lam_init = 0.8 - 0.6 * math.exp(-0.3 * i)
            q_scale = DIFF_HEAD_DIM ** -0.5 * LOG2E
            groups = DIFF_HEADS // DIFF_KV_HEADS
            qw = DIFF_HEADS * 2 * DIFF_HEAD_DIM
            kw = DIFF_KV_HEADS * 2 * DIFF_HEAD_DIM
            w_qkv = diff_w_qkv[j].astype(BF16)
            w_o = diff_w_o[j].astype(BF16)
            lam = jnp.stack([diff_lambda_q1[j], diff_lambda_k1[j], diff_lambda_q2[j], diff_lambda_k2[j]])
            subln = diff_subln[j][None, :]
            common = dict(n_kv_heads=DIFF_KV_HEADS, dk=DIFF_HEAD_DIM, dv=2 * DIFF_HEAD_DIM, groups=groups,
                          n_maps=2, lam=lam, subln=subln, lam_init=lam_init)
            (cos_p, sin_p), (cos_s, sin_s) = tab128
            q, k, v, kb, vt = _qkv_proj(yp, gain, w_qkv, cos_p, sin_p, qw, kw, kw, DIFF_HEAD_DIM,
                                        q_scale=q_scale, vt_head=(2 * DIFF_HEAD_DIM, ONES_ROWS))
            o = _flash_attention(q, kb, vt, batch=bp, seq=sp, **common)
            yp, hp = _mm_res(o, w_o, yp, ffn_gain)
            outs_p["dk"].append(k.reshape(bp, sp, DIFF_KV_HEADS, 2, DIFF_HEAD_DIM))
            outs_p["dv"].append(v.reshape(bp, sp, DIFF_KV_HEADS, 2 * DIFF_HEAD_DIM))

            q, k, v, _, _ = _qkv_proj(ys, gain, w_qkv, cos_s, sin_s, qw, kw, kw, DIFF_HEAD_DIM,
                                      q_scale=q_scale)
            o = _decode_attention(q, cache_diff_k[j].reshape(bs * past, kw),
                                  cache_diff_v[j].reshape(bs * past, kw), k, v,
                                  batch=bs, past=past, new=ss, **common)
            ys, hs = _mm_res(o, w_o, ys, ffn_gain)
            outs_s["dk"].append(k.reshape(bs, ss, DIFF_KV_HEADS, 2, DIFF_HEAD_DIM))
            outs_s["dv"].append(v.reshape(bs, ss, DIFF_KV_HEADS, 2 * DIFF_HEAD_DIM))

        yp, conv_p = _conv_ffn(yp, hp, sp, jnp.zeros((bp, CONV_W - 1, d_ff), F32), i, *ffn_w)
        ys, conv_s = _conv_ffn(ys, hs, ss, state_ffn_conv[i], i, *ffn_w)
        outs_p["conv"].append(conv_p)
        outs_s["conv"].append(conv_s)

    fg = final_norm[None, :]
    y_prompt = _final_norm(yp, fg).reshape(bp, sp, d_model)
    y_sample = _final_norm(ys, fg).reshape(bs, ss, d_model)
    order = ("ckv", "kpe", "swk", "swv", "dk", "dv", "conv")
    stack = lambda xs: xs[0][None] if len(xs) == 1 else jnp.stack(xs)
    return (y_prompt, y_sample) + tuple(stack(outs_p[k]) for k in order) + tuple(
        stack(outs_s[k]) for k in order)
```

```python
import functools
import math

import jax
import jax.numpy as jnp
from jax import lax
from jax.experimental import pallas as pl
from jax.experimental.pallas import tpu as pltpu

F32 = jnp.float32
BF16 = jnp.bfloat16

CHUNK = 64
ROPE_THETA = 10000.0
NORM_EPS = 1e-6
SUBLN_EPS = 1e-5
N_MIXERS = 3
MLA_HEADS = 16
MLA_NOPE = 128
MLA_ROPE = 64
MLA_V = 128
SWA_HEADS = 32
SWA_KV_HEADS = 4
SWA_HEAD_DIM = 64
WINDOW = 128
DIFF_HEADS = 8
DIFF_KV_HEADS = 4
DIFF_HEAD_DIM = 128
CONV_W = 3

LANES = 128
SUBLANES = 8
MXU_COLS = 256
MLA_QK_PAD = MXU_COLS
VMEM_LIMIT_BYTES = 56 * 1024 * 1024
LOG2E = 1.4426950408889634

ROW_TILE = 512
FFN_ROW_TILE = 1024
FFN_COL_TILE = 512
FFN_SUB_TILE = 256
ATTN_Q_TILE = 2048
ATTN_KV_TILE = 512
ATTN_CHAIN = 256
ATTN_LOOKAHEAD = 3
ONES_ROWS = 16
MLA_DECODE_KV_TILE = 1024
MM_TILE = 512
MM_WEIGHT_BLOCK_BYTES = 12 * 1024 * 1024
MM_SUB_TILE = 256
NORM_SUB_TILE = 128


def _cparams(n_axes):
    return pltpu.CompilerParams(dimension_semantics=("arbitrary",) * n_axes,
                                vmem_limit_bytes=VMEM_LIMIT_BYTES)


def _resident(shape):
    return pl.BlockSpec(shape, lambda *_: (0,) * len(shape), pipeline_mode=pl.Buffered(1))


def _tile(n, pref):
    t = min(n, pref)
    assert n % t == 0, (n, pref)
    return t


def _rms(x, g, eps):
    ms = jnp.mean(x * x, axis=-1, keepdims=True)
    return (x * lax.rsqrt(ms + eps)) * g


def _rope_slab(x, cos, sin, head_dim):
    if head_dim == LANES:
        swapped = pltpu.roll(x, LANES // 2, axis=1)
    else:
        half = head_dim // 2
        lane = lax.broadcasted_iota(jnp.int32, x.shape, 1)
        first = (lane & (head_dim - 1)) < half
        swapped = jnp.where(first, pltpu.roll(x, LANES - half, axis=1), pltpu.roll(x, half, axis=1))
    return x * cos + swapped * sin


def _rope_tables(seq_len, offset, head_dim, rows):
    half = head_dim // 2
    pos = jnp.arange(seq_len, dtype=F32) + offset
    inv = ROPE_THETA ** (-jnp.arange(0, head_dim, 2, dtype=F32) / head_dim)
    ang = pos[:, None] * inv[None, :]
    cos, sin = jnp.cos(ang), jnp.sin(ang)
    reps = LANES // head_dim
    cos_l = jnp.tile(jnp.concatenate([cos, cos], axis=1), (1, reps))
    sin_l = jnp.tile(jnp.concatenate([-sin, sin], axis=1), (1, reps))
    n = max(rows // seq_len, 1)
    return jnp.tile(cos_l, (n, 1)), jnp.tile(sin_l, (n, 1))


def _mla_down_kernel(x_ref, g_ref, w_ref, qn_ref, kvn_ref, cos_ref, sin_ref,
                     qa_ref, ckv_ref, kpe_ref, *, q_rank, kv_rank, sub):
    def project(r0):
        h = _rms(x_ref[r0:r0 + sub, :], g_ref[...], NORM_EPS).astype(BF16)
        return jnp.dot(h, w_ref[...], preferred_element_type=F32)

    starts = list(range(0, x_ref.shape[0], sub))
    pending = project(starts[0])
    for t, r0 in enumerate(starts):
        d = pending
        if t + 1 < len(starts):
            pending = project(starts[t + 1])
        rows = slice(r0, r0 + sub)
        qa_ref[rows, :] = _rms(d[:, :q_rank], qn_ref[...], NORM_EPS).astype(BF16)
        ckv_ref[rows, :] = _rms(d[:, q_rank:q_rank + kv_rank], kvn_ref[...], NORM_EPS)
        slab = d[:, q_rank + kv_rank:q_rank + kv_rank + LANES]
        kpe_ref[rows, :] = _rope_slab(slab, cos_ref[rows, :], sin_ref[rows, :], MLA_ROPE)[:, :MLA_ROPE]


def _mla_down(x, gain, w_cat, q_norm, kv_norm, cos, sin, q_rank, kv_rank):
    m, d_model = x.shape
    tm = _tile(m, ROW_TILE)
    n_tab = cos.shape[0] // tm
    wn = w_cat.shape[1]
    row = lambda i: (i, 0)
    fixed = lambda i: (0, 0)
    tab = lambda i: (i % n_tab, 0)
    return pl.pallas_call(
        functools.partial(_mla_down_kernel, q_rank=q_rank, kv_rank=kv_rank, sub=_tile(tm, NORM_SUB_TILE)),
        grid=(m // tm,),
        in_specs=[pl.BlockSpec((tm, d_model), row), pl.BlockSpec((1, d_model), fixed),
                  _resident((d_model, wn)), pl.BlockSpec((1, q_rank), fixed),
                  pl.BlockSpec((1, kv_rank), fixed), pl.BlockSpec((tm, LANES), tab),
                  pl.BlockSpec((tm, LANES), tab)],
        out_specs=[pl.BlockSpec((tm, q_rank), row), pl.BlockSpec((tm, kv_rank), row),
                   pl.BlockSpec((tm, MLA_ROPE), row)],
        out_shape=[jax.ShapeDtypeStruct((m, q_rank), BF16), jax.ShapeDtypeStruct((m, kv_rank), F32),
                   jax.ShapeDtypeStruct((m, MLA_ROPE), F32)],
        compiler_params=_cparams(1), name="mla_down",
    )(x, gain, w_cat, q_norm, kv_norm, cos, sin)


def _mla_q_up_kernel(qa_ref, w_ref, cos_ref, sin_ref, q_ref, *, q_scale):
    qa = qa_ref[...]
    cos, sin = cos_ref[...], sin_ref[...]
    for h in range(MLA_HEADS):
        c0 = h * MLA_QK_PAD
        d = jnp.dot(qa, w_ref[:, c0:c0 + MLA_QK_PAD], preferred_element_type=F32)
        q_ref[:, c0:c0 + LANES] = (d[:, :LANES] * q_scale).astype(BF16)
        q_ref[:, c0 + LANES:c0 + MLA_QK_PAD] = (_rope_slab(d[:, LANES:], cos, sin, MLA_ROPE) * q_scale).astype(BF16)


def _mla_q_up(qa, w_pad, cos, sin, q_scale):
    m, q_rank = qa.shape
    tm = _tile(m, ROW_TILE)
    n_tab = cos.shape[0] // tm
    n = w_pad.shape[1]
    row = lambda i: (i, 0)
    fixed = lambda i: (0, 0)
    tab = lambda i: (i % n_tab, 0)
    return pl.pallas_call(
        functools.partial(_mla_q_up_kernel, q_scale=q_scale), grid=(m // tm,),
        in_specs=[pl.BlockSpec((tm, q_rank), row), _resident((q_rank, n)),
                  pl.BlockSpec((tm, LANES), tab), pl.BlockSpec((tm, LANES), tab)],
        out_specs=pl.BlockSpec((tm, n), row),
        out_shape=jax.ShapeDtypeStruct((m, n), BF16),
        compiler_params=_cparams(1), name="mla_q_up",
    )(qa, w_pad, cos, sin)


def _mla_expand_kernel(ckv_ref, kpe_ref, wk_ref, wvt_ref, k_ref, vt_ref):
    c = ckv_ref[...].astype(BF16)
    kpe = kpe_ref[...].astype(BF16)
    tm = c.shape[0]
    dva = MLA_V + ONES_ROWS
    ones = jnp.ones((ONES_ROWS, tm), BF16)
    zeros = jnp.zeros((tm, MLA_QK_PAD - MLA_NOPE - MLA_ROPE), BF16)
    pair = MXU_COLS // MLA_NOPE

    def project(h0):
        kn = jnp.dot(c, wk_ref[:, h0 * MLA_NOPE:(h0 + pair) * MLA_NOPE], preferred_element_type=F32)
        vt = lax.dot_general(wvt_ref[h0 * MLA_V:(h0 + pair) * MLA_V, :], c, (((1,), (1,)), ((), ())),
                             preferred_element_type=F32)
        return kn, vt

    pending = project(0)
    for h0 in range(0, MLA_HEADS, pair):
        kn, vt = pending
        if h0 + pair < MLA_HEADS:
            pending = project(h0 + pair)
        for i in range(pair):
            h = h0 + i
            c0 = h * MLA_QK_PAD
            k_ref[:, c0:c0 + MLA_NOPE] = kn[:, i * MLA_NOPE:(i + 1) * MLA_NOPE].astype(BF16)
            k_ref[:, c0 + MLA_NOPE:c0 + MLA_NOPE + MLA_ROPE] = kpe
            k_ref[:, c0 + MLA_NOPE + MLA_ROPE:c0 + MLA_QK_PAD] = zeros
            vt_ref[0, h * dva:h * dva + MLA_V, :] = vt[i * MLA_V:(i + 1) * MLA_V].astype(BF16)
            vt_ref[0, h * dva + MLA_V:(h + 1) * dva, :] = ones


def _mla_expand(ckv, kpe, wk, wvt):
    m, kv_rank = ckv.shape
    tm = _tile(m, ATTN_KV_TILE)
    row = lambda i: (i, 0)
    nk, nva = MLA_HEADS * MLA_QK_PAD, MLA_HEADS * (MLA_V + ONES_ROWS)
    return pl.pallas_call(
        _mla_expand_kernel, grid=(m // tm,),
        in_specs=[pl.BlockSpec((tm, kv_rank), row), pl.BlockSpec((tm, MLA_ROPE), row),
                  _resident(wk.shape), _resident(wvt.shape)],
        out_specs=[pl.BlockSpec((tm, nk), row), pl.BlockSpec((1, nva, tm), lambda i: (i, 0, 0))],
        out_shape=[jax.ShapeDtypeStruct((m, nk), BF16), jax.ShapeDtypeStruct((m // tm, nva, tm), BF16)],
        compiler_params=_cparams(1), name="mla_expand",
    )(ckv, kpe, wk, wvt)


def _qkv_proj_kernel(x_ref, g_ref, w_ref, cos_ref, sin_ref, q_ref, k_ref, v_ref, kb_ref, vb_ref,
                     *, qw, kw, vw, head_dim, q_scale, vt_head):
    h = _rms(x_ref[...], g_ref[...], NORM_EPS).astype(BF16)
    cos, sin = cos_ref[...], sin_ref[...]
    vt_dst = None
    if vt_head:
        vt_dst = vb_ref.at[0] if len(vb_ref.shape) == 3 else vb_ref
    for n0 in range(0, qw + kw + vw, MXU_COLS):
        dd = jnp.dot(h, w_ref[:, n0:n0 + MXU_COLS], preferred_element_type=F32)
        for c0 in range(n0, n0 + MXU_COLS, LANES):
            d = dd[:, c0 - n0:c0 - n0 + LANES]
            if c0 < qw:
                q_ref[:, c0:c0 + LANES] = (_rope_slab(d, cos, sin, head_dim) * q_scale).astype(BF16)
            elif c0 < qw + kw:
                r = _rope_slab(d, cos, sin, head_dim)
                k_ref[:, c0 - qw:c0 - qw + LANES] = r
                kb_ref[:, c0 - qw:c0 - qw + LANES] = r.astype(BF16)
            else:
                c1 = c0 - qw - kw
                v_ref[:, c1:c1 + LANES] = d
                if vt_head:
                    vd, ones_rows = vt_head
                    dt = d.T.astype(BF16)
                    for hv in range(c1 // vd, (c1 + LANES - 1) // vd + 1):
                        lo, hi = max(c1, hv * vd), min(c1 + LANES, (hv + 1) * vd)
                        r0 = hv * (vd + ones_rows) + lo - hv * vd
                        vt_dst[r0:r0 + hi - lo, :] = dt[lo - c1:hi - c1]
                        if hi == (hv + 1) * vd:
                            vt_dst[r0 + hi - lo:r0 + hi - lo + ones_rows, :] = jnp.ones((ones_rows, d.shape[0]), BF16)
                else:
                    vb_ref[:, c1:c1 + LANES] = d.astype(BF16)


def _qkv_proj(x, gain, w, cos, sin, qw, kw, vw, head_dim, *, q_scale=1.0, vt_head=None, vt_blocked=True):
    m, d_model = x.shape
    tm = _tile(m, ATTN_KV_TILE if vt_head else ROW_TILE)
    n_tab = cos.shape[0] // tm
    row = lambda i: (i, 0)
    fixed = lambda i: (0, 0)
    tab = lambda i: (i % n_tab, 0)
    if vt_head:
        vwa = vw // vt_head[0] * (vt_head[0] + vt_head[1])
        if vt_blocked:
            vb_spec = pl.BlockSpec((1, vwa, tm), lambda i: (i, 0, 0))
            vb_shape = jax.ShapeDtypeStruct((m // tm, vwa, tm), BF16)
        else:
            vb_spec = pl.BlockSpec((vwa, tm), lambda i: (0, i))
            vb_shape = jax.ShapeDtypeStruct((vwa, m), BF16)
    else:
        vb_spec = pl.BlockSpec((tm, vw), row)
        vb_shape = jax.ShapeDtypeStruct((m, vw), BF16)
    return pl.pallas_call(
        functools.partial(_qkv_proj_kernel, qw=qw, kw=kw, vw=vw, head_dim=head_dim, q_scale=q_scale,
                          vt_head=vt_head),
        grid=(m // tm,),
        in_specs=[pl.BlockSpec((tm, d_model), row), pl.BlockSpec((1, d_model), fixed),
                  _resident(w.shape), pl.BlockSpec((tm, LANES), tab),
                  pl.BlockSpec((tm, LANES), tab)],
        out_specs=[pl.BlockSpec((tm, qw), row), pl.BlockSpec((tm, kw), row), pl.BlockSpec((tm, vw), row),
                   pl.BlockSpec((tm, kw), row), vb_spec],
        out_shape=[jax.ShapeDtypeStruct((m, qw), BF16), jax.ShapeDtypeStruct((m, kw), F32),
                   jax.ShapeDtypeStruct((m, vw), F32), jax.ShapeDtypeStruct((m, kw), BF16), vb_shape],
        compiler_params=_cparams(1), name="qkv_proj",
    )(x, gain, w, cos, sin)


def _mm_res_kernel(*refs, sub, with_norm, norm_only):
    o_ref = g_ref = h_ref = None
    if norm_only:
        a_ref, w_ref, r_ref, g_ref, h_ref = refs
    elif with_norm:
        a_ref, w_ref, r_ref, g_ref, o_ref, h_ref = refs
    else:
        a_ref, w_ref, r_ref, o_ref = refs
    starts = list(range(0, a_ref.shape[0], sub))
    pending = jnp.dot(a_ref[0:sub, :], w_ref[...], preferred_element_type=F32)
    for t, r0 in enumerate(starts):
        d = pending
        if t + 1 < len(starts):
            r1 = starts[t + 1]
            pending = jnp.dot(a_ref[r1:r1 + sub, :], w_ref[...], preferred_element_type=F32)
        y = r_ref[r0:r0 + sub, :] + d
        if o_ref is not None:
            o_ref[r0:r0 + sub, :] = y
        if h_ref is not None:
            h_ref[r0:r0 + sub, :] = _rms(y, g_ref[...], NORM_EPS).astype(h_ref.dtype)


def _mm_res(a, w, res, norm_gain=None, layer=None, norm_only=False):
    m, k = a.shape
    n = w.shape[-1]
    tm = _tile(m, MM_TILE // 2 if norm_only else MM_TILE)
    tn = n
    while not norm_only and tn * k * w.dtype.itemsize > MM_WEIGHT_BLOCK_BYTES and tn % (2 * MXU_COLS) == 0:
        tn //= 2
    with_norm = norm_gain is not None
    assert not with_norm or tn == n
    tile = pl.BlockSpec((tm, tn), lambda j, i: (i, j))
    once = dict(pipeline_mode=pl.Buffered(1)) if norm_only else {}
    if layer is None:
        w_spec = pl.BlockSpec((k, tn), lambda j, i: (0, j), **once)
    else:
        w_spec = pl.BlockSpec((None, k, tn), lambda j, i: (layer, 0, j), **once)
    in_specs = [pl.BlockSpec((tm, k), lambda j, i: (i, 0)), w_spec, tile]
    out_specs, out_shape, args = tile, jax.ShapeDtypeStruct((m, n), F32), [a, w, res]
    if with_norm:
        in_specs.append(pl.BlockSpec((1, n), lambda j, i: (0, 0)))
        out_specs, out_shape = [tile, tile], [out_shape, jax.ShapeDtypeStruct((m, n), BF16)]
        if norm_only:
            out_specs, out_shape = tile, jax.ShapeDtypeStruct((m, n), F32)
        args.append(norm_gain)
    return pl.pallas_call(
        functools.partial(_mm_res_kernel, sub=_tile(tm, MM_SUB_TILE), with_norm=with_norm, norm_only=norm_only),
        grid=(n // tn, m // tm), in_specs=in_specs, out_specs=out_specs, out_shape=out_shape,
        compiler_params=_cparams(2), name="mm_res",
    )(*args)


def _diff_lambda(lam_ref, lam_init):
    lam = lam_ref[...]
    a = jnp.sum(lam[0:1] * lam[1:2], axis=-1, keepdims=True)
    b = jnp.sum(lam[2:3] * lam[3:4], axis=-1, keepdims=True)
    return jnp.exp(a) - jnp.exp(b) + lam_init


def _combine_heads(o_list, o_ref, rows, lam_ref, subln_ref, *, n_maps, dv, lam_init):
    groups = len(o_list) // n_maps
    for g in range(groups):
        if n_maps == 1:
            o = o_list[g]
        else:
            lam = _diff_lambda(lam_ref, lam_init)
            o = o_list[2 * g] - lam * o_list[2 * g + 1]
            o = _rms(o, subln_ref[...], SUBLN_EPS) * (1.0 - lam_init)
        o_ref[rows, g * dv:(g + 1) * dv] = o.astype(o_ref.dtype)


def _flash_kernel(*refs, tq, tk, cw, dk, dv, groups, n_maps, lam_init):
    if n_maps == 2:
        lam_ref, subln_ref, q_ref, k_ref, vt_ref, o_ref, m_ref, acc_ref, st_ref, mx_ref = refs
    else:
        q_ref, k_ref, vt_ref, o_ref, m_ref, acc_ref, st_ref, mx_ref = refs
        lam_ref = subln_ref = None
    n_sub = groups * n_maps
    n_chain = tq // cw
    kv_per_q = tq // tk
    qi = pl.program_id(2)
    nt = (((1,), (1,)), ((), ()))

    m_ref[...] = jnp.full(m_ref.shape, -jnp.inf, F32)
    acc_ref[...] = jnp.zeros(acc_ref.shape, F32)

    def scores(item, k_blks, slot):
        u, n, k_lo, bi, nk = item
        mi = u % n_maps
        q_n = q_ref[n * cw:(n + 1) * cw, u * dk:(u + 1) * dk]
        st = lax.dot_general(k_blks[bi][:nk, mi * dk:(mi + 1) * dk], q_n, nt, preferred_element_type=F32)
        if k_lo is not None:
            kc = (lax.broadcasted_iota(jnp.int32, st.shape, 0) + k_lo) // CHUNK
            qc = (lax.broadcasted_iota(jnp.int32, st.shape, 1) + n * cw) // CHUNK
            st = jnp.where(kc <= qc, st, -jnp.inf)
        st_ref[slot, 0:nk, :] = st
        mx_ref[slot] = jnp.max(st, axis=0, keepdims=True)

    def update(item, slot, vt_blks):
        u, n, _, bi, nk = item
        st = st_ref[slot, 0:nk, :]
        idx = u * n_chain + n
        m_prev = m_ref[idx]
        m_new = jnp.maximum(m_prev, mx_ref[slot])
        alpha = jnp.exp2(m_prev - m_new)
        pt = jnp.exp2(st - m_new).astype(BF16)
        acc_ref[idx] = acc_ref[idx] * alpha + jnp.dot(vt_blks[bi][:, :nk], pt, preferred_element_type=F32)
        m_ref[idx] = m_new

    def run_chains(items, first_block):
        blocks = sorted({it[3] for it in items})
        k_blks = {bi: k_ref[pl.ds(pl.multiple_of((first_block + bi) * tk, tk), tk), :] for bi in blocks}
        vt_blks = {bi: vt_ref[first_block + bi] for bi in blocks}
        n_slots = ATTN_LOOKAHEAD + 1
        for t in range(min(ATTN_LOOKAHEAD, len(items))):
            scores(items[t], k_blks, t % n_slots)
        for t, item in enumerate(items):
            if t + ATTN_LOOKAHEAD < len(items):
                scores(items[t + ATTN_LOOKAHEAD], k_blks, (t + ATTN_LOOKAHEAD) % n_slots)
            update(item, t % n_slots, vt_blks)

    def body(j, carry):
        items = [(u, n, None, bi, tk) for bi in range(kv_per_q) for n in range(n_chain) for u in range(n_sub)]
        run_chains(items, j * kv_per_q)
        return carry

    lax.fori_loop(0, qi, body, 0)

    items = [(u, n, bi * tk if (bi + 1) * tk > n * cw else None, bi, min(tk, (n + 1) * cw - bi * tk))
             for bi in range(kv_per_q) for n in range(n_chain) if bi * tk < (n + 1) * cw
             for u in range(n_sub)]
    run_chains(items, qi * kv_per_q)

    for n in range(n_chain):
        outs = []
        for u in range(n_sub):
            a = acc_ref[u * n_chain + n]
            outs.append((a[:dv] / a[dv:dv + 1]).T)
        _combine_heads(outs, o_ref, slice(n * cw, (n + 1) * cw), lam_ref, subln_ref,
                       n_maps=n_maps, dv=dv, lam_init=lam_init)


def _flash_attention(q, k, vt, *, batch, seq, n_kv_heads, dk, dv, groups, n_maps,
                     lam=None, subln=None, lam_init=0.0):
    tk = vt.shape[2]
    dva = dv + ONES_ROWS
    n_sub = groups * n_maps
    tq = _tile(seq, max(ATTN_Q_TILE if n_sub == 1 else ATTN_Q_TILE // 2, tk))
    cw = _tile(tq, ATTN_CHAIN)
    assert tq % tk == 0 and tk % CHUNK == 0 and cw % CHUNK == 0
    nq, nk = seq // tq, seq // tk
    n_chain = tq // cw
    kern = functools.partial(_flash_kernel, tq=tq, tk=tk, cw=cw, dk=dk, dv=dv, groups=groups,
                             n_maps=n_maps, lam_init=lam_init)
    in_specs = [pl.BlockSpec((tq, n_sub * dk), lambda b, h, i: (b * nq + i, h)),
                pl.BlockSpec((seq, n_maps * dk), lambda b, h, i: (b, h)),
                pl.BlockSpec((nk, dva, tk), lambda b, h, i: (b, h, 0))]
    args = [q, k, vt]
    if n_maps == 2:
        in_specs = [pl.BlockSpec(lam.shape, lambda b, h, i: (0, 0)),
                    pl.BlockSpec(subln.shape, lambda b, h, i: (0, 0))] + in_specs
        args = [lam, subln] + args
    return pl.pallas_call(
        kern, grid=(batch, n_kv_heads, nq), in_specs=in_specs,
        out_specs=pl.BlockSpec((tq, groups * dv), lambda b, h, i: (b * nq + i, h)),
        out_shape=jax.ShapeDtypeStruct((batch * seq, n_kv_heads * groups * dv), BF16),
        scratch_shapes=[pltpu.VMEM((n_sub * n_chain, 1, cw), F32),
                        pltpu.VMEM((n_sub * n_chain, dva, cw), F32),
                        pltpu.VMEM((ATTN_LOOKAHEAD + 1, tk, cw), F32),
                        pltpu.VMEM((ATTN_LOOKAHEAD + 1, 1, cw), F32)],
        compiler_params=_cparams(3), name="flash_attention",
    )(*args)


def _decode_kernel(*refs, dk, dv, groups, n_maps, lam_init):
    if n_maps == 2:
        lam_ref, subln_ref, q_ref, kc_ref, vc_ref, kn_ref, vn_ref, o_ref = refs
    else:
        q_ref, kc_ref, vc_ref, kn_ref, vn_ref, o_ref = refs
        lam_ref = subln_ref = None
    nt = (((1,), (1,)), ((), ()))
    vc = vc_ref[...].astype(BF16)
    vn = vn_ref[...].astype(BF16)
    outs = []
    for u in range(groups * n_maps):
        mi = u % n_maps
        q = q_ref[:, u * dk:(u + 1) * dk]
        kc = kc_ref[:, mi * dk:(mi + 1) * dk].astype(BF16)
        kn = kn_ref[:, mi * dk:(mi + 1) * dk].astype(BF16)
        s1 = lax.dot_general(q, kc, nt, preferred_element_type=F32)
        s2 = lax.dot_general(q, kn, nt, preferred_element_type=F32)
        m = jnp.maximum(jnp.max(s1, axis=1, keepdims=True), jnp.max(s2, axis=1, keepdims=True))
        p1 = jnp.exp2(s1 - m)
        p2 = jnp.exp2(s2 - m)
        l = jnp.sum(p1, axis=1, keepdims=True) + jnp.sum(p2, axis=1, keepdims=True)
        o = (jnp.dot(p1.astype(BF16), vc, preferred_element_type=F32)
             + jnp.dot(p2.astype(BF16), vn, preferred_element_type=F32))
        outs.append(o / l)
    _combine_heads(outs, o_ref, slice(None), lam_ref, subln_ref, n_maps=n_maps, dv=dv, lam_init=lam_init)


def _decode_attention(q, kc, vc, kn, vn, *, batch, past, new, n_kv_heads, dk, dv, groups, n_maps,
                      lam=None, subln=None, lam_init=0.0):
    n_sub = groups * n_maps
    kern = functools.partial(_decode_kernel, dk=dk, dv=dv, groups=groups, n_maps=n_maps, lam_init=lam_init)
    bh = lambda b, h: (b, h)
    in_specs = [pl.BlockSpec((new, n_sub * dk), bh), pl.BlockSpec((past, n_maps * dk), bh),
                pl.BlockSpec((past, dv), bh), pl.BlockSpec((new, n_maps * dk), bh),
                pl.BlockSpec((new, dv), bh)]
    args = [q, kc, vc, kn, vn]
    if n_maps == 2:
        in_specs = [pl.BlockSpec(lam.shape, lambda b, h: (0, 0)),
                    pl.BlockSpec(subln.shape, lambda b, h: (0, 0))] + in_specs
        args = [lam, subln] + args
    return pl.pallas_call(
        kern, grid=(batch, n_kv_heads), in_specs=in_specs,
        out_specs=pl.BlockSpec((new, groups * dv), bh),
        out_shape=jax.ShapeDtypeStruct((batch * new, n_kv_heads * groups * dv), BF16),
        compiler_params=_cparams(2), name="decode_attention",
    )(*args)


def _mla_decode_kernel(q_ref, ckv_c_ref, kpe_c_ref, ckv_n_ref, kpe_n_ref, wk_ref, wv_ref, o_ref,
                       kcat_ref, qcat_ref, *, block):
    new = q_ref.shape[0]
    past, rank = ckv_c_ref.shape
    nt = (((1,), (1,)), ((), ()))
    kcat_ref[0:past, 0:rank] = ckv_c_ref[...].astype(BF16)
    kcat_ref[past:past + new, 0:rank] = ckv_n_ref[...].astype(BF16)
    kcat_ref[0:past, rank:rank + MLA_ROPE] = kpe_c_ref[...].astype(BF16)
    kcat_ref[past:past + new, rank:rank + MLA_ROPE] = kpe_n_ref[...].astype(BF16)
    kcat_ref[:, rank + MLA_ROPE:rank + LANES] = jnp.zeros((past + new, LANES - MLA_ROPE), BF16)
    for h in range(MLA_HEADS):
        c0 = h * MLA_QK_PAD
        q_lat = lax.dot_general(q_ref[:, c0:c0 + MLA_NOPE], wk_ref[:, h * MLA_NOPE:(h + 1) * MLA_NOPE], nt,
                                preferred_element_type=F32)
        qcat_ref[h * new:(h + 1) * new, 0:rank] = q_lat.astype(BF16)
        qcat_ref[h * new:(h + 1) * new, rank:rank + LANES] = q_ref[:, c0 + MLA_NOPE:c0 + MLA_QK_PAD]
    q = qcat_ref[...]
    rows = MLA_HEADS * new
    m = jnp.full((rows, 1), -jnp.inf, F32)
    l = jnp.zeros((rows, 1), F32)
    acc = jnp.zeros((rows, rank), F32)
    starts = list(range(0, past, block)) + [past]
    for k0 in starts:
        nk = min(block, past - k0) if k0 < past else new
        kb = kcat_ref[k0:k0 + nk, :]
        s = lax.dot_general(q, kb, nt, preferred_element_type=F32)
        m_new = jnp.maximum(m, jnp.max(s, axis=1, keepdims=True))
        alpha = jnp.exp2(m - m_new)
        p = jnp.exp2(s - m_new)
        l = alpha * l + jnp.sum(p, axis=1, keepdims=True)
        acc = alpha * acc + jnp.dot(p.astype(BF16), kb[:, 0:rank], preferred_element_type=F32)
        m = m_new
    o_lat = (acc / l).astype(BF16)
    for h in range(MLA_HEADS):
        o_ref[:, h * MLA_V:(h + 1) * MLA_V] = jnp.dot(
            o_lat[h * new:(h + 1) * new], wv_ref[:, h * MLA_V:(h + 1) * MLA_V],
            preferred_element_type=F32).astype(o_ref.dtype)


def _mla_decode_attention(q, ckv_cache, kpe_cache, ckv_new, kpe_new, wk, wv, *, layer, batch, past, new):
    rank = ckv_cache.shape[1]
    per = lambda b: (b, 0)
    cached = lambda b: (layer * batch + b, 0)
    return pl.pallas_call(
        functools.partial(_mla_decode_kernel, block=_tile(past, MLA_DECODE_KV_TILE)), grid=(batch,),
        in_specs=[pl.BlockSpec((new, q.shape[1]), per), pl.BlockSpec((past, rank), cached),
                  pl.BlockSpec((past, MLA_ROPE), cached), pl.BlockSpec((new, rank), per),
                  pl.BlockSpec((new, MLA_ROPE), per), _resident(wk.shape), _resident(wv.shape)],
        out_specs=pl.BlockSpec((new, MLA_HEADS * MLA_V), per),
        out_shape=jax.ShapeDtypeStruct((batch * new, MLA_HEADS * MLA_V), BF16),
        scratch_shapes=[pltpu.VMEM((past + new, rank + LANES), BF16),
                        pltpu.VMEM((MLA_HEADS * new, rank + LANES), BF16)],
        compiler_params=_cparams(1), name="mla_decode_attention",
    )(q, ckv_cache, kpe_cache, ckv_new, kpe_new, wk, wv)


def _swa_decode_kernel(sinks_ref, q_ref, ka_ref, kb_ref, va_ref, vb_ref, o_ref):
    nq = q_ref.shape[0]
    group = SWA_HEADS // SWA_KV_HEADS
    d = SWA_HEAD_DIM
    nt = (((1,), (1,)), ((), ()))
    for h in range(SWA_KV_HEADS):
        hs = slice(h * d, (h + 1) * d)
        k = jnp.concatenate([ka_ref[:, hs], kb_ref[:, hs]], axis=0).astype(BF16)
        v = jnp.concatenate([va_ref[:, hs], vb_ref[:, hs]], axis=0).astype(BF16)
        q = jnp.concatenate([q_ref[:, (h * group + g) * d:(h * group + g + 1) * d] for g in range(group)],
                            axis=0)
        sink = jnp.concatenate([jnp.full((nq, 1), sinks_ref[h * group + g] * LOG2E, F32)
                                for g in range(group)], axis=0)
        s = lax.dot_general(q, k, nt, preferred_element_type=F32)
        m = jnp.maximum(jnp.max(s, axis=1, keepdims=True), sink)
        p = jnp.exp2(s - m)
        l = jnp.sum(p, axis=1, keepdims=True) + jnp.exp2(sink - m)
        o = jnp.dot(p.astype(BF16), v, preferred_element_type=F32) / l
        for g in range(0, group, 2):
            pair = jnp.concatenate([o[g * nq:(g + 1) * nq], o[(g + 1) * nq:(g + 2) * nq]], axis=1)
            c0 = (h * group + g) * d
            o_ref[:, c0:c0 + 2 * d] = pair.astype(o_ref.dtype)


def _swa_decode_attention(q, ka, kb, va, vb, sinks, *, nq, na, nb):
    m, qw = q.shape
    kvw = ka.shape[1]
    cur = lambda t: (t, 0)
    return pl.pallas_call(
        _swa_decode_kernel, grid=(m // nq,),
        in_specs=[pl.BlockSpec(memory_space=pltpu.SMEM), pl.BlockSpec((nq, qw), cur),
                  pl.BlockSpec((na, kvw), cur), pl.BlockSpec((nb, kvw), cur),
                  pl.BlockSpec((na, kvw), cur), pl.BlockSpec((nb, kvw), cur)],
        out_specs=pl.BlockSpec((nq, qw), cur),
        out_shape=jax.ShapeDtypeStruct((m, qw), BF16),
        compiler_params=_cparams(1), name="swa_decode_attention",
    )(sinks, q, ka, kb, va, vb)


def _swa_kernel(sinks_ref, q_ref, ka_ref, kb_ref, vta_ref, vtb_ref, o_ref, st_ref, *, blocks_per_seq):
    nq, na = q_ref.shape[0], ka_ref.shape[0]
    nk = na + kb_ref.shape[0]
    group = SWA_HEADS // SWA_KV_HEADS
    d = SWA_HEAD_DIM
    vrows = vta_ref.shape[0] // SWA_KV_HEADS
    nt = (((1,), (1,)), ((), ()))
    kc = lax.broadcasted_iota(jnp.int32, (nk, group * nq), 0) // CHUNK
    qc = (lax.broadcasted_iota(jnp.int32, (nk, group * nq), 1) % nq) // CHUNK
    first = (pl.program_id(0) % blocks_per_seq) == 0
    lo = jnp.where(first, na // CHUNK, 0)
    valid = (kc >= qc) & (kc <= qc + na // CHUNK) & (kc >= lo)

    def scores(h):
        hs = slice(h * d, (h + 1) * d)
        k = jnp.concatenate([ka_ref[:, hs], kb_ref[:, hs]], axis=0)
        q = jnp.concatenate([q_ref[:, (h * group + g) * d:(h * group + g + 1) * d] for g in range(group)],
                            axis=0)
        st = lax.dot_general(k, q, nt, preferred_element_type=F32)
        st_ref[h % 2] = jnp.where(valid, st, -jnp.inf)

    scores(0)
    for h in range(SWA_KV_HEADS):
        if h + 1 < SWA_KV_HEADS:
            scores(h + 1)
        st = st_ref[h % 2]
        sink = jnp.concatenate([jnp.full((1, nq), sinks_ref[h * group + g] * LOG2E, F32)
                                for g in range(group)], axis=1)
        m = jnp.maximum(jnp.max(st, axis=0, keepdims=True), sink)
        pt = jnp.exp2(st - m).astype(BF16)
        vt = jnp.concatenate([vta_ref[h * vrows:(h + 1) * vrows, :], vtb_ref[h * vrows:(h + 1) * vrows, :]],
                             axis=1)
        ot = jnp.dot(vt, pt, preferred_element_type=F32)
        l = ot[d:d + 1] + jnp.exp2(sink - m)
        o = ot[:d] / l
        for g in range(0, group, 2):
            pair = jnp.concatenate([o[:, g * nq:(g + 1) * nq], o[:, (g + 1) * nq:(g + 2) * nq]], axis=0)
            c0 = (h * group + g) * d
            o_ref[:, c0:c0 + 2 * d] = pair.T.astype(o_ref.dtype)


def _swa_attention(q, k, vt, sinks, *, seq):
    m, qw = q.shape
    kvw = k.shape[1]
    nq = WINDOW
    cur = lambda t: (t, 0)
    prev = lambda t: (jnp.maximum(t - 1, 0), 0)
    cur_t = lambda t: (0, t)
    prev_t = lambda t: (0, jnp.maximum(t - 1, 0))
    return pl.pallas_call(
        functools.partial(_swa_kernel, blocks_per_seq=seq // nq),
        grid=(m // nq,),
        in_specs=[pl.BlockSpec(memory_space=pltpu.SMEM), pl.BlockSpec((nq, qw), cur),
                  pl.BlockSpec((nq, kvw), prev), pl.BlockSpec((nq, kvw), cur),
                  pl.BlockSpec((vt.shape[0], nq), prev_t), pl.BlockSpec((vt.shape[0], nq), cur_t)],
        out_specs=pl.BlockSpec((nq, qw), cur),
        out_shape=jax.ShapeDtypeStruct((m, qw), BF16),
        scratch_shapes=[pltpu.VMEM((2, 2 * nq, SWA_HEADS // SWA_KV_HEADS * nq), F32)],
        compiler_params=_cparams(1), name="swa_attention",
    )(sinks, q, k, k, vt, vt)


def _ffn_up_kernel(h_ref, wg_ref, wu_ref, cw_ref, cb_ref, prev_ref, act_ref, tail_ref,
                   halo_ref, carry_ref, *, rows, n_slab, tiles_per_seq, sub):
    i, j = pl.program_id(0), pl.program_id(1)
    halo = SUBLANES

    w0, w1, w2 = cw_ref[0:1, :], cw_ref[1:2, :], cw_ref[2:3, :]
    bias = cb_ref[...]
    tm = h_ref.shape[0]

    def matmuls(r0):
        h = h_ref[r0:r0 + sub, :]
        return (jnp.dot(h, wg_ref[...], preferred_element_type=F32),
                jnp.dot(h, wu_ref[...], preferred_element_type=F32))

    def conv_act(gs, us, before):
        r8 = lax.broadcasted_iota(jnp.int32, (halo, gs.shape[1]), 0)
        g1 = pltpu.roll(gs, 1, axis=0)
        g2 = pltpu.roll(gs, 2, axis=0)
        head1 = jnp.where(r8 == 0, before[halo - 1:halo], g1[:halo])
        head2 = jnp.where(r8 == 0, before[halo - 2:halo - 1], jnp.where(r8 == 1, before[halo - 1:halo], g2[:halo]))
        g1 = jnp.concatenate([head1, g1[halo:]], axis=0)
        g2 = jnp.concatenate([head2, g2[halo:]], axis=0)
        conv = ((bias + g2 * w0) + g1 * w1) + gs * w2
        return (conv * jax.nn.sigmoid(conv)) * us

    if n_slab == 1:
        is_start = (i % tiles_per_seq) == 0

        @pl.when(is_start)
        def _():
            halo_ref[...] = prev_ref[0]

        @pl.when(jnp.logical_not(is_start))
        def _():
            halo_ref[...] = carry_ref[j]

    starts = list(range(0, tm, sub))
    pending = matmuls(starts[0])
    before = halo_ref[...] if n_slab == 1 else None
    for t, r0 in enumerate(starts):
        gate, up = pending
        if t + 1 < len(starts):
            pending = matmuls(starts[t + 1])
        for s0 in range(0, sub, rows):
            gs, us = gate[s0:s0 + rows], up[s0:s0 + rows]
            if n_slab > 1:
                before = prev_ref[(r0 + s0) // rows]
            act_ref[r0 + s0:r0 + s0 + rows, :] = conv_act(gs, us, before).astype(BF16)
            before = gs[rows - halo:rows]
            if n_slab > 1:
                tail_ref[(r0 + s0) // rows] = before
    if n_slab == 1:
        tail_ref[0] = before
        carry_ref[j] = before


def _ffn_up(h, wg, wu, conv_w, conv_b, prev, *, layer, seq):
    m, d_model = h.shape
    d_ff = wg.shape[-1]
    tm = _tile(m, FFN_ROW_TILE)
    tf = _tile(d_ff, FFN_COL_TILE)
    if tm >= seq:
        rows, n_slab, tiles_per_seq = seq, tm // seq, 1
        prev_spec = pl.BlockSpec((n_slab, SUBLANES, tf), lambda i, j: (i, 0, j))
    else:
        rows, n_slab, tiles_per_seq = tm, 1, seq // tm
        prev_spec = pl.BlockSpec((1, SUBLANES, tf), lambda i, j: (i // tiles_per_seq, 0, j))
    sub = _tile(tm, FFN_SUB_TILE)
    if n_slab == 1:
        rows = sub
    assert sub % rows == 0
    nj = d_ff // tf
    return pl.pallas_call(
        functools.partial(_ffn_up_kernel, rows=rows, n_slab=n_slab, tiles_per_seq=tiles_per_seq, sub=sub),
        grid=(m // tm, nj),
        in_specs=[pl.BlockSpec((tm, d_model), lambda i, j: (i, 0)),
                  pl.BlockSpec((None, d_model, tf), lambda i, j: (layer, 0, j)),
                  pl.BlockSpec((None, d_model, tf), lambda i, j: (layer, 0, j)),
                  pl.BlockSpec((None, CONV_W, tf), lambda i, j: (layer, 0, j)),
                  pl.BlockSpec((None, 1, tf), lambda i, j: (layer, 0, j)),
                  prev_spec],
        out_specs=[pl.BlockSpec((tm, tf), lambda i, j: (i, j)),
                   pl.BlockSpec((n_slab, SUBLANES, tf), lambda i, j: (i, 0, j))],
        out_shape=[jax.ShapeDtypeStruct((m, d_ff), BF16),
                   jax.ShapeDtypeStruct((m // tm * n_slab, SUBLANES, d_ff), F32)],
        scratch_shapes=[pltpu.VMEM((SUBLANES, tf), F32),
                        pltpu.VMEM((nj, SUBLANES, tf), F32)],
        compiler_params=_cparams(2), name="ffn_up",
    )(h, wg, wu, conv_w, conv_b, prev)


def _conv_ffn(y, h, seq, prev_state, layer, wg, wu, conv_w, conv_b, wd, final_gain=None):
    n_seq = y.shape[0] // seq
    d_ff = wg.shape[-1]
    prev = jnp.concatenate([jnp.zeros((n_seq, SUBLANES - (CONV_W - 1), d_ff), F32), prev_state], axis=1)
    act, tails = _ffn_up(h, wg, wu, conv_w, conv_b, prev, layer=layer, seq=seq)
    y = _mm_res(act, wd, y, final_gain, layer=layer, norm_only=final_gain is not None)
    state = tails.reshape(n_seq, -1, SUBLANES, d_ff)[:, -1, SUBLANES - (CONV_W - 1):, :]
    return y, state


def kernel(x_prompt, x_sample, cache_mla_ckv, cache_mla_kpe, cache_swa_k, cache_swa_v, cache_diff_k,
           cache_diff_v, state_ffn_conv, norm_mix, norm_ffn, final_norm, mla_w_dq, mla_q_norm, mla_w_uq,
           mla_w_dkv, mla_kv_norm, mla_w_ukv, mla_w_o, swa_w_qkv, swa_sinks, swa_w_o, diff_w_qkv,
           diff_lambda_q1, diff_lambda_k1, diff_lambda_q2, diff_lambda_k2, diff_subln, diff_w_o,
           ffn_w_gate, ffn_w_up, ffn_conv_w, ffn_conv_b, ffn_w_down):
    bp, sp, d_model = x_prompt.shape
    bs, ss, _ = x_sample.shape
    depth = norm_mix.shape[0]
    past = cache_mla_ckv.shape[2]
    d_ff = ffn_w_gate.shape[2]
    q_rank = mla_w_dq.shape[2]
    kv_rank = mla_w_ukv.shape[1]

    yp = x_prompt.reshape(bp * sp, d_model)
    ys = x_sample.reshape(bs * ss, d_model)
    row_p = _tile(bp * sp, ROW_TILE)
    row_s = _tile(bs * ss, ROW_TILE)

    def tables(head_dim):
        return (_rope_tables(sp, 0, head_dim, row_p), _rope_tables(ss, past, head_dim, row_s))

    tab64, tab128 = tables(64), tables(128)
    outs_p = {k: [] for k in ("ckv", "kpe", "swk", "swv", "dk", "dv", "conv")}
    outs_s = {k: [] for k in ("ckv", "kpe", "swk", "swv", "dk", "dv", "conv")}

    ffn_w = (ffn_w_gate.astype(BF16), ffn_w_up.astype(BF16), ffn_conv_w, ffn_conv_b[:, None, :],
             ffn_w_down.astype(BF16))

    for i in range(depth):
        kind, j = i % N_MIXERS, i // N_MIXERS
        gain = norm_mix[i][None, :]
        ffn_gain = norm_ffn[i][None, :]
        if kind == 0:
            q_scale = (MLA_NOPE + MLA_ROPE) ** -0.5 * LOG2E
            pad = (-(q_rank + kv_rank + MLA_ROPE)) % LANES
            w_cat = jnp.concatenate([mla_w_dq[j], mla_w_dkv[j], jnp.zeros((d_model, pad), F32)],
                                    axis=1).astype(BF16)
            w_uq = mla_w_uq[j].reshape(q_rank, MLA_HEADS, MLA_NOPE + MLA_ROPE)
            w_uq = jnp.pad(w_uq, ((0, 0), (0, 0), (0, MLA_QK_PAD - MLA_NOPE - MLA_ROPE)))
            w_uq = w_uq.reshape(q_rank, MLA_HEADS * MLA_QK_PAD).astype(BF16)
            w_ukv = mla_w_ukv[j].reshape(kv_rank, MLA_HEADS, MLA_NOPE + MLA_V)
            wk = w_ukv[:, :, :MLA_NOPE].reshape(kv_rank, MLA_HEADS * MLA_NOPE).astype(BF16)
            wv = w_ukv[:, :, MLA_NOPE:].reshape(kv_rank, MLA_HEADS * MLA_V).astype(BF16)
            w_o = mla_w_o[j].astype(BF16)
            qn, kvn = mla_q_norm[j][None, :], mla_kv_norm[j][None, :]

            (cos_p, sin_p), (cos_s, sin_s) = tab64
            qa, ckv_p, kpe_p = _mla_down(yp, gain, w_cat, qn, kvn, cos_p, sin_p, q_rank, kv_rank)
            q = _mla_q_up(qa, w_uq, cos_p, sin_p, q_scale)
            k, vt = _mla_expand(ckv_p, kpe_p, wk, wv.T)
            o = _flash_attention(q, k, vt, batch=bp, seq=sp, n_kv_heads=MLA_HEADS, dk=MLA_QK_PAD,
                                 dv=MLA_V, groups=1, n_maps=1)
            yp, hp = _mm_res(o, w_o, yp, ffn_gain)

            qa, ckv_s, kpe_s = _mla_down(ys, gain, w_cat, qn, kvn, cos_s, sin_s, q_rank, kv_rank)
            q = _mla_q_up(qa, w_uq, cos_s, sin_s, q_scale)
            o = _mla_decode_attention(q, cache_mla_ckv.reshape(-1, kv_rank), cache_mla_kpe.reshape(-1, MLA_ROPE),
                                      ckv_s, kpe_s, wk, wv, layer=j, batch=bs, past=past, new=ss)
            ys, hs = _mm_res(o, w_o, ys, ffn_gain)
            outs_p["ckv"].append(ckv_p.reshape(bp, sp, kv_rank))
            outs_p["kpe"].append(kpe_p.reshape(bp, sp, MLA_ROPE))
            outs_s["ckv"].append(ckv_s.reshape(bs, ss, kv_rank))
            outs_s["kpe"].append(kpe_s.reshape(bs, ss, MLA_ROPE))
        elif kind == 1:
            qw, kw = SWA_HEADS * SWA_HEAD_DIM, SWA_KV_HEADS * SWA_HEAD_DIM
            w_qkv = swa_w_qkv[j].astype(BF16)
            w_o = swa_w_o[j].astype(BF16)
            sinks = swa_sinks[j]
            (cos_p, sin_p), (cos_s, sin_s) = tab64
            q_scale = SWA_HEAD_DIM ** -0.5 * LOG2E
            q, k, v, kb, vt = _qkv_proj(yp, gain, w_qkv, cos_p, sin_p, qw, kw, kw, SWA_HEAD_DIM,
                                        q_scale=q_scale, vt_head=(SWA_HEAD_DIM, LANES - SWA_HEAD_DIM),
                                        vt_blocked=False)
            o = _swa_attention(q, kb, vt, sinks, seq=sp)
            yp, hp = _mm_res(o, w_o, yp, ffn_gain)
            k3 = k.reshape(bp, sp, SWA_KV_HEADS, SWA_HEAD_DIM)
            v3 = v.reshape(bp, sp, SWA_KV_HEADS, SWA_HEAD_DIM)
            outs_p["swk"].append(k3[:, sp - WINDOW:])
            outs_p["swv"].append(v3[:, sp - WINDOW:])

            q, k, v, _, _ = _qkv_proj(ys, gain, w_qkv, cos_s, sin_s, qw, kw, kw, SWA_HEAD_DIM, q_scale=q_scale)
            kc = cache_swa_k[j].reshape(bs * WINDOW, kw)
            vc = cache_swa_v[j].reshape(bs * WINDOW, kw)
            o = _swa_decode_attention(q, kc, k, vc, v, sinks, nq=ss, na=WINDOW, nb=ss)
            ys, hs = _mm_res(o, w_o, ys, ffn_gain)
            k_all = jnp.concatenate([cache_swa_k[j], k.reshape(bs, ss, SWA_KV_HEADS, SWA_HEAD_DIM)], axis=1)
            v_all = jnp.concatenate([cache_swa_v[j], v.reshape(bs, ss, SWA_KV_HEADS, SWA_HEAD_DIM)], axis=1)
            outs_s["swk"].append(k_all[:, ss:])
            outs_s["swv"].append(v_all[:, ss:])
        else:
            lam_init = 0.8 - 0.6 * math.exp(-0.3 * i)
            q_scale = DIFF_HEAD_DIM ** -0.5 * LOG2E
            groups = DIFF_HEADS // DIFF_KV_HEADS
            qw = DIFF_HEADS * 2 * DIFF_HEAD_DIM
            kw = DIFF_KV_HEADS * 2 * DIFF_HEAD_DIM
            w_qkv = diff_w_qkv[j].astype(BF16)
            w_o = diff_w_o[j].astype(BF16)
            lam = jnp.stack([diff_lambda_q1[j], diff_lambda_k1[j], diff_lambda_q2[j], diff_lambda_k2[j]])
            subln = diff_subln[j][None, :]
            common = dict(n_kv_heads=DIFF_KV_HEADS, dk=DIFF_HEAD_DIM, dv=2 * DIFF_HEAD_DIM, groups=groups,
                          n_maps=2, lam=lam, subln=subln, lam_init=lam_init)
            (cos_p, sin_p), (cos_s, sin_s) = tab128
            q, k, v, kb, vt = _qkv_proj(yp, gain, w_qkv, cos_p, sin_p, qw, kw, kw, DIFF_HEAD_DIM,
                                        q_scale=q_scale, vt_head=(2 * DIFF_HEAD_DIM, ONES_ROWS))
            o = _flash_attention(q, kb, vt, batch=bp, seq=sp, **common)
            yp, hp = _mm_res(o, w_o, yp, ffn_gain)
            outs_p["dk"].append(k.reshape(bp, sp, DIFF_KV_HEADS, 2, DIFF_HEAD_DIM))
            outs_p["dv"].append(v.reshape(bp, sp, DIFF_KV_HEADS, 2 * DIFF_HEAD_DIM))

            q, k, v, _, _ = _qkv_proj(ys, gain, w_qkv, cos_s, sin_s, qw, kw, kw, DIFF_HEAD_DIM,
                                      q_scale=q_scale)
            o = _decode_attention(q, cache_diff_k[j].reshape(bs * past, kw),
                                  cache_diff_v[j].reshape(bs * past, kw), k, v,
                                  batch=bs, past=past, new=ss, **common)
            ys, hs = _mm_res(o, w_o, ys, ffn_gain)
            outs_s["dk"].append(k.reshape(bs, ss, DIFF_KV_HEADS, 2, DIFF_HEAD_DIM))
            outs_s["dv"].append(v.reshape(bs, ss, DIFF_KV_HEADS, 2 * DIFF_HEAD_DIM))

        final_gain = final_norm[None, :] if i == depth - 1 else None
        yp, conv_p = _conv_ffn(yp, hp, sp, jnp.zeros((bp, CONV_W - 1, d_ff), F32), i, *ffn_w, final_gain)
        ys, conv_s = _conv_ffn(ys, hs, ss, state_ffn_conv[i], i, *ffn_w, final_gain)
        outs_p["conv"].append(conv_p)
        outs_s["conv"].append(conv_s)

    y_prompt = yp.reshape(bp, sp, d_model)
    y_sample = ys.reshape(bs, ss, d_model)
    order = ("ckv", "kpe", "swk", "swv", "dk", "dv", "conv")
    stack = lambda xs: xs[0][None] if len(xs) == 1 else jnp.stack(xs)
    return (y_prompt, y_sample) + tuple(stack(outs_p[k]) for k in order) + tuple(
        stack(outs_s[k]) for k in order)
```

```python
import functools
import math

import jax
import jax.numpy as jnp
from jax import lax
from jax.experimental import pallas as pl
from jax.experimental.pallas import tpu as pltpu

F32 = jnp.float32
BF16 = jnp.bfloat16

CHUNK = 64
ROPE_THETA = 10000.0
NORM_EPS = 1e-6
SUBLN_EPS = 1e-5
N_MIXERS = 3
MLA_HEADS = 16
MLA_NOPE = 128
MLA_ROPE = 64
MLA_V = 128
SWA_HEADS = 32
SWA_KV_HEADS = 4
SWA_HEAD_DIM = 64
WINDOW = 128
DIFF_HEADS = 8
DIFF_KV_HEADS = 4
DIFF_HEAD_DIM = 128
CONV_W = 3

LANES = 128
SUBLANES = 8
MXU_COLS = 256
MLA_QK_PAD = MXU_COLS
VMEM_LIMIT_BYTES = 56 * 1024 * 1024
LOG2E = 1.4426950408889634

ROW_TILE = 512
FFN_ROW_TILE = 1024
FFN_COL_TILE = 512
FFN_SUB_TILE = 256
ATTN_Q_TILE = 2048
ATTN_KV_TILE = 512
ATTN_CHAIN = 256
ATTN_LOOKAHEAD = 3
ONES_ROWS = 16
MLA_DECODE_KV_TILE = 1024
MM_TILE = 512
MM_WEIGHT_BLOCK_BYTES = 12 * 1024 * 1024
MM_SUB_TILE = 256
NORM_SUB_TILE = 128


def _cparams(n_axes):
    return pltpu.CompilerParams(dimension_semantics=("arbitrary",) * n_axes,
                                vmem_limit_bytes=VMEM_LIMIT_BYTES)


def _resident(shape):
    return pl.BlockSpec(shape, lambda *_: (0,) * len(shape), pipeline_mode=pl.Buffered(1))


def _tile(n, pref):
    t = min(n, pref)
    assert n % t == 0, (n, pref)
    return t


def _rms(x, g, eps):
    ms = jnp.mean(x * x, axis=-1, keepdims=True)
    return (x * lax.rsqrt(ms + eps)) * g


def _rope_slab(x, cos, sin, head_dim):
    if head_dim == LANES:
        swapped = pltpu.roll(x, LANES // 2, axis=1)
    else:
        half = head_dim // 2
        lane = lax.broadcasted_iota(jnp.int32, x.shape, 1)
        first = (lane & (head_dim - 1)) < half
        swapped = jnp.where(first, pltpu.roll(x, LANES - half, axis=1), pltpu.roll(x, half, axis=1))
    return x * cos + swapped * sin


def _rope_tables(seq_len, offset, head_dim, rows):
    half = head_dim // 2
    pos = jnp.arange(seq_len, dtype=F32) + offset
    inv = ROPE_THETA ** (-jnp.arange(0, head_dim, 2, dtype=F32) / head_dim)
    ang = pos[:, None] * inv[None, :]
    cos, sin = jnp.cos(ang), jnp.sin(ang)
    reps = LANES // head_dim
    cos_l = jnp.tile(jnp.concatenate([cos, cos], axis=1), (1, reps))
    sin_l = jnp.tile(jnp.concatenate([-sin, sin], axis=1), (1, reps))
    n = max(rows // seq_len, 1)
    return jnp.tile(cos_l, (n, 1)), jnp.tile(sin_l, (n, 1))


def _mla_down_kernel(x_ref, g_ref, w_ref, qn_ref, kvn_ref, cos_ref, sin_ref,
                     qa_ref, ckv_ref, kpe_ref, *, q_rank, kv_rank, sub):
    def project(r0):
        h = _rms(x_ref[r0:r0 + sub, :], g_ref[...], NORM_EPS).astype(BF16)
        return jnp.dot(h, w_ref[...], preferred_element_type=F32)

    starts = list(range(0, x_ref.shape[0], sub))
    pending = project(starts[0])
    for t, r0 in enumerate(starts):
        d = pending
        if t + 1 < len(starts):
            pending = project(starts[t + 1])
        rows = slice(r0, r0 + sub)
        qa_ref[rows, :] = _rms(d[:, :q_rank], qn_ref[...], NORM_EPS).astype(BF16)
        ckv_ref[rows, :] = _rms(d[:, q_rank:q_rank + kv_rank], kvn_ref[...], NORM_EPS)
        slab = d[:, q_rank + kv_rank:q_rank + kv_rank + LANES]
        kpe_ref[rows, :] = _rope_slab(slab, cos_ref[rows, :], sin_ref[rows, :], MLA_ROPE)[:, :MLA_ROPE]


def _mla_down(x, gain, w_cat, q_norm, kv_norm, cos, sin, q_rank, kv_rank):
    m, d_model = x.shape
    tm = _tile(m, ROW_TILE)
    n_tab = cos.shape[0] // tm
    wn = w_cat.shape[1]
    row = lambda i: (i, 0)
    fixed = lambda i: (0, 0)
    tab = lambda i: (i % n_tab, 0)
    return pl.pallas_call(
        functools.partial(_mla_down_kernel, q_rank=q_rank, kv_rank=kv_rank, sub=_tile(tm, NORM_SUB_TILE)),
        grid=(m // tm,),
        in_specs=[pl.BlockSpec((tm, d_model), row), pl.BlockSpec((1, d_model), fixed),
                  _resident((d_model, wn)), pl.BlockSpec((1, q_rank), fixed),
                  pl.BlockSpec((1, kv_rank), fixed), pl.BlockSpec((tm, LANES), tab),
                  pl.BlockSpec((tm, LANES), tab)],
        out_specs=[pl.BlockSpec((tm, q_rank), row), pl.BlockSpec((tm, kv_rank), row),
                   pl.BlockSpec((tm, MLA_ROPE), row)],
        out_shape=[jax.ShapeDtypeStruct((m, q_rank), BF16), jax.ShapeDtypeStruct((m, kv_rank), F32),
                   jax.ShapeDtypeStruct((m, MLA_ROPE), F32)],
        compiler_params=_cparams(1), name="mla_down",
    )(x, gain, w_cat, q_norm, kv_norm, cos, sin)


def _mla_q_up_kernel(qa_ref, w_ref, cos_ref, sin_ref, q_ref, *, q_scale):
    qa = qa_ref[...]
    cos, sin = cos_ref[...], sin_ref[...]
    for h in range(MLA_HEADS):
        c0 = h * MLA_QK_PAD
        d = jnp.dot(qa, w_ref[:, c0:c0 + MLA_QK_PAD], preferred_element_type=F32)
        q_ref[:, c0:c0 + LANES] = (d[:, :LANES] * q_scale).astype(BF16)
        q_ref[:, c0 + LANES:c0 + MLA_QK_PAD] = (_rope_slab(d[:, LANES:], cos, sin, MLA_ROPE) * q_scale).astype(BF16)


def _mla_q_up(qa, w_pad, cos, sin, q_scale):
    m, q_rank = qa.shape
    tm = _tile(m, ROW_TILE)
    n_tab = cos.shape[0] // tm
    n = w_pad.shape[1]
    row = lambda i: (i, 0)
    fixed = lambda i: (0, 0)
    tab = lambda i: (i % n_tab, 0)
    return pl.pallas_call(
        functools.partial(_mla_q_up_kernel, q_scale=q_scale), grid=(m // tm,),
        in_specs=[pl.BlockSpec((tm, q_rank), row), _resident((q_rank, n)),
                  pl.BlockSpec((tm, LANES), tab), pl.BlockSpec((tm, LANES), tab)],
        out_specs=pl.BlockSpec((tm, n), row),
        out_shape=jax.ShapeDtypeStruct((m, n), BF16),
        compiler_params=_cparams(1), name="mla_q_up",
    )(qa, w_pad, cos, sin)


def _mla_expand_kernel(ckv_ref, kpe_ref, wk_ref, wvt_ref, k_ref, vt_ref):
    c = ckv_ref[...].astype(BF16)
    kpe = kpe_ref[...].astype(BF16)
    tm = c.shape[0]
    dva = MLA_V + ONES_ROWS
    ones = jnp.ones((ONES_ROWS, tm), BF16)
    kpe_pad = jnp.concatenate([kpe, jnp.zeros((tm, MLA_QK_PAD - MLA_NOPE - MLA_ROPE), BF16)], axis=1)
    pair = MXU_COLS // MLA_NOPE

    def project(h0):
        kn = jnp.dot(c, wk_ref[:, h0 * MLA_NOPE:(h0 + pair) * MLA_NOPE], preferred_element_type=F32)
        vt = lax.dot_general(wvt_ref[h0 * MLA_V:(h0 + pair) * MLA_V, :], c, (((1,), (1,)), ((), ())),
                             preferred_element_type=F32)
        return kn, vt

    pending = project(0)
    for h0 in range(0, MLA_HEADS, pair):
        kn, vt = pending
        if h0 + pair < MLA_HEADS:
            pending = project(h0 + pair)
        for i in range(pair):
            h = h0 + i
            c0 = h * MLA_QK_PAD
            k_ref[:, c0:c0 + MLA_NOPE] = kn[:, i * MLA_NOPE:(i + 1) * MLA_NOPE].astype(BF16)
            k_ref[:, c0 + MLA_NOPE:c0 + MLA_QK_PAD] = kpe_pad
            vt_ref[0, h * dva:h * dva + MLA_V, :] = vt[i * MLA_V:(i + 1) * MLA_V].astype(BF16)
            vt_ref[0, h * dva + MLA_V:(h + 1) * dva, :] = ones


def _mla_expand(ckv, kpe, wk, wvt):
    m, kv_rank = ckv.shape
    tm = _tile(m, ATTN_KV_TILE)
    row = lambda i: (i, 0)
    nk, nva = MLA_HEADS * MLA_QK_PAD, MLA_HEADS * (MLA_V + ONES_ROWS)
    return pl.pallas_call(
        _mla_expand_kernel, grid=(m // tm,),
        in_specs=[pl.BlockSpec((tm, kv_rank), row), pl.BlockSpec((tm, MLA_ROPE), row),
                  _resident(wk.shape), _resident(wvt.shape)],
        out_specs=[pl.BlockSpec((tm, nk), row), pl.BlockSpec((1, nva, tm), lambda i: (i, 0, 0))],
        out_shape=[jax.ShapeDtypeStruct((m, nk), BF16), jax.ShapeDtypeStruct((m // tm, nva, tm), BF16)],
        compiler_params=_cparams(1), name="mla_expand",
    )(ckv, kpe, wk, wvt)


def _qkv_proj_kernel(x_ref, g_ref, w_ref, cos_ref, sin_ref, q_ref, k_ref, v_ref, kb_ref, vb_ref,
                     *, qw, kw, vw, head_dim, q_scale, vt_head, sub):
    vt_dst = None
    if vt_head:
        vt_dst = vb_ref.at[0] if len(vb_ref.shape) == 3 else vb_ref
    for s0 in range(0, x_ref.shape[0], sub):
        rows = slice(s0, s0 + sub)
        h = _rms(x_ref[rows, :], g_ref[...], NORM_EPS).astype(BF16)
        cos, sin = cos_ref[rows, :], sin_ref[rows, :]
        for n0 in range(0, qw + kw + vw, MXU_COLS):
            dd = jnp.dot(h, w_ref[:, n0:n0 + MXU_COLS], preferred_element_type=F32)
            for c0 in range(n0, n0 + MXU_COLS, LANES):
                d = dd[:, c0 - n0:c0 - n0 + LANES]
                if c0 < qw:
                    q_ref[rows, c0:c0 + LANES] = (_rope_slab(d, cos, sin, head_dim) * q_scale).astype(BF16)
                elif c0 < qw + kw:
                    r = _rope_slab(d, cos, sin, head_dim)
                    k_ref[rows, c0 - qw:c0 - qw + LANES] = r
                    kb_ref[rows, c0 - qw:c0 - qw + LANES] = r.astype(BF16)
                else:
                    c1 = c0 - qw - kw
                    v_ref[rows, c1:c1 + LANES] = d
                    if vt_head:
                        vd, ones_rows = vt_head
                        dt = d.T.astype(BF16)
                        for hv in range(c1 // vd, (c1 + LANES - 1) // vd + 1):
                            lo, hi = max(c1, hv * vd), min(c1 + LANES, (hv + 1) * vd)
                            r0 = hv * (vd + ones_rows) + lo - hv * vd
                            vt_dst[r0:r0 + hi - lo, rows] = dt[lo - c1:hi - c1]
                            if hi == (hv + 1) * vd:
                                vt_dst[r0 + hi - lo:r0 + hi - lo + ones_rows, rows] = jnp.ones((ones_rows, sub), BF16)
                    else:
                        vb_ref[rows, c1:c1 + LANES] = d.astype(BF16)


def _qkv_proj(x, gain, w, cos, sin, qw, kw, vw, head_dim, *, q_scale=1.0, vt_head=None, vt_blocked=True):
    m, d_model = x.shape
    tm = _tile(m, ATTN_KV_TILE if vt_head else ROW_TILE)
    n_tab = cos.shape[0] // tm
    row = lambda i: (i, 0)
    fixed = lambda i: (0, 0)
    tab = lambda i: (i % n_tab, 0)
    if vt_head:
        vwa = vw // vt_head[0] * (vt_head[0] + vt_head[1])
        if vt_blocked:
            vb_spec = pl.BlockSpec((1, vwa, tm), lambda i: (i, 0, 0))
            vb_shape = jax.ShapeDtypeStruct((m // tm, vwa, tm), BF16)
        else:
            vb_spec = pl.BlockSpec((vwa, tm), lambda i: (0, i))
            vb_shape = jax.ShapeDtypeStruct((vwa, m), BF16)
    else:
        vb_spec = pl.BlockSpec((tm, vw), row)
        vb_shape = jax.ShapeDtypeStruct((m, vw), BF16)
    return pl.pallas_call(
        functools.partial(_qkv_proj_kernel, qw=qw, kw=kw, vw=vw, head_dim=head_dim, q_scale=q_scale,
                          vt_head=vt_head, sub=_tile(tm, MM_SUB_TILE)),
        grid=(m // tm,),
        in_specs=[pl.BlockSpec((tm, d_model), row), pl.BlockSpec((1, d_model), fixed),
                  _resident(w.shape), pl.BlockSpec((tm, LANES), tab),
                  pl.BlockSpec((tm, LANES), tab)],
        out_specs=[pl.BlockSpec((tm, qw), row), pl.BlockSpec((tm, kw), row), pl.BlockSpec((tm, vw), row),
                   pl.BlockSpec((tm, kw), row), vb_spec],
        out_shape=[jax.ShapeDtypeStruct((m, qw), BF16), jax.ShapeDtypeStruct((m, kw), F32),
                   jax.ShapeDtypeStruct((m, vw), F32), jax.ShapeDtypeStruct((m, kw), BF16), vb_shape],
        compiler_params=_cparams(1), name="qkv_proj",
    )(x, gain, w, cos, sin)


def _mm_res_kernel(*refs, sub, with_norm, norm_only):
    o_ref = g_ref = h_ref = None
    if norm_only:
        a_ref, w_ref, r_ref, g_ref, h_ref = refs
    elif with_norm:
        a_ref, w_ref, r_ref, g_ref, o_ref, h_ref = refs
    else:
        a_ref, w_ref, r_ref, o_ref = refs
    starts = list(range(0, a_ref.shape[0], sub))
    pending = jnp.dot(a_ref[0:sub, :], w_ref[...], preferred_element_type=F32)
    for t, r0 in enumerate(starts):
        d = pending
        if t + 1 < len(starts):
            r1 = starts[t + 1]
            pending = jnp.dot(a_ref[r1:r1 + sub, :], w_ref[...], preferred_element_type=F32)
        y = r_ref[r0:r0 + sub, :] + d
        if o_ref is not None:
            o_ref[r0:r0 + sub, :] = y
        if h_ref is not None:
            h_ref[r0:r0 + sub, :] = _rms(y, g_ref[...], NORM_EPS).astype(h_ref.dtype)


def _mm_res(a, w, res, norm_gain=None, layer=None, norm_only=False):
    m, k = a.shape
    n = w.shape[-1]
    tm = _tile(m, MM_TILE // 2 if norm_only else MM_TILE)
    tn = n
    while not norm_only and tn * k * w.dtype.itemsize > MM_WEIGHT_BLOCK_BYTES and tn % (2 * MXU_COLS) == 0:
        tn //= 2
    with_norm = norm_gain is not None
    assert not with_norm or tn == n
    tile = pl.BlockSpec((tm, tn), lambda j, i: (i, j))
    once = dict(pipeline_mode=pl.Buffered(1)) if norm_only else {}
    if layer is None:
        w_spec = pl.BlockSpec((k, tn), lambda j, i: (0, j), **once)
    else:
        w_spec = pl.BlockSpec((None, k, tn), lambda j, i: (layer, 0, j), **once)
    in_specs = [pl.BlockSpec((tm, k), lambda j, i: (i, 0)), w_spec, tile]
    out_specs, out_shape, args = tile, jax.ShapeDtypeStruct((m, n), F32), [a, w, res]
    if with_norm:
        in_specs.append(pl.BlockSpec((1, n), lambda j, i: (0, 0)))
        out_specs, out_shape = [tile, tile], [out_shape, jax.ShapeDtypeStruct((m, n), BF16)]
        if norm_only:
            out_specs, out_shape = tile, jax.ShapeDtypeStruct((m, n), F32)
        args.append(norm_gain)
    return pl.pallas_call(
        functools.partial(_mm_res_kernel, sub=_tile(tm, MM_SUB_TILE), with_norm=with_norm, norm_only=norm_only),
        grid=(n // tn, m // tm), in_specs=in_specs, out_specs=out_specs, out_shape=out_shape,
        compiler_params=_cparams(2), name="mm_res",
    )(*args)


def _diff_lambda(lam_ref, lam_init):
    lam = lam_ref[...]
    a = jnp.sum(lam[0:1] * lam[1:2], axis=-1, keepdims=True)
    b = jnp.sum(lam[2:3] * lam[3:4], axis=-1, keepdims=True)
    return jnp.exp(a) - jnp.exp(b) + lam_init


def _combine_heads(o_list, o_ref, rows, lam_ref, subln_ref, *, n_maps, dv, lam_init):
    groups = len(o_list) // n_maps
    for g in range(groups):
        if n_maps == 1:
            o = o_list[g]
        else:
            lam = _diff_lambda(lam_ref, lam_init)
            o = o_list[2 * g] - lam * o_list[2 * g + 1]
            o = _rms(o, subln_ref[...], SUBLN_EPS) * (1.0 - lam_init)
        o_ref[rows, g * dv:(g + 1) * dv] = o.astype(o_ref.dtype)


def _flash_kernel(*refs, tq, tk, cw, dk, dv, groups, n_maps, lam_init):
    if n_maps == 2:
        lam_ref, subln_ref, q_ref, k_ref, vt_ref, o_ref, m_ref, acc_ref, st_ref, mx_ref = refs
    else:
        q_ref, k_ref, vt_ref, o_ref, m_ref, acc_ref, st_ref, mx_ref = refs
        lam_ref = subln_ref = None
    n_sub = groups * n_maps
    n_chain = tq // cw
    kv_per_q = tq // tk
    qi = pl.program_id(2)
    nt = (((1,), (1,)), ((), ()))

    m_ref[...] = jnp.full(m_ref.shape, -jnp.inf, F32)
    acc_ref[...] = jnp.zeros(acc_ref.shape, F32)

    def scores(item, k_blks, slot):
        u, n, k_lo, bi, nk = item
        mi = u % n_maps
        q_n = q_ref[n * cw:(n + 1) * cw, u * dk:(u + 1) * dk]
        st = lax.dot_general(k_blks[bi][:nk, mi * dk:(mi + 1) * dk], q_n, nt, preferred_element_type=F32)
        if k_lo is not None:
            kc = (lax.broadcasted_iota(jnp.int32, st.shape, 0) + k_lo) // CHUNK
            qc = (lax.broadcasted_iota(jnp.int32, st.shape, 1) + n * cw) // CHUNK
            st = jnp.where(kc <= qc, st, -jnp.inf)
        st_ref[slot, 0:nk, :] = st
        mx_ref[slot] = jnp.max(st, axis=0, keepdims=True)

    def update(item, slot, vt_blks):
        u, n, _, bi, nk = item
        st = st_ref[slot, 0:nk, :]
        idx = u * n_chain + n
        m_prev = m_ref[idx]
        m_new = jnp.maximum(m_prev, mx_ref[slot])
        alpha = jnp.exp2(m_prev - m_new)
        pt = jnp.exp2(st - m_new).astype(BF16)
        acc_ref[idx] = acc_ref[idx] * alpha + jnp.dot(vt_blks[bi][:, :nk], pt, preferred_element_type=F32)
        m_ref[idx] = m_new

    def run_chains(items, first_block):
        blocks = sorted({it[3] for it in items})
        k_blks = {bi: k_ref[pl.ds(pl.multiple_of((first_block + bi) * tk, tk), tk), :] for bi in blocks}
        vt_blks = {bi: vt_ref[first_block + bi] for bi in blocks}
        n_slots = ATTN_LOOKAHEAD + 1
        for t in range(min(ATTN_LOOKAHEAD, len(items))):
            scores(items[t], k_blks, t % n_slots)
        for t, item in enumerate(items):
            if t + ATTN_LOOKAHEAD < len(items):
                scores(items[t + ATTN_LOOKAHEAD], k_blks, (t + ATTN_LOOKAHEAD) % n_slots)
            update(item, t % n_slots, vt_blks)

    def body(j, carry):
        items = [(u, n, None, bi, tk) for bi in range(kv_per_q) for n in range(n_chain) for u in range(n_sub)]
        run_chains(items, j * kv_per_q)
        return carry

    lax.fori_loop(0, qi, body, 0)

    items = [(u, n, bi * tk if (bi + 1) * tk > n * cw else None, bi, min(tk, (n + 1) * cw - bi * tk))
             for bi in range(kv_per_q) for n in range(n_chain) if bi * tk < (n + 1) * cw
             for u in range(n_sub)]
    run_chains(items, qi * kv_per_q)

    for n in range(n_chain):
        outs = []
        for u in range(n_sub):
            a = acc_ref[u * n_chain + n]
            outs.append((a[:dv] / a[dv:dv + 1]).T)
        _combine_heads(outs, o_ref, slice(n * cw, (n + 1) * cw), lam_ref, subln_ref,
                       n_maps=n_maps, dv=dv, lam_init=lam_init)


def _flash_attention(q, k, vt, *, batch, seq, n_kv_heads, dk, dv, groups, n_maps,
                     lam=None, subln=None, lam_init=0.0):
    tk = vt.shape[2]
    dva = dv + ONES_ROWS
    n_sub = groups * n_maps
    tq = _tile(seq, max(ATTN_Q_TILE if n_sub == 1 else ATTN_Q_TILE // 2, tk))
    cw = _tile(tq, ATTN_CHAIN)
    assert tq % tk == 0 and tk % CHUNK == 0 and cw % CHUNK == 0
    nq, nk = seq // tq, seq // tk
    n_chain = tq // cw
    kern = functools.partial(_flash_kernel, tq=tq, tk=tk, cw=cw, dk=dk, dv=dv, groups=groups,
                             n_maps=n_maps, lam_init=lam_init)
    in_specs = [pl.BlockSpec((tq, n_sub * dk), lambda b, h, i: (b * nq + i, h)),
                pl.BlockSpec((seq, n_maps * dk), lambda b, h, i: (b, h)),
                pl.BlockSpec((nk, dva, tk), lambda b, h, i: (b, h, 0))]
    args = [q, k, vt]
    if n_maps == 2:
        in_specs = [pl.BlockSpec(lam.shape, lambda b, h, i: (0, 0)),
                    pl.BlockSpec(subln.shape, lambda b, h, i: (0, 0))] + in_specs
        args = [lam, subln] + args
    return pl.pallas_call(
        kern, grid=(batch, n_kv_heads, nq), in_specs=in_specs,
        out_specs=pl.BlockSpec((tq, groups * dv), lambda b, h, i: (b * nq + i, h)),
        out_shape=jax.ShapeDtypeStruct((batch * seq, n_kv_heads * groups * dv), BF16),
        scratch_shapes=[pltpu.VMEM((n_sub * n_chain, 1, cw), F32),
                        pltpu.VMEM((n_sub * n_chain, dva, cw), F32),
                        pltpu.VMEM((ATTN_LOOKAHEAD + 1, tk, cw), F32),
                        pltpu.VMEM((ATTN_LOOKAHEAD + 1, 1, cw), F32)],
        compiler_params=_cparams(3), name="flash_attention",
    )(*args)


def _decode_kernel(*refs, dk, dv, groups, n_maps, lam_init):
    if n_maps == 2:
        lam_ref, subln_ref, q_ref, kc_ref, vc_ref, kn_ref, vn_ref, o_ref = refs
    else:
        q_ref, kc_ref, vc_ref, kn_ref, vn_ref, o_ref = refs
        lam_ref = subln_ref = None
    nt = (((1,), (1,)), ((), ()))
    vc = vc_ref[...].astype(BF16)
    vn = vn_ref[...].astype(BF16)
    outs = []
    for u in range(groups * n_maps):
        mi = u % n_maps
        q = q_ref[:, u * dk:(u + 1) * dk]
        kc = kc_ref[:, mi * dk:(mi + 1) * dk].astype(BF16)
        kn = kn_ref[:, mi * dk:(mi + 1) * dk].astype(BF16)
        s1 = lax.dot_general(q, kc, nt, preferred_element_type=F32)
        s2 = lax.dot_general(q, kn, nt, preferred_element_type=F32)
        m = jnp.maximum(jnp.max(s1, axis=1, keepdims=True), jnp.max(s2, axis=1, keepdims=True))
        p1 = jnp.exp2(s1 - m)
        p2 = jnp.exp2(s2 - m)
        l = jnp.sum(p1, axis=1, keepdims=True) + jnp.sum(p2, axis=1, keepdims=True)
        o = (jnp.dot(p1.astype(BF16), vc, preferred_element_type=F32)
             + jnp.dot(p2.astype(BF16), vn, preferred_element_type=F32))
        outs.append(o / l)
    _combine_heads(outs, o_ref, slice(None), lam_ref, subln_ref, n_maps=n_maps, dv=dv, lam_init=lam_init)


def _decode_attention(q, kc, vc, kn, vn, *, batch, past, new, n_kv_heads, dk, dv, groups, n_maps,
                      lam=None, subln=None, lam_init=0.0):
    n_sub = groups * n_maps
    kern = functools.partial(_decode_kernel, dk=dk, dv=dv, groups=groups, n_maps=n_maps, lam_init=lam_init)
    bh = lambda b, h: (b, h)
    in_specs = [pl.BlockSpec((new, n_sub * dk), bh), pl.BlockSpec((past, n_maps * dk), bh),
                pl.BlockSpec((past, dv), bh), pl.BlockSpec((new, n_maps * dk), bh),
                pl.BlockSpec((new, dv), bh)]
    args = [q, kc, vc, kn, vn]
    if n_maps == 2:
        in_specs = [pl.BlockSpec(lam.shape, lambda b, h: (0, 0)),
                    pl.BlockSpec(subln.shape, lambda b, h: (0, 0))] + in_specs
        args = [lam, subln] + args
    return pl.pallas_call(
        kern, grid=(batch, n_kv_heads), in_specs=in_specs,
        out_specs=pl.BlockSpec((new, groups * dv), bh),
        out_shape=jax.ShapeDtypeStruct((batch * new, n_kv_heads * groups * dv), BF16),
        compiler_params=_cparams(2), name="decode_attention",
    )(*args)


def _mla_decode_kernel(q_ref, ckv_c_ref, kpe_c_ref, ckv_n_ref, kpe_n_ref, wk_ref, wv_ref, o_ref,
                       kcat_ref, qcat_ref, *, block):
    new = q_ref.shape[0]
    past, rank = ckv_c_ref.shape
    nt = (((1,), (1,)), ((), ()))
    kcat_ref[0:past, 0:rank] = ckv_c_ref[...].astype(BF16)
    kcat_ref[past:past + new, 0:rank] = ckv_n_ref[...].astype(BF16)
    kcat_ref[0:past, rank:rank + MLA_ROPE] = kpe_c_ref[...].astype(BF16)
    kcat_ref[past:past + new, rank:rank + MLA_ROPE] = kpe_n_ref[...].astype(BF16)
    kcat_ref[:, rank + MLA_ROPE:rank + LANES] = jnp.zeros((past + new, LANES - MLA_ROPE), BF16)
    for h in range(MLA_HEADS):
        c0 = h * MLA_QK_PAD
        q_lat = lax.dot_general(q_ref[:, c0:c0 + MLA_NOPE], wk_ref[:, h * MLA_NOPE:(h + 1) * MLA_NOPE], nt,
                                preferred_element_type=F32)
        qcat_ref[h * new:(h + 1) * new, 0:rank] = q_lat.astype(BF16)
        qcat_ref[h * new:(h + 1) * new, rank:rank + LANES] = q_ref[:, c0 + MLA_NOPE:c0 + MLA_QK_PAD]
    q = qcat_ref[...]
    rows = MLA_HEADS * new
    m = jnp.full((rows, 1), -jnp.inf, F32)
    l = jnp.zeros((rows, 1), F32)
    acc = jnp.zeros((rows, rank), F32)
    starts = list(range(0, past, block)) + [past]
    for k0 in starts:
        nk = min(block, past - k0) if k0 < past else new
        kb = kcat_ref[k0:k0 + nk, :]
        s = lax.dot_general(q, kb, nt, preferred_element_type=F32)
        m_new = jnp.maximum(m, jnp.max(s, axis=1, keepdims=True))
        alpha = jnp.exp2(m - m_new)
        p = jnp.exp2(s - m_new)
        l = alpha * l + jnp.sum(p, axis=1, keepdims=True)
        acc = alpha * acc + jnp.dot(p.astype(BF16), kb[:, 0:rank], preferred_element_type=F32)
        m = m_new
    o_lat = (acc / l).astype(BF16)
    for h in range(MLA_HEADS):
        o_ref[:, h * MLA_V:(h + 1) * MLA_V] = jnp.dot(
            o_lat[h * new:(h + 1) * new], wv_ref[:, h * MLA_V:(h + 1) * MLA_V],
            preferred_element_type=F32).astype(o_ref.dtype)


def _mla_decode_attention(q, ckv_cache, kpe_cache, ckv_new, kpe_new, wk, wv, *, layer, batch, past, new):
    rank = ckv_cache.shape[1]
    per = lambda b: (b, 0)
    cached = lambda b: (layer * batch + b, 0)
    return pl.pallas_call(
        functools.partial(_mla_decode_kernel, block=_tile(past, MLA_DECODE_KV_TILE)), grid=(batch,),
        in_specs=[pl.BlockSpec((new, q.shape[1]), per), pl.BlockSpec((past, rank), cached),
                  pl.BlockSpec((past, MLA_ROPE), cached), pl.BlockSpec((new, rank), per),
                  pl.BlockSpec((new, MLA_ROPE), per), _resident(wk.shape), _resident(wv.shape)],
        out_specs=pl.BlockSpec((new, MLA_HEADS * MLA_V), per),
        out_shape=jax.ShapeDtypeStruct((batch * new, MLA_HEADS * MLA_V), BF16),
        scratch_shapes=[pltpu.VMEM((past + new, rank + LANES), BF16),
                        pltpu.VMEM((MLA_HEADS * new, rank + LANES), BF16)],
        compiler_params=_cparams(1), name="mla_decode_attention",
    )(q, ckv_cache, kpe_cache, ckv_new, kpe_new, wk, wv)


def _swa_decode_kernel(sinks_ref, q_ref, ka_ref, kb_ref, va_ref, vb_ref, o_ref):
    nq = q_ref.shape[0]
    group = SWA_HEADS // SWA_KV_HEADS
    d = SWA_HEAD_DIM
    nt = (((1,), (1,)), ((), ()))
    for h in range(SWA_KV_HEADS):
        hs = slice(h * d, (h + 1) * d)
        k = jnp.concatenate([ka_ref[:, hs], kb_ref[:, hs]], axis=0).astype(BF16)
        v = jnp.concatenate([va_ref[:, hs], vb_ref[:, hs]], axis=0).astype(BF16)
        q = jnp.concatenate([q_ref[:, (h * group + g) * d:(h * group + g + 1) * d] for g in range(group)],
                            axis=0)
        sink = jnp.concatenate([jnp.full((nq, 1), sinks_ref[h * group + g] * LOG2E, F32)
                                for g in range(group)], axis=0)
        s = lax.dot_general(q, k, nt, preferred_element_type=F32)
        m = jnp.maximum(jnp.max(s, axis=1, keepdims=True), sink)
        p = jnp.exp2(s - m)
        l = jnp.sum(p, axis=1, keepdims=True) + jnp.exp2(sink - m)
        o = jnp.dot(p.astype(BF16), v, preferred_element_type=F32) / l
        for g in range(0, group, 2):
            pair = jnp.concatenate([o[g * nq:(g + 1) * nq], o[(g + 1) * nq:(g + 2) * nq]], axis=1)
            c0 = (h * group + g) * d
            o_ref[:, c0:c0 + 2 * d] = pair.astype(o_ref.dtype)


def _swa_decode_attention(q, ka, kb, va, vb, sinks, *, nq, na, nb):
    m, qw = q.shape
    kvw = ka.shape[1]
    cur = lambda t: (t, 0)
    return pl.pallas_call(
        _swa_decode_kernel, grid=(m // nq,),
        in_specs=[pl.BlockSpec(memory_space=pltpu.SMEM), pl.BlockSpec((nq, qw), cur),
                  pl.BlockSpec((na, kvw), cur), pl.BlockSpec((nb, kvw), cur),
                  pl.BlockSpec((na, kvw), cur), pl.BlockSpec((nb, kvw), cur)],
        out_specs=pl.BlockSpec((nq, qw), cur),
        out_shape=jax.ShapeDtypeStruct((m, qw), BF16),
        compiler_params=_cparams(1), name="swa_decode_attention",
    )(sinks, q, ka, kb, va, vb)


def _swa_kernel(sinks_ref, q_ref, ka_ref, kb_ref, vta_ref, vtb_ref, o_ref, st_ref, *, blocks_per_seq):
    nq, na = q_ref.shape[0], ka_ref.shape[0]
    nk = na + kb_ref.shape[0]
    group = SWA_HEADS // SWA_KV_HEADS
    d = SWA_HEAD_DIM
    vrows = vta_ref.shape[0] // SWA_KV_HEADS
    nt = (((1,), (1,)), ((), ()))
    kc = lax.broadcasted_iota(jnp.int32, (nk, group * nq), 0) // CHUNK
    qc = (lax.broadcasted_iota(jnp.int32, (nk, group * nq), 1) % nq) // CHUNK
    first = (pl.program_id(0) % blocks_per_seq) == 0
    lo = jnp.where(first, na // CHUNK, 0)
    valid = (kc >= qc) & (kc <= qc + na // CHUNK) & (kc >= lo)

    def scores(h):
        hs = slice(h * d, (h + 1) * d)
        k = jnp.concatenate([ka_ref[:, hs], kb_ref[:, hs]], axis=0)
        q = jnp.concatenate([q_ref[:, (h * group + g) * d:(h * group + g + 1) * d] for g in range(group)],
                            axis=0)
        st = lax.dot_general(k, q, nt, preferred_element_type=F32)
        st_ref[h % 2] = jnp.where(valid, st, -jnp.inf)

    scores(0)
    for h in range(SWA_KV_HEADS):
        if h + 1 < SWA_KV_HEADS:
            scores(h + 1)
        st = st_ref[h % 2]
        sink = jnp.concatenate([jnp.full((1, nq), sinks_ref[h * group + g] * LOG2E, F32)
                                for g in range(group)], axis=1)
        m = jnp.maximum(jnp.max(st, axis=0, keepdims=True), sink)
        pt = jnp.exp2(st - m).astype(BF16)
        vt = jnp.concatenate([vta_ref[h * vrows:(h + 1) * vrows, :], vtb_ref[h * vrows:(h + 1) * vrows, :]],
                             axis=1)
        ot = jnp.dot(vt, pt, preferred_element_type=F32)
        l = ot[d:d + 1] + jnp.exp2(sink - m)
        o = ot[:d] / l
        for g in range(0, group, 2):
            pair = jnp.concatenate([o[:, g * nq:(g + 1) * nq], o[:, (g + 1) * nq:(g + 2) * nq]], axis=0)
            c0 = (h * group + g) * d
            o_ref[:, c0:c0 + 2 * d] = pair.T.astype(o_ref.dtype)


def _swa_attention(q, k, vt, sinks, *, seq):
    m, qw = q.shape
    kvw = k.shape[1]
    nq = WINDOW
    cur = lambda t: (t, 0)
    prev = lambda t: (jnp.maximum(t - 1, 0), 0)
    cur_t = lambda t: (0, t)
    prev_t = lambda t: (0, jnp.maximum(t - 1, 0))
    return pl.pallas_call(
        functools.partial(_swa_kernel, blocks_per_seq=seq // nq),
        grid=(m // nq,),
        in_specs=[pl.BlockSpec(memory_space=pltpu.SMEM), pl.BlockSpec((nq, qw), cur),
                  pl.BlockSpec((nq, kvw), prev), pl.BlockSpec((nq, kvw), cur),
                  pl.BlockSpec((vt.shape[0], nq), prev_t), pl.BlockSpec((vt.shape[0], nq), cur_t)],
        out_specs=pl.BlockSpec((nq, qw), cur),
        out_shape=jax.ShapeDtypeStruct((m, qw), BF16),
        scratch_shapes=[pltpu.VMEM((2, 2 * nq, SWA_HEADS // SWA_KV_HEADS * nq), F32)],
        compiler_params=_cparams(1), name="swa_attention",
    )(sinks, q, k, k, vt, vt)


def _ffn_up_kernel(h_ref, wg_ref, wu_ref, cw_ref, cb_ref, prev_ref, act_ref, tail_ref,
                   halo_ref, carry_ref, *, rows, n_slab, tiles_per_seq, sub):
    i, j = pl.program_id(0), pl.program_id(1)
    halo = SUBLANES

    w0, w1, w2 = cw_ref[0:1, :], cw_ref[1:2, :], cw_ref[2:3, :]
    bias = cb_ref[...]
    tm = h_ref.shape[0]

    def matmuls(r0):
        h = h_ref[r0:r0 + sub, :]
        return (jnp.dot(h, wg_ref[...], preferred_element_type=F32),
                jnp.dot(h, wu_ref[...], preferred_element_type=F32))

    def conv_act(gs, us, before):
        r8 = lax.broadcasted_iota(jnp.int32, (halo, gs.shape[1]), 0)
        g1 = pltpu.roll(gs, 1, axis=0)
        g2 = pltpu.roll(gs, 2, axis=0)
        head1 = jnp.where(r8 == 0, before[halo - 1:halo], g1[:halo])
        head2 = jnp.where(r8 == 0, before[halo - 2:halo - 1], jnp.where(r8 == 1, before[halo - 1:halo], g2[:halo]))
        g1 = jnp.concatenate([head1, g1[halo:]], axis=0)
        g2 = jnp.concatenate([head2, g2[halo:]], axis=0)
        conv = ((bias + g2 * w0) + g1 * w1) + gs * w2
        return (conv * jax.nn.sigmoid(conv)) * us

    if n_slab == 1:
        is_start = (i % tiles_per_seq) == 0

        @pl.when(is_start)
        def _():
            halo_ref[...] = prev_ref[0]

        @pl.when(jnp.logical_not(is_start))
        def _():
            halo_ref[...] = carry_ref[j]

    starts = list(range(0, tm, sub))
    pending = matmuls(starts[0])
    before = halo_ref[...] if n_slab == 1 else None
    for t, r0 in enumerate(starts):
        gate, up = pending
        if t + 1 < len(starts):
            pending = matmuls(starts[t + 1])
        for s0 in range(0, sub, rows):
            gs, us = gate[s0:s0 + rows], up[s0:s0 + rows]
            if n_slab > 1:
                before = prev_ref[(r0 + s0) // rows]
            act_ref[r0 + s0:r0 + s0 + rows, :] = conv_act(gs, us, before).astype(BF16)
            before = gs[rows - halo:rows]
            if n_slab > 1:
                tail_ref[(r0 + s0) // rows] = before
    if n_slab == 1:
        tail_ref[0] = before
        carry_ref[j] = before


def _ffn_up(h, wg, wu, conv_w, conv_b, prev, *, layer, seq):
    m, d_model = h.shape
    d_ff = wg.shape[-1]
    tm = _tile(m, FFN_ROW_TILE)
    tf = _tile(d_ff, FFN_COL_TILE)
    if tm >= seq:
        rows, n_slab, tiles_per_seq = seq, tm // seq, 1
        prev_spec = pl.BlockSpec((n_slab, SUBLANES, tf), lambda i, j: (i, 0, j))
    else:
        rows, n_slab, tiles_per_seq = tm, 1, seq // tm
        prev_spec = pl.BlockSpec((1, SUBLANES, tf), lambda i, j: (i // tiles_per_seq, 0, j))
    sub = _tile(tm, FFN_SUB_TILE)
    if n_slab == 1:
        rows = sub
    assert sub % rows == 0
    nj = d_ff // tf
    return pl.pallas_call(
        functools.partial(_ffn_up_kernel, rows=rows, n_slab=n_slab, tiles_per_seq=tiles_per_seq, sub=sub),
        grid=(m // tm, nj),
        in_specs=[pl.BlockSpec((tm, d_model), lambda i, j: (i, 0)),
                  pl.BlockSpec((None, d_model, tf), lambda i, j: (layer, 0, j)),
                  pl.BlockSpec((None, d_model, tf), lambda i, j: (layer, 0, j)),
                  pl.BlockSpec((None, CONV_W, tf), lambda i, j: (layer, 0, j)),
                  pl.BlockSpec((None, 1, tf), lambda i, j: (layer, 0, j)),
                  prev_spec],
        out_specs=[pl.BlockSpec((tm, tf), lambda i, j: (i, j)),
                   pl.BlockSpec((n_slab, SUBLANES, tf), lambda i, j: (i, 0, j))],
        out_shape=[jax.ShapeDtypeStruct((m, d_ff), BF16),
                   jax.ShapeDtypeStruct((m // tm * n_slab, SUBLANES, d_ff), F32)],
        scratch_shapes=[pltpu.VMEM((SUBLANES, tf), F32),
                        pltpu.VMEM((nj, SUBLANES, tf), F32)],
        compiler_params=_cparams(2), name="ffn_up",
    )(h, wg, wu, conv_w, conv_b, prev)


def _conv_ffn(y, h, seq, prev_state, layer, wg, wu, conv_w, conv_b, wd, final_gain=None):
    n_seq = y.shape[0] // seq
    d_ff = wg.shape[-1]
    prev = jnp.concatenate([jnp.zeros((n_seq, SUBLANES - (CONV_W - 1), d_ff), F32), prev_state], axis=1)
    act, tails = _ffn_up(h, wg, wu, conv_w, conv_b, prev, layer=layer, seq=seq)
    y = _mm_res(act, wd, y, final_gain, layer=layer, norm_only=final_gain is not None)
    state = tails.reshape(n_seq, -1, SUBLANES, d_ff)[:, -1, SUBLANES - (CONV_W - 1):, :]
    return y, state


def kernel(x_prompt, x_sample, cache_mla_ckv, cache_mla_kpe, cache_swa_k, cache_swa_v, cache_diff_k,
           cache_diff_v, state_ffn_conv, norm_mix, norm_ffn, final_norm, mla_w_dq, mla_q_norm, mla_w_uq,
           mla_w_dkv, mla_kv_norm, mla_w_ukv, mla_w_o, swa_w_qkv, swa_sinks, swa_w_o, diff_w_qkv,
           diff_lambda_q1, diff_lambda_k1, diff_lambda_q2, diff_lambda_k2, diff_subln, diff_w_o,
           ffn_w_gate, ffn_w_up, ffn_conv_w, ffn_conv_b, ffn_w_down):
    bp, sp, d_model = x_prompt.shape
    bs, ss, _ = x_sample.shape
    depth = norm_mix.shape[0]
    past = cache_mla_ckv.shape[2]
    d_ff = ffn_w_gate.shape[2]
    q_rank = mla_w_dq.shape[2]
    kv_rank = mla_w_ukv.shape[1]

    yp = x_prompt.reshape(bp * sp, d_model)
    ys = x_sample.reshape(bs * ss, d_model)
    row_p = _tile(bp * sp, ROW_TILE)
    row_s = _tile(bs * ss, ROW_TILE)

    def tables(head_dim):
        return (_rope_tables(sp, 0, head_dim, row_p), _rope_tables(ss, past, head_dim, row_s))

    tab64, tab128 = tables(64), tables(128)
    outs_p = {k: [] for k in ("ckv", "kpe", "swk", "swv", "dk", "dv", "conv")}
    outs_s = {k: [] for k in ("ckv", "kpe", "swk", "swv", "dk", "dv", "conv")}

    ffn_w = (ffn_w_gate.astype(BF16), ffn_w_up.astype(BF16), ffn_conv_w, ffn_conv_b[:, None, :],
             ffn_w_down.astype(BF16))

    for i in range(depth):
        kind, j = i % N_MIXERS, i // N_MIXERS
        gain = norm_mix[i][None, :]
        ffn_gain = norm_ffn[i][None, :]
        if kind == 0:
            q_scale = (MLA_NOPE + MLA_ROPE) ** -0.5 * LOG2E
            pad = (-(q_rank + kv_rank + MLA_ROPE)) % LANES
            w_cat = jnp.concatenate([mla_w_dq[j], mla_w_dkv[j], jnp.zeros((d_model, pad), F32)],
                                    axis=1).astype(BF16)
            w_uq = mla_w_uq[j].reshape(q_rank, MLA_HEADS, MLA_NOPE + MLA_ROPE)
            w_uq = jnp.pad(w_uq, ((0, 0), (0, 0), (0, MLA_QK_PAD - MLA_NOPE - MLA_ROPE)))
            w_uq = w_uq.reshape(q_rank, MLA_HEADS * MLA_QK_PAD).astype(BF16)
            w_ukv = mla_w_ukv[j].reshape(kv_rank, MLA_HEADS, MLA_NOPE + MLA_V)
            wk = w_ukv[:, :, :MLA_NOPE].reshape(kv_rank, MLA_HEADS * MLA_NOPE).astype(BF16)
            wv = w_ukv[:, :, MLA_NOPE:].reshape(kv_rank, MLA_HEADS * MLA_V).astype(BF16)
            w_o = mla_w_o[j].astype(BF16)
            qn, kvn = mla_q_norm[j][None, :], mla_kv_norm[j][None, :]

            (cos_p, sin_p), (cos_s, sin_s) = tab64
            qa, ckv_p, kpe_p = _mla_down(yp, gain, w_cat, qn, kvn, cos_p, sin_p, q_rank, kv_rank)
            q = _mla_q_up(qa, w_uq, cos_p, sin_p, q_scale)
            k, vt = _mla_expand(ckv_p, kpe_p, wk, wv.T)
            o = _flash_attention(q, k, vt, batch=bp, seq=sp, n_kv_heads=MLA_HEADS, dk=MLA_QK_PAD,
                                 dv=MLA_V, groups=1, n_maps=1)
            yp, hp = _mm_res(o, w_o, yp, ffn_gain)

            qa, ckv_s, kpe_s = _mla_down(ys, gain, w_cat, qn, kvn, cos_s, sin_s, q_rank, kv_rank)
            q = _mla_q_up(qa, w_uq, cos_s, sin_s, q_scale)
            o = _mla_decode_attention(q, cache_mla_ckv.reshape(-1, kv_rank), cache_mla_kpe.reshape(-1, MLA_ROPE),
                                      ckv_s, kpe_s, wk, wv, layer=j, batch=bs, past=past, new=ss)
            ys, hs = _mm_res(o, w_o, ys, ffn_gain)
            outs_p["ckv"].append(ckv_p.reshape(bp, sp, kv_rank))
            outs_p["kpe"].append(kpe_p.reshape(bp, sp, MLA_ROPE))
            outs_s["ckv"].append(ckv_s.reshape(bs, ss, kv_rank))
            outs_s["kpe"].append(kpe_s.reshape(bs, ss, MLA_ROPE))
        elif kind == 1:
            qw, kw = SWA_HEADS * SWA_HEAD_DIM, SWA_KV_HEADS * SWA_HEAD_DIM
            w_qkv = swa_w_qkv[j].astype(BF16)
            w_o = swa_w_o[j].astype(BF16)
            sinks = swa_sinks[j]
            (cos_p, sin_p), (cos_s, sin_s) = tab64
            q_scale = SWA_HEAD_DIM ** -0.5 * LOG2E
            q, k, v, kb, vt = _qkv_proj(yp, gain, w_qkv, cos_p, sin_p, qw, kw, kw, SWA_HEAD_DIM,
                                        q_scale=q_scale, vt_head=(SWA_HEAD_DIM, LANES - SWA_HEAD_DIM),
                                        vt_blocked=False)
            o = _swa_attention(q, kb, vt, sinks, seq=sp)
            yp, hp = _mm_res(o, w_o, yp, ffn_gain)
            k3 = k.reshape(bp, sp, SWA_KV_HEADS, SWA_HEAD_DIM)
            v3 = v.reshape(bp, sp, SWA_KV_HEADS, SWA_HEAD_DIM)
            outs_p["swk"].append(k3[:, sp - WINDOW:])
            outs_p["swv"].append(v3[:, sp - WINDOW:])

            q, k, v, _, _ = _qkv_proj(ys, gain, w_qkv, cos_s, sin_s, qw, kw, kw, SWA_HEAD_DIM, q_scale=q_scale)
            kc = cache_swa_k[j].reshape(bs * WINDOW, kw)
            vc = cache_swa_v[j].reshape(bs * WINDOW, kw)
            o = _swa_decode_attention(q, kc, k, vc, v, sinks, nq=ss, na=WINDOW, nb=ss)
            ys, hs = _mm_res(o, w_o, ys, ffn_gain)
            k_all = jnp.concatenate([cache_swa_k[j], k.reshape(bs, ss, SWA_KV_HEADS, SWA_HEAD_DIM)], axis=1)
            v_all = jnp.concatenate([cache_swa_v[j], v.reshape(bs, ss, SWA_KV_HEADS, SWA_HEAD_DIM)], axis=1)
            outs_s["swk"].append(k_all[:, ss:])
            outs_s["swv"].append(v_all[:, ss:])
        else:
            lam_init = 0.8 - 0.6 * math.exp(-0.3 * i)
            q_scale = DIFF_HEAD_DIM ** -0.5 * LOG2E
            groups = DIFF_HEADS // DIFF_KV_HEADS
            qw = DIFF_HEADS * 2 * DIFF_HEAD_DIM
            kw = DIFF_KV_HEADS * 2 * DIFF_HEAD_DIM
            w_qkv = diff_w_qkv[j].astype(BF16)
            w_o = diff_w_o[j].astype(BF16)
            lam = jnp.stack([diff_lambda_q1[j], diff_lambda_k1[j], diff_lambda_q2[j], diff_lambda_k2[j]])
            subln = diff_subln[j][None, :]
            common = dict(n_kv_heads=DIFF_KV_HEADS, dk=DIFF_HEAD_DIM, dv=2 * DIFF_HEAD_DIM, groups=groups,
                          n_maps=2, lam=lam, subln=subln, lam_init=lam_init)
            (cos_p, sin_p), (cos_s, sin_s) = tab128
            q, k, v, kb, vt = _qkv_proj(yp, gain, w_qkv, cos_p, sin_p, qw, kw, kw, DIFF_HEAD_DIM,
                                        q_scale=q_scale, vt_head=(2 * DIFF_HEAD_DIM, ONES_ROWS))
            o = _flash_attention(q, kb, vt, batch=bp, seq=sp, **common)
            yp, hp = _mm_res(o, w_o, yp, ffn_gain)
            outs_p["dk"].append(k.reshape(bp, sp, DIFF_KV_HEADS, 2, DIFF_HEAD_DIM))
            outs_p["dv"].append(v.reshape(bp, sp, DIFF_KV_HEADS, 2 * DIFF_HEAD_DIM))

            q, k, v, _, _ = _qkv_proj(ys, gain, w_qkv, cos_s, sin_s, qw, kw, kw, DIFF_HEAD_DIM,
                                      q_scale=q_scale)
            o = _decode_attention(q, cache_diff_k[j].reshape(bs * past, kw),
                                  cache_diff_v[j].reshape(bs * past, kw), k, v,
                                  batch=bs, past=past, new=ss, **common)
            ys, hs = _mm_res(o, w_o, ys, ffn_gain)
            outs_s["dk"].append(k.reshape(bs, ss, DIFF_KV_HEADS, 2, DIFF_HEAD_DIM))
            outs_s["dv"].append(v.reshape(bs, ss, DIFF_KV_HEADS, 2 * DIFF_HEAD_DIM))

        final_gain = final_norm[None, :] if i == depth - 1 else None
        yp, conv_p = _conv_ffn(yp, hp, sp, jnp.zeros((bp, CONV_W - 1, d_ff), F32), i, *ffn_w, final_gain)
        ys, conv_s = _conv_ffn(ys, hs, ss, state_ffn_conv[i], i, *ffn_w, final_gain)
        outs_p["conv"].append(conv_p)
        outs_s["conv"].append(conv_s)

    y_prompt = yp.reshape(bp, sp, d_model)
    y_sample = ys.reshape(bs, ss, d_model)
    order = ("ckv", "kpe", "swk", "swv", "dk", "dv", "conv")
    stack = lambda xs: xs[0][None] if len(xs) == 1 else jnp.stack(xs)
    return (y_prompt, y_sample) + tuple(stack(outs_p[k]) for k in order) + tuple(
        stack(outs_s[k]) for k in order)
```
